```python
import math
import jax, jax.numpy as jnp
from jax import lax
import numpy as np


D_MODEL = 1024
BATCH = 4
SEQ = 4096
DEPTH = 2
DEC_BATCH = 128
DEC_SEQ = 4
PAST_LEN = 2048
PAGE_SIZE = 128

MIX = D_MODEL
MIX_A = MIX // 2
H_A = 4
DA = MIX_A // (2 * H_A)
DV_A = 2 * DA
MIX_B = MIX - MIX_A
H_B = 4
DB = MIX_B // H_B
HI = 8
DI = 64
TOPK_MAX = 256
QBLOCK = 128
MIX_C = MIX // 2
MIX_D = MIX - MIX_C
POOL_WINDOWS = (2, 4, 8, 16)
N_POOL_GROUPS = len(POOL_WINDOWS)
C_GROUP = MIX_C // N_POOL_GROUPS
POOL_BUF = max(POOL_WINDOWS) - 1
CHUNK = 128
D_GROUPS = 4
D_GROUP_W = MIX_D // D_GROUPS
D_FF = 4 * D_MODEL
N_EVEN = (DEPTH + 1) // 2
N_ODD = DEPTH // 2
ALPHA = (2 * DEPTH) ** 0.25
BETA = (8 * DEPTH) ** -0.25
EPS = 1e-5
E_SIZES = (2 * H_A * DA, 2 * H_A * DA, H_A * DV_A, H_B * DB, DB, DB, HI * DI, DI, HI)
E_COLS = sum(E_SIZES)
O_COLS = MIX_C + 2 * MIX_D

kernel_name = 'hybrid_diffattn_dsa_pool_sgu_step'


def _layernorm(x, g, b):
    xf = x.astype(jnp.float32)
    mu = jnp.mean(xf, -1, keepdims=True)
    var = jnp.mean(jnp.square(xf - mu), -1, keepdims=True)
    return ((xf - mu) * lax.rsqrt(var + EPS) * g + b).astype(x.dtype)


def _rmsnorm(x, g):
    xf = x.astype(jnp.float32)
    return (xf * lax.rsqrt(jnp.mean(xf * xf, -1, keepdims=True) + EPS) * g).astype(x.dtype)


def _split(h, sizes):
    cuts = [int(c) for c in np.cumsum(sizes)[:-1]]
    return jnp.split(h, cuts, axis=-1)


def _to_blocks(x, nblk):
    return jnp.moveaxis(x.reshape((x.shape[0], nblk, -1) + x.shape[2:]), 1, 0)


def _gather_pages(pool, page_table):
    g = pool[page_table]
    return g.reshape((page_table.shape[0], page_table.shape[1] * pool.shape[1]) + pool.shape[2:])


def _mlp(x, w1, w2):
    h = jax.nn.relu(x @ w1)
    return (h * h) @ w2


def _attn_block(q_a, q_b, q_i, w_i, q_pos, k1, k2, va, kb, vb, kidx, lam, subln_g, lam_init, topk):
    B, Q = q_a.shape[:2]
    L = k1.shape[1]
    neg = -jnp.inf
    k_pos = jnp.arange(L)
    causal = k_pos[None, :] <= q_pos[:, None]
    s1 = jnp.einsum('bqhd,bkhd->bhqk', q_a[..., :DA], k1, preferred_element_type=jnp.float32) * DA ** -0.5
    s2 = jnp.einsum('bqhd,bkhd->bhqk', q_a[..., DA:], k2, preferred_element_type=jnp.float32) * DA ** -0.5
    p1 = jax.nn.softmax(jnp.where(causal, s1, neg), axis=-1)
    p2 = jax.nn.softmax(jnp.where(causal, s2, neg), axis=-1)
    att = (p1 - lam * p2).astype(va.dtype)
    o_a = jnp.einsum('bhqk,bkhd->bqhd', att, va)
    o_a = _rmsnorm(o_a, subln_g) * (1.0 - lam_init)
    dots = jnp.einsum('bqhd,bkd->bqhk', q_i, kidx, preferred_element_type=jnp.float32) * DI ** -0.5
    score = jnp.einsum('bqh,bqhk->bqk', w_i.astype(jnp.float32), jax.nn.relu(dots))
    score = jnp.where(causal, score, neg)
    _, idx = lax.top_k(score, topk)
    sel_ok = idx <= q_pos[None, :, None]
    take = jax.vmap(lambda m, i: m[i])
    kb_sel = take(kb, idx)
    vb_sel = take(vb, idx)
    sb = jnp.einsum('bqhd,bqkd->bhqk', q_b, kb_sel, preferred_element_type=jnp.float32) * DB ** -0.5
    pb = jax.nn.softmax(jnp.where(sel_ok[:, None], sb, neg), axis=-1).astype(vb.dtype)
    o_b = jnp.einsum('bhqk,bqkd->bqhd', pb, vb_sel)
    return jnp.concatenate([o_a.reshape(B, Q, MIX_A), o_b.reshape(B, Q, MIX_B)], -1)


def _even_mixer(x, past_a, past_b, w_in, lam_p, subln_g, w_out, lam_init):
    B, T, _ = x.shape
    q_a, k_a, v_a, q_b, k_b, v_b, q_i, k_i, w_i = _split(x @ w_in, E_SIZES)
    new_a = jnp.concatenate([k_a.reshape(B, T, H_A, 2 * DA), v_a.reshape(B, T, H_A, DV_A)], -1)
    new_b = jnp.concatenate([k_b, v_b, k_i], -1)
    keys_a = new_a if past_a is None else jnp.concatenate([past_a.astype(new_a.dtype), new_a], 1)
    keys_b = new_b if past_b is None else jnp.concatenate([past_b.astype(new_b.dtype), new_b], 1)
    L = keys_a.shape[1]
    topk = min(TOPK_MAX, L // 4)
    k1, k2, va = keys_a[..., :DA], keys_a[..., DA:2 * DA], keys_a[..., 2 * DA:]
    kb, vb, kidx = keys_b[..., :DB], keys_b[..., DB:2 * DB], keys_b[..., 2 * DB:]
    lp = lam_p.astype(jnp.float32)
    lam = jnp.exp(jnp.sum(lp[0] * lp[1])) - jnp.exp(jnp.sum(lp[2] * lp[3])) + lam_init
    q_pos = (L - T) + jnp.arange(T)
    qb = min(QBLOCK, T)
    nblk = T // qb
    blocks = (_to_blocks(q_a.reshape(B, T, H_A, 2 * DA), nblk),
              _to_blocks(q_b.reshape(B, T, H_B, DB), nblk),
              _to_blocks(q_i.reshape(B, T, HI, DI), nblk),
              _to_blocks(w_i * HI ** -0.5, nblk),
              q_pos.reshape(nblk, qb))
    out = lax.map(lambda blk: _attn_block(*blk, k1, k2, va, kb, vb, kidx, lam, subln_g, lam_init, topk), blocks)
    out = jnp.moveaxis(out, 0, 1).reshape(B, T, MIX)
    return out @ w_out, new_a, new_b


def _odd_mixer(x, prev, start, w_in, w_pool, pool_scale, sgu_g, sgu_b, w_s, b_s, w_out):
    B, T, _ = x.shape
    xc, z = jnp.split(x @ w_in, [MIX_C], axis=-1)
    u, v = jnp.split(jax.nn.gelu(z), 2, axis=-1)
    ext = jnp.concatenate([prev.astype(xc.dtype), xc], 1)
    n_prev = prev.shape[1]
    cs = jnp.cumsum(ext.astype(jnp.float32), axis=1)
    cs = jnp.concatenate([jnp.zeros_like(cs[:, :1]), cs], 1)
    csg = cs.reshape(B, -1, N_POOL_GROUPS, C_GROUP)
    end = n_prev + 1 + jnp.arange(T)
    pos = start + jnp.arange(T)
    means = []
    for g, w in enumerate(POOL_WINDOWS):
        s = csg[:, end, g] - csg[:, end - w, g]
        cnt = jnp.minimum(w, pos + 1).astype(jnp.float32)
        means.append(s / cnt[None, :, None])
    pooled = jnp.stack(means, 2).astype(xc.dtype) - xc.reshape(B, T, N_POOL_GROUPS, C_GROUP)
    c_out = jnp.einsum('btgc,gcd->btgd', pooled, w_pool).reshape(B, T, MIX_C) * pool_scale
    new_prev = ext[:, -POOL_BUF:]
    vn = _layernorm(v, sgu_g, sgu_b)
    n = min(T, CHUNK)
    ws = jnp.where(jnp.tril(jnp.ones((n, n), bool)), w_s[:, :n, :n], 0)
    vc = vn.reshape(B, T // n, n, D_GROUPS, D_GROUP_W)
    s = jnp.einsum('gij,bkjgc->bkigc', ws, vc) + b_s[:, :n].T[None, None, :, :, None]
    d_out = u * s.reshape(B, T, MIX_D)
    y = jnp.concatenate([c_out, d_out], -1) @ w_out
    return y, new_prev, vn


def setup_inputs(seed: int = 0) -> dict:
    key = jax.random.key(seed)
    ks = jax.random.split(key, 24)
    n_pages = PAST_LEN // PAGE_SIZE
    n_used = DEC_BATCH * n_pages
    n_pool = n_used + max(1, n_used // 4)

    def nrm(k, shape, s):
        return jax.random.normal(k, shape, jnp.float32) * s

    x_prompt = nrm(ks[0], (BATCH, SEQ, D_MODEL), 1.0)
    x_sample = nrm(ks[1], (DEC_BATCH, DEC_SEQ, D_MODEL), 1.0)
    cache_a = nrm(ks[2], (N_EVEN, n_pool, PAGE_SIZE, H_A, 2 * DA + DV_A), 1.0)
    cache_b = nrm(ks[3], (N_EVEN, n_pool, PAGE_SIZE, 2 * DB + DI), 1.0)
    state_pool = nrm(ks[4], (N_ODD, DEC_BATCH, POOL_BUF, MIX_C), 1.0)
    page_table = jax.random.permutation(ks[5], n_pool)[:n_used].reshape(DEC_BATCH, n_pages).astype(jnp.int32)
    col_scale = jnp.concatenate([jnp.full((s,), BETA if j in (2, 5) else 1.0, jnp.float32)
                                 for j, s in enumerate(E_SIZES)])
    w_in_e = nrm(ks[6], (N_EVEN, D_MODEL, E_COLS), D_MODEL ** -0.5) * col_scale
    lam_e = nrm(ks[7], (N_EVEN, 4, DA), 0.1)
    subln_g = 1.0 + nrm(ks[8], (N_EVEN, 2 * DA), 0.05)
    w_out_e = nrm(ks[9], (N_EVEN, MIX, D_MODEL), MIX ** -0.5 * BETA)
    w_in_o = nrm(ks[10], (N_ODD, D_MODEL, O_COLS), D_MODEL ** -0.5)
    w_pool = nrm(ks[11], (N_ODD, N_POOL_GROUPS, C_GROUP, C_GROUP), C_GROUP ** -0.5)
    pool_scale = 1.0 + nrm(ks[12], (N_ODD, MIX_C), 0.05)
    sgu_g = 1.0 + nrm(ks[13], (N_ODD, MIX_D), 0.05)
    sgu_b = nrm(ks[14], (N_ODD, MIX_D), 0.02)
    w_s = nrm(ks[15], (N_ODD, D_GROUPS, CHUNK, CHUNK), CHUNK ** -0.5)
    b_s = 1.0 + nrm(ks[16], (N_ODD, D_GROUPS, CHUNK), 0.05)
    w_out_o = nrm(ks[17], (N_ODD, MIX, D_MODEL), MIX ** -0.5 * BETA)
    w_mlp1 = nrm(ks[18], (DEPTH, D_MODEL, D_FF), D_MODEL ** -0.5)
    w_mlp2 = nrm(ks[19], (DEPTH, D_FF, D_MODEL), D_FF ** -0.5 * BETA)
    ln_g = 1.0 + nrm(ks[20], (DEPTH, 2, D_MODEL), 0.05)
    ln_b = nrm(ks[21], (DEPTH, 2, D_MODEL), 0.02)
    return {'x_prompt': x_prompt, 'x_sample': x_sample, 'cache_a': cache_a, 'cache_b': cache_b,
            'state_pool': state_pool, 'page_table': page_table, 'w_in_e': w_in_e, 'lam_e': lam_e,
            'subln_g': subln_g, 'w_out_e': w_out_e, 'w_in_o': w_in_o, 'w_pool': w_pool,
            'pool_scale': pool_scale, 'sgu_g': sgu_g, 'sgu_b': sgu_b, 'w_s': w_s, 'b_s': b_s,
            'w_out_o': w_out_o, 'w_mlp1': w_mlp1, 'w_mlp2': w_mlp2, 'ln_g': ln_g, 'ln_b': ln_b}


def reference(x_prompt, x_sample, cache_a, cache_b, state_pool, page_table, w_in_e, lam_e, subln_g,
              w_out_e, w_in_o, w_pool, pool_scale, sgu_g, sgu_b, w_s, b_s, w_out_o, w_mlp1, w_mlp2,
              ln_g, ln_b):
    xp, xs = x_prompt, x_sample
    a_p, b_p, pool_p, a_s, b_s_new, pool_s, v_s = [], [], [], [], [], [], []
    for l in range(DEPTH):
        i = l // 2
        if l % 2 == 0:
            lam_init = 0.8 - 0.6 * math.exp(-0.3 * l)
            mp, na, nb = _even_mixer(xp, None, None, w_in_e[i], lam_e[i], subln_g[i], w_out_e[i], lam_init)
            ms, nas, nbs = _even_mixer(xs, _gather_pages(cache_a[i], page_table),
                                       _gather_pages(cache_b[i], page_table),
                                       w_in_e[i], lam_e[i], subln_g[i], w_out_e[i], lam_init)
            a_p.append(na); b_p.append(nb); a_s.append(nas); b_s_new.append(nbs)
        else:
            prev0 = jnp.zeros((xp.shape[0], POOL_BUF, MIX_C), xp.dtype)
            mp, npool, _ = _odd_mixer(xp, prev0, 0, w_in_o[i], w_pool[i], pool_scale[i], sgu_g[i],
                                      sgu_b[i], w_s[i], b_s[i], w_out_o[i])
            ms, npools, nvs = _odd_mixer(xs, state_pool[i], PAST_LEN, w_in_o[i], w_pool[i], pool_scale[i],
                                         sgu_g[i], sgu_b[i], w_s[i], b_s[i], w_out_o[i])
            pool_p.append(npool); pool_s.append(npools); v_s.append(nvs)
        xp = _layernorm(ALPHA * xp + mp, ln_g[l, 0], ln_b[l, 0])
        xp = _layernorm(ALPHA * xp + _mlp(xp, w_mlp1[l], w_mlp2[l]), ln_g[l, 1], ln_b[l, 1])
        xs = _layernorm(ALPHA * xs + ms, ln_g[l, 0], ln_b[l, 0])
        xs = _layernorm(ALPHA * xs + _mlp(xs, w_mlp1[l], w_mlp2[l]), ln_g[l, 1], ln_b[l, 1])
    return (xp, xs, jnp.stack(a_p), jnp.stack(b_p), jnp.stack(pool_p), jnp.stack(a_s),
            jnp.stack(b_s_new), jnp.stack(pool_s), jnp.stack(v_s))
```

```python
import functools
import math

import jax
import jax.numpy as jnp
from jax import lax
from jax.experimental import pallas as pl
from jax.experimental.pallas import tpu as pltpu

D_MODEL = 1024
H_A = 4
DA = 64
DV_A = 128
H_B = 4
DB = 128
HI = 8
DI = 64
TOPK_MAX = 256
MIX_C = 512
MIX_D = 512
POOL_WINDOWS = (2, 4, 8, 16)
C_GROUP = 128
POOL_BUF = 15
CHUNK = 128
D_GROUPS = 4
D_FF = 4096
DEPTH = 2
ALPHA = (2 * DEPTH) ** 0.25
EPS = 1e-5

BF = jnp.bfloat16
F32 = jnp.float32
NEG = -1e30
INT_MIN = -(2 ** 31)
VMEM_LIMIT_BYTES = 56 * 1024 * 1024


def _params(n_axes):
    return pltpu.CompilerParams(dimension_semantics=("arbitrary",) * n_axes,
                                vmem_limit_bytes=VMEM_LIMIT_BYTES)


def _nn(a, b):
    return jnp.dot(a, b, preferred_element_type=F32)


def _nt(a, b):
    return lax.dot_general(a, b, (((1,), (1,)), ((), ())), preferred_element_type=F32)


def _ln(z, g, b):
    mu = jnp.mean(z, -1, keepdims=True)
    d = z - mu
    var = jnp.mean(d * d, -1, keepdims=True)
    return d * lax.rsqrt(var + EPS) * g + b


def _const_spec(shape):
    nd = len(shape)
    return pl.BlockSpec(shape, lambda *_: (0,) * nd)


def _float_key(x):
    b = lax.bitcast_convert_type(x, jnp.int32)
    return jnp.where(b < 0, b ^ jnp.int32(0x7FFFFFFF), b)


def _proj_even_kernel(x_ref, wa_ref, wb_ref, wq_ref, ww_ref, qs_ref,
                      na_ref, nb_ref, abf_ref, bbf_ref, q_ref, wi_ref):
    x = x_ref[...].astype(BF)
    a = _nn(x, wa_ref[...])
    na_ref[...] = a
    abf_ref[...] = a.astype(BF)
    b = _nn(x, wb_ref[...])
    nb_ref[...] = b
    bbf_ref[...] = b.astype(BF)
    q_ref[...] = (_nn(x, wq_ref[...]) * qs_ref[...]).astype(BF)
    wi_ref[...] = _nn(x, ww_ref[...]) * (HI ** -0.5)


def _proj_even(x, wa, wb, wq, ww, qscale, tm=256):
    n = x.shape[0]
    tm = min(tm, n)
    row = lambda w: pl.BlockSpec((tm, w), lambda i: (i, 0))
    return pl.pallas_call(
        _proj_even_kernel,
        grid=(n // tm,),
        in_specs=[row(D_MODEL), _const_spec(wa.shape), _const_spec(wb.shape),
                  _const_spec(wq.shape), _const_spec(ww.shape), _const_spec(qscale.shape)],
        out_specs=[row(1024), row(320), row(1024), row(320), row(1536), row(128)],
        out_shape=[jax.ShapeDtypeStruct((n, 1024), F32), jax.ShapeDtypeStruct((n, 320), F32),
                   jax.ShapeDtypeStruct((n, 1024), BF), jax.ShapeDtypeStruct((n, 320), BF),
                   jax.ShapeDtypeStruct((n, 1536), BF), jax.ShapeDtypeStruct((n, 128), F32)],
        compiler_params=_params(1),
        name="proj_even",
    )(x, wa, wb, wq, ww, qscale)


def _lambda(lam_ref, lam_init):
    lp = lam_ref[...]
    return (jnp.exp(jnp.sum(lp[0:1] * lp[1:2], axis=-1, keepdims=True))
            - jnp.exp(jnp.sum(lp[2:3] * lp[3:4], axis=-1, keepdims=True)) + lam_init)


def _split_q12(q):
    qf = q.astype(F32)
    lane = lax.broadcasted_iota(jnp.int32, qf.shape, 1)
    return jnp.concatenate([jnp.where(lane < DA, qf, 0.0), jnp.where(lane >= DA, qf, 0.0)],
                           axis=0).astype(BF)


def _subln(o, lam, g, lam_init):
    r = o.shape[0] // 2
    d = o[:r] - lam * o[r:]
    ms = jnp.mean(d * d, -1, keepdims=True)
    return d * lax.rsqrt(ms + EPS) * g * (1.0 - lam_init)


def _select_threshold(keys_ref, nch, tk, ksel):
    rows = keys_ref.shape[0]

    def count(cand, strict):
        def body(c, acc):
            kk = keys_ref[:, pl.ds(pl.multiple_of(c * tk, tk), tk)]
            hit = (kk > cand) if strict else (kk >= cand)
            return acc + jnp.where(hit, 1.0, 0.0)
        acc = lax.fori_loop(0, nch, body, jnp.zeros((rows, tk), F32))
        return jnp.sum(acc, -1, keepdims=True)

    c0 = count(jnp.zeros((rows, 1), jnp.int32), False)
    t0 = jnp.where(c0 >= ksel, jnp.int32(0), jnp.int32(INT_MIN))

    def bit_body(it, t):
        cand = t | lax.shift_left(jnp.int32(1), jnp.int32(30) - it)
        return jnp.where(count(cand, False) >= ksel, cand, t)

    t = lax.fori_loop(0, 31, bit_body, t0)
    need = ksel - count(t, True)
    return t, need


def _selected(kk, t, need, eq_before, tri):
    eq = jnp.where(kk == t, 1.0, 0.0)
    rank = eq_before + _nn(eq.astype(BF), tri)
    tie_taken = jnp.where(rank <= need, eq, 0.0)
    return jnp.where(kk > t, 1.0, tie_taken), jnp.sum(eq, -1, keepdims=True)


def _upper_tri(n):
    r = lax.broadcasted_iota(jnp.int32, (n, n), 0)
    c = lax.broadcasted_iota(jnp.int32, (n, n), 1)
    return jnp.where(r <= c, 1.0, 0.0).astype(BF)


def _diff_kernel(lam_ref, g_ref, q_ref, a_ref, o_ref, *, tq, lam_init):
    i = pl.program_id(1)
    lam = _lambda(lam_ref, lam_init)
    row = lax.broadcasted_iota(jnp.int32, (2 * tq, tq), 0)
    col = lax.broadcasted_iota(jnp.int32, (2 * tq, tq), 1)
    diag_ok = col <= jnp.where(row >= tq, row - tq, row)
    for h in range(H_A):
        qq = _split_q12(q_ref[:, h * 128:(h + 1) * 128])

        def step(c, carry, masked, h=h, qq=qq):
            m, l, acc = carry
            r0 = pl.multiple_of(c * tq, tq)
            k = a_ref[pl.ds(r0, tq), h * 256:h * 256 + 128]
            v = a_ref[pl.ds(r0, tq), h * 256 + 128:(h + 1) * 256]
            s = _nt(qq, k)
            if masked:
                s = jnp.where(diag_ok, s, NEG)
            m_new = jnp.maximum(m, jnp.max(s, -1, keepdims=True))
            p = jnp.exp(s - m_new)
            alpha = jnp.exp(m - m_new)
            l = alpha * l + jnp.sum(p, -1, keepdims=True)
            acc = alpha * acc + _nn(p.astype(BF), v)
            return m_new, l, acc

        init = (jnp.full((2 * tq, 1), NEG, F32), jnp.zeros((2 * tq, 1), F32),
                jnp.zeros((2 * tq, DV_A), F32))
        carry = lax.fori_loop(0, i, lambda c, cr: step(c, cr, False), init)
        _, l, acc = step(i, carry, True)
        o = _subln(acc / l, lam, g_ref[...], lam_init)
        o_ref[:, h * 128:(h + 1) * 128] = o.astype(o_ref.dtype)


def _diff_attn_prompt(lam_e, g, q, a_bf, batch, seq, lam_init, tq=256):
    nq = seq // tq
    return pl.pallas_call(
        functools.partial(_diff_kernel, tq=tq, lam_init=lam_init),
        grid=(batch, nq),
        in_specs=[_const_spec(lam_e.shape), _const_spec(g.shape),
                  pl.BlockSpec((tq, 512), lambda b, i: (b * nq + i, 0)),
                  pl.BlockSpec((seq, 1024), lambda b, i: (b, 0))],
        out_specs=pl.BlockSpec((tq, 512), lambda b, i: (b * nq + i, 0)),
        out_shape=jax.ShapeDtypeStruct((batch * seq, 512), BF),
        compiler_params=_params(2),
        name="diff_attn_prompt",
    )(lam_e, g, q, a_bf)


def _sparse_kernel(qb_ref, qi_ref, wi_ref, b_ref, o_ref, keys_ref, *, tq, ksel):
    i = pl.program_id(1)
    nch = i + 1
    rowg = i * tq + lax.broadcasted_iota(jnp.int32, (tq, tq), 0)
    coll = lax.broadcasted_iota(jnp.int32, (tq, tq), 1)
    qi = qi_ref[...]
    w = wi_ref[...]

    def score_chunk(c, _):
        r0 = pl.multiple_of(c * tq, tq)
        kidx = b_ref[pl.ds(r0, tq), 2 * DB:2 * DB + DI]
        acc = jnp.zeros((tq, tq), F32)
        for h in range(HI):
            d = _nt(qi[:, h * DI:(h + 1) * DI], kidx)
            acc = acc + w[:, h:h + 1] * jnp.maximum(d, 0.0)
        acc = jnp.where(c * tq + coll <= rowg, acc, -jnp.inf)
        keys_ref[:, pl.ds(r0, tq)] = _float_key(acc)
        return 0

    lax.fori_loop(0, nch, score_chunk, 0)
    t, need = _select_threshold(keys_ref, nch, tq, ksel)

    tri = _upper_tri(tq)
    qb = qb_ref[...]
    q4 = jnp.concatenate([qb[:, h * DB:(h + 1) * DB] for h in range(H_B)], axis=0)

    def attend(c, carry):
        m, l, acc, eq_before = carry
        r0 = pl.multiple_of(c * tq, tq)
        kk = keys_ref[:, pl.ds(r0, tq)]
        sel, n_eq = _selected(kk, t, need, eq_before, tri)
        sel = jnp.where(c * tq + coll <= rowg, sel, 0.0)
        sel4 = jnp.concatenate([sel] * H_B, axis=0) > 0.5
        s = _nt(q4, b_ref[pl.ds(r0, tq), 0:DB])
        m_new = jnp.maximum(m, jnp.max(jnp.where(sel4, s, NEG), -1, keepdims=True))
        p = jnp.where(sel4, jnp.exp(s - m_new), 0.0)
        alpha = jnp.exp(m - m_new)
        l = alpha * l + jnp.sum(p, -1, keepdims=True)
        acc = alpha * acc + _nn(p.astype(BF), b_ref[pl.ds(r0, tq), DB:2 * DB])
        return m_new, l, acc, eq_before + n_eq

    init = (jnp.full((H_B * tq, 1), NEG, F32), jnp.zeros((H_B * tq, 1), F32),
            jnp.zeros((H_B * tq, DB), F32), jnp.zeros((tq, 1), F32))
    _, l, acc, _ = lax.fori_loop(0, nch, attend, init)
    o = acc / l
    for h in range(H_B):
        o_ref[:, h * DB:(h + 1) * DB] = o[h * tq:(h + 1) * tq].astype(o_ref.dtype)


def _sparse_attn_prompt(q, wi, b_bf, batch, seq, ksel, tq=128):
    nq = seq // tq
    return pl.pallas_call(
        functools.partial(_sparse_kernel, tq=tq, ksel=ksel),
        grid=(batch, nq),
        in_specs=[pl.BlockSpec((tq, 512), lambda b, i: (b * nq + i, 1)),
                  pl.BlockSpec((tq, 512), lambda b, i: (b * nq + i, 2)),
                  pl.BlockSpec((tq, 128), lambda b, i: (b * nq + i, 0)),
                  pl.BlockSpec((seq, 320), lambda b, i: (b, 0))],
        out_specs=pl.BlockSpec((tq, 512), lambda b, i: (b * nq + i, 0)),
        out_shape=jax.ShapeDtypeStruct((batch * seq, 512), BF),
        scratch_shapes=[pltpu.VMEM((tq, seq), jnp.int32)],
        compiler_params=_params(2),
        name="sparse_attn_prompt",
    )(q, q, wi, b_bf)


def _sample_even_kernel(pt_ref, lam_ref, g_ref, q_ref, wi_ref, an_ref, bn_ref, *rest,
                        npg, page, ksel, lam_init):
    del pt_ref
    a_pages = rest[:npg]
    b_pages = rest[npg:2 * npg]
    o_ref = rest[2 * npg]
    keys_ref = rest[2 * npg + 1]
    r = q_ref.shape[0]
    nblk = npg + 1
    lam = _lambda(lam_ref, lam_init)
    q = q_ref[...]
    an = an_ref[...]
    bn = bn_ref[...]
    zpad = jnp.zeros((page - r, 128), F32)

    def new_block(x):
        return jnp.concatenate([x, zpad[:, :x.shape[1]]], axis=0).astype(BF)

    def new_ok(rows):
        tok = lax.broadcasted_iota(jnp.int32, (rows, page), 0) % r
        return lax.broadcasted_iota(jnp.int32, (rows, page), 1) <= tok

    ok2 = new_ok(2 * r)
    for h in range(H_A):
        qq = _split_q12(q[:, h * 128:(h + 1) * 128])
        ks = [a_pages[p][:, h * 256:h * 256 + 128].astype(BF) for p in range(npg)]
        ks.append(new_block(an[:, h * 256:h * 256 + 128]))
        vs = [a_pages[p][:, h * 256 + 128:(h + 1) * 256].astype(BF) for p in range(npg)]
        vs.append(new_block(an[:, h * 256 + 128:(h + 1) * 256]))
        ss = [_nt(qq, k) for k in ks]
        ss[npg] = jnp.where(ok2, ss[npg], NEG)
        m = functools.reduce(jnp.maximum, ss)
        m = jnp.max(m, -1, keepdims=True)
        ps = [jnp.exp(s - m) for s in ss]
        l = jnp.sum(functools.reduce(lambda x, y: x + y, ps), -1, keepdims=True)
        acc = functools.reduce(lambda x, y: x + y, [_nn(p.astype(BF), v) for p, v in zip(ps, vs)])
        o_ref[:, h * 128:(h + 1) * 128] = _subln(acc / l, lam, g_ref[...], lam_init)

    qi = jnp.concatenate([q[:, 1024 + h * DI:1024 + (h + 1) * DI] for h in range(HI)], axis=0).astype(BF)
    w = wi_ref[...]
    wcol = jnp.concatenate([w[:, h:h + 1] for h in range(HI)], axis=0)
    ok1 = new_ok(r)
    for p in range(nblk):
        kidx = (b_pages[p][:, 2 * DB:2 * DB + DI].astype(BF) if p < npg
                else new_block(bn[:, 2 * DB:2 * DB + DI]))
        d = jnp.maximum(_nt(qi, kidx), 0.0) * wcol
        sc = d[0:r]
        for h in range(1, HI):
            sc = sc + d[h * r:(h + 1) * r]
        if p == npg:
            sc = jnp.where(ok1, sc, -jnp.inf)
        keys_ref[:, p * page:(p + 1) * page] = _float_key(sc)
    t, need = _select_threshold(keys_ref, nblk, page, ksel)

    tri = _upper_tri(page)
    q4 = jnp.concatenate([q[:, 512 + h * DB:512 + (h + 1) * DB] for h in range(H_B)], axis=0).astype(BF)
    eq_before = jnp.zeros((r, 1), F32)
    ss, sels, vs = [], [], []
    for p in range(nblk):
        kk = keys_ref[:, p * page:(p + 1) * page]
        sel, n_eq = _selected(kk, t, need, eq_before, tri)
        eq_before = eq_before + n_eq
        if p == npg:
            sel = jnp.where(ok1, sel, 0.0)
        kb = b_pages[p][:, 0:DB].astype(BF) if p < npg else new_block(bn[:, 0:DB])
        vs.append(b_pages[p][:, DB:2 * DB].astype(BF) if p < npg else new_block(bn[:, DB:2 * DB]))
        sels.append(jnp.concatenate([sel] * H_B, axis=0) > 0.5)
        ss.append(_nt(q4, kb))
    m = functools.reduce(jnp.maximum, [jnp.where(sl, s, NEG) for sl, s in zip(sels, ss)])
    m = jnp.max(m, -1, keepdims=True)
    ps = [jnp.where(sl, jnp.exp(s - m), 0.0) for sl, s in zip(sels, ss)]
    l = jnp.sum(functools.reduce(lambda x, y: x + y, ps), -1, keepdims=True)
    acc = functools.reduce(lambda x, y: x + y, [_nn(p.astype(BF), v) for p, v in zip(ps, vs)])
    o = acc / l
    for h in range(H_B):
        o_ref[:, 512 + h * DB:512 + (h + 1) * DB] = o[h * r:(h + 1) * r]


def _sample_even(page_table, lam_e, g, qs, wis, anew, bnew, cache_a2, cache_b2, ksel, lam_init):
    nreq, npg = page_table.shape
    page = cache_a2.shape[1]
    r = qs.shape[1]
    req = lambda w: pl.BlockSpec((None, r, w), lambda i, pt: (i, 0, 0))

    def page_spec(w, p):
        return pl.BlockSpec((None, page, w), lambda i, pt, p=p: (pt[i, p], 0, 0))

    in_specs = [_const_spec(lam_e.shape), _const_spec(g.shape), req(1536), req(128), req(1024), req(320)]
    in_specs += [page_spec(1024, p) for p in range(npg)]
    in_specs += [page_spec(320, p) for p in range(npg)]
    grid_spec = pltpu.PrefetchScalarGridSpec(
        num_scalar_prefetch=1, grid=(nreq,), in_specs=in_specs,
        out_specs=pl.BlockSpec((None, r, 1024), lambda i, pt: (i, 0, 0)),
        scratch_shapes=[pltpu.VMEM((r, (npg + 1) * page), jnp.int32)])
    return pl.pallas_call(
        functools.partial(_sample_even_kernel, npg=npg, page=page, ksel=ksel, lam_init=lam_init),
        grid_spec=grid_spec,
        out_shape=jax.ShapeDtypeStruct((nreq, r, 1024), F32),
        compiler_params=_params(1),
        name="sample_even",
    )(page_table, lam_e, g, qs, wis, anew, bnew, *([cache_a2] * npg), *([cache_b2] * npg))


def _outproj_kernel(*refs, n_lhs):
    lhs = refs[:n_lhs]
    w_ref, x_ref, g_ref, b_ref, o_ref = refs[n_lhs:]
    y = None
    k0 = 0
    for a_ref in lhs:
        kw = a_ref.shape[1]
        part = _nn(a_ref[...].astype(BF), w_ref[k0:k0 + kw, :])
        y = part if y is None else y + part
        k0 += kw
    o_ref[...] = _ln(ALPHA * x_ref[...] + y, g_ref[...], b_ref[...])


def _outproj_res_ln(lhs, w, x, g, b, tm=256):
    n = x.shape[0]
    tm = min(tm, n)
    row = lambda wd: pl.BlockSpec((tm, wd), lambda i: (i, 0))
    return pl.pallas_call(
        functools.partial(_outproj_kernel, n_lhs=len(lhs)),
        grid=(n // tm,),
        in_specs=[row(a.shape[1]) for a in lhs] + [_const_spec(w.shape), row(D_MODEL),
                                                   _const_spec(g.shape), _const_spec(b.shape)],
        out_specs=row(D_MODEL),
        out_shape=jax.ShapeDtypeStruct((n, D_MODEL), F32),
        compiler_params=_params(1),
        name="outproj_res_ln",
    )(*lhs, w, x, g, b)


def _mlp_kernel(x_ref, w1_ref, w2_ref, g_ref, b_ref, o_ref, acc_ref, *, ck):
    x = x_ref[...]
    xb = x.astype(BF)
    for c in range(D_FF // ck):
        h = jnp.maximum(_nn(xb, w1_ref[:, c * ck:(c + 1) * ck]), 0.0)
        part = _nn((h * h).astype(BF), w2_ref[c * ck:(c + 1) * ck, :])
        if c == 0:
            acc_ref[...] = part
        else:
            acc_ref[...] += part
    o_ref[...] = _ln(ALPHA * x + acc_ref[...], g_ref[...], b_ref[...])


def _mlp_res_ln(x, w1, w2, g, b, tm=512, ck=512):
    n = x.shape[0]
    tm = min(tm, n)
    row = pl.BlockSpec((tm, D_MODEL), lambda i: (i, 0))
    resident = lambda s: pl.BlockSpec(s, lambda i: (0, 0), pipeline_mode=pl.Buffered(1))
    return pl.pallas_call(
        functools.partial(_mlp_kernel, ck=ck),
        grid=(n // tm,),
        in_specs=[row, resident(w1.shape), resident(w2.shape), _const_spec(g.shape), _const_spec(b.shape)],
        out_specs=row,
        out_shape=jax.ShapeDtypeStruct((n, D_MODEL), F32),
        scratch_shapes=[pltpu.VMEM((tm, D_MODEL), F32)],
        compiler_params=_params(1),
        name="mlp_res_ln",
    )(x, w1, w2, g, b)


def _gelu(x):
    return 0.5 * x * (1.0 + jnp.tanh(math.sqrt(2.0 / math.pi) * (x + 0.044715 * (x * x * x))))


def _proj_odd_kernel(x_ref, w_ref, g_ref, b_ref, xc_ref, u_ref, vn_ref):
    h = _nn(x_ref[...].astype(BF), w_ref[...])
    xc_ref[...] = h[:, :MIX_C]
    u_ref[...] = _gelu(h[:, MIX_C:MIX_C + MIX_D])
    vn_ref[...] = _ln(_gelu(h[:, MIX_C + MIX_D:]), g_ref[...], b_ref[...])


def _proj_odd(x, w, g, b, tm=256):
    n = x.shape[0]
    tm = min(tm, n)
    row = lambda wd: pl.BlockSpec((tm, wd), lambda i: (i, 0))
    return pl.pallas_call(
        _proj_odd_kernel,
        grid=(n // tm,),
        in_specs=[row(D_MODEL), _const_spec(w.shape), _const_spec(g.shape), _const_spec(b.shape)],
        out_specs=[row(512), row(512), row(512)],
        out_shape=[jax.ShapeDtypeStruct((n, 512), F32)] * 3,
        compiler_params=_params(1),
        name="proj_odd",
    )(x, w, g, b)


def _pool_sgu_kernel(prev_ref, halo_ref, xc_ref, u_ref, vn_ref, wp_ref, sc_ref, ws_ref, bs_ref,
                     o_ref, ext_ref, *, start):
    t = pl.program_id(1)
    hal = prev_ref.shape[0]
    ext_ref[0:hal, :] = jnp.where(t == 0, prev_ref[...], halo_ref[...])
    ext_ref[hal:hal + CHUNK, :] = xc_ref[...]
    pos = start + t * CHUNK + lax.broadcasted_iota(jnp.int32, (CHUNK, 1), 0)
    for g, w in enumerate(POOL_WINDOWS):
        gs = slice(g * C_GROUP, (g + 1) * C_GROUP)
        acc = ext_ref[hal:hal + CHUNK, gs]
        for s in range(1, w):
            acc = acc + ext_ref[hal - s:hal - s + CHUNK, gs]
        cnt = jnp.minimum(w, pos + 1).astype(F32)
        pooled = acc / cnt - xc_ref[:, gs]
        c = _nn(pooled.astype(BF), wp_ref[g]) * sc_ref[:, gs]
        o_ref[:, gs] = c.astype(o_ref.dtype)
    r = lax.broadcasted_iota(jnp.int32, (CHUNK, CHUNK), 0)
    cc = lax.broadcasted_iota(jnp.int32, (CHUNK, CHUNK), 1)
    for g in range(D_GROUPS):
        gs = slice(g * 128, (g + 1) * 128)
        ws = jnp.where(r >= cc, ws_ref[g], 0.0).astype(BF)
        s = _nn(ws, vn_ref[:, gs].astype(BF)) + bs_ref[:, g:g + 1]
        o_ref[:, MIX_C + g * 128:MIX_C + (g + 1) * 128] = (u_ref[:, gs] * s).astype(o_ref.dtype)


def _pool_sgu_prompt(prev16, xc, u, vn, wp, scale, ws, bs_t, batch, seq, start):
    nt = seq // CHUNK
    hal = prev16.shape[1]
    per = CHUNK // hal
    row = pl.BlockSpec((CHUNK, 512), lambda b, t: (b * nt + t, 0))
    return pl.pallas_call(
        functools.partial(_pool_sgu_kernel, start=start),
        grid=(batch, nt),
        in_specs=[pl.BlockSpec((None, hal, 512), lambda b, t: (b, 0, 0)),
                  pl.BlockSpec((hal, 512), lambda b, t: (jnp.maximum((b * nt + t) * per - 1, 0), 0)),
                  row, row, row, _const_spec(wp.shape), _const_spec(scale.shape),
                  _const_spec(ws.shape), _const_spec(bs_t.shape)],
        out_specs=pl.BlockSpec((CHUNK, 1024), lambda b, t: (b * nt + t, 0)),
        out_shape=jax.ShapeDtypeStruct((batch * seq, 1024), BF),
        scratch_shapes=[pltpu.VMEM((hal + CHUNK, 512), F32)],
        compiler_params=_params(2),
        name="pool_sgu_prompt",
    )(prev16, xc, xc, u, vn, wp, scale, ws, bs_t)


def _pool_sgu_sample_kernel(ws_ref, bs_ref, prev_ref, xc_ref, u_ref, vn_ref, wp_ref, sc_ref, o_ref,
                            *, start):
    nprev = prev_ref.shape[0]
    ntok = xc_ref.shape[0]
    for t in range(ntok):
        for g, w in enumerate(POOL_WINDOWS):
            gs = slice(g * C_GROUP, (g + 1) * C_GROUP)
            acc = None
            for s in range(w):
                j = nprev + t - s
                slab = prev_ref[j, :, gs] if j < nprev else xc_ref[j - nprev, :, gs]
                acc = slab if acc is None else acc + slab
            cnt = float(min(w, start + t + 1))
            pooled = acc / cnt - xc_ref[t, :, gs]
            o_ref[t, :, gs] = _nn(pooled.astype(BF), wp_ref[g]) * sc_ref[:, gs]
        for g in range(D_GROUPS):
            gs = slice(g * 128, (g + 1) * 128)
            s = None
            for j in range(t + 1):
                term = ws_ref[(g * ntok + t) * ntok + j] * vn_ref[j, :, gs]
                s = term if s is None else s + term
            s = s + bs_ref[g * ntok + t]
            o_ref[t, :, MIX_C + g * 128:MIX_C + (g + 1) * 128] = u_ref[t, :, gs] * s


def _pool_sgu_sample(ws_small, bs_small, prev_t, xc_t, u_t, vn_t, wp, scale, start):
    ntok, nreq, _ = xc_t.shape
    smem = pl.BlockSpec(memory_space=pltpu.SMEM)
    return pl.pallas_call(
        functools.partial(_pool_sgu_sample_kernel, start=start),
        grid=(1,),
        in_specs=[smem, smem, _const_spec(prev_t.shape), _const_spec(xc_t.shape), _const_spec(u_t.shape),
                  _const_spec(vn_t.shape), _const_spec(wp.shape), _const_spec(scale.shape)],
        out_specs=_const_spec((ntok, nreq, 1024)),
        out_shape=jax.ShapeDtypeStruct((ntok, nreq, 1024), F32),
        compiler_params=_params(1),
        name="pool_sgu_sample",
    )(ws_small, bs_small, prev_t, xc_t, u_t, vn_t, wp, scale)


def _even_weights(w_in):
    q_a, k_a, v_a, q_b, k_b, v_b, q_i, k_i, w_i = jnp.split(
        w_in, [512, 1024, 1536, 2048, 2176, 2304, 2816, 2880], axis=1)
    wa = jnp.concatenate([k_a.reshape(D_MODEL, H_A, 2 * DA), v_a.reshape(D_MODEL, H_A, DV_A)],
                         -1).reshape(D_MODEL, H_A * (2 * DA + DV_A))
    wb = jnp.concatenate([k_b, v_b, k_i], 1)
    wq = jnp.concatenate([q_a, q_b, q_i], 1)
    ww = jnp.concatenate([w_i, jnp.zeros((D_MODEL, 128 - HI), w_in.dtype)], 1)
    qscale = jnp.concatenate([jnp.full((512,), DA ** -0.5, F32), jnp.full((512,), DB ** -0.5, F32),
                              jnp.full((512,), DI ** -0.5, F32)]).reshape(1, 1536)
    return wa.astype(BF), wb.astype(BF), wq.astype(BF), ww.astype(BF), qscale


def _pad_rows(x, rows):
    return jnp.pad(x, ((0, 0), (0, rows - x.shape[1]), (0, 0)))


def kernel(x_prompt, x_sample, cache_a, cache_b, state_pool, page_table, w_in_e, lam_e, subln_g, w_out_e,
           w_in_o, w_pool, pool_scale, sgu_g, sgu_b, w_s, b_s, w_out_o, w_mlp1, w_mlp2, ln_g, ln_b):
    batch, seq, _ = x_prompt.shape
    nreq, ntok, _ = x_sample.shape
    npg = page_table.shape[1]
    page = cache_a.shape[2]
    past = npg * page
    xp = x_prompt.reshape(batch * seq, D_MODEL)
    xs = x_sample.reshape(nreq * ntok, D_MODEL)
    outs = {k: [] for k in ("a_p", "b_p", "pool_p", "a_s", "b_s", "pool_s", "v_s")}
    rpad = 8

    for l in range(DEPTH):
        i = l // 2
        row2 = lambda v: v.reshape(1, -1)
        if l % 2 == 0:
            lam_init = 0.8 - 0.6 * math.exp(-0.3 * l)
            wa, wb, wq, ww, qscale = _even_weights(w_in_e[i])
            g = row2(subln_g[i])
            w_out = w_out_e[i].astype(BF)
            na, nb, abf, bbf, q, wi = _proj_even(xp, wa, wb, wq, ww, qscale)
            o_a = _diff_attn_prompt(lam_e[i], g, q, abf, batch, seq, lam_init)
            o_b = _sparse_attn_prompt(q, wi, bbf, batch, seq, min(TOPK_MAX, seq // 4))
            mix_p = [o_a, o_b]
            outs["a_p"].append(na.reshape(batch, seq, H_A, 2 * DA + DV_A))
            outs["b_p"].append(nb.reshape(batch, seq, 2 * DB + DI))
            nas, nbs, _, _, qs, wis = _proj_even(xs, wa, wb, wq, ww, qscale)
            o_s = _sample_even(
                page_table, lam_e[i], g,
                _pad_rows(qs.astype(F32).reshape(nreq, ntok, 1536), rpad),
                _pad_rows(wis.reshape(nreq, ntok, 128), rpad),
                _pad_rows(nas.reshape(nreq, ntok, 1024), rpad),
                _pad_rows(nbs.reshape(nreq, ntok, 320), rpad),
                cache_a[i].reshape(-1, page, H_A * (2 * DA + DV_A)), cache_b[i],
                min(TOPK_MAX, (past + ntok) // 4), lam_init)
            mix_s = [o_s[:, :ntok].reshape(nreq * ntok, 1024)]
            outs["a_s"].append(nas.reshape(nreq, ntok, H_A, 2 * DA + DV_A))
            outs["b_s"].append(nbs.reshape(nreq, ntok, 2 * DB + DI))
        else:
            w_in = w_in_o[i].astype(BF)
            w_out = w_out_o[i].astype(BF)
            wp = w_pool[i].astype(BF)
            scale = row2(pool_scale[i])
            sg, sb = row2(sgu_g[i]), row2(sgu_b[i])
            xc, u, vn = _proj_odd(xp, w_in, sg, sb)
            prev16 = jnp.zeros((batch, POOL_BUF + 1, MIX_C), F32)
            mix_p = [_pool_sgu_prompt(prev16, xc, u, vn, wp, scale, w_s[i], b_s[i].T, batch, seq, 0)]
            outs["pool_p"].append(xc.reshape(batch, seq, MIX_C)[:, seq - POOL_BUF:])
            xcs, us, vns = _proj_odd(xs, w_in, sg, sb)
            tmaj = lambda v: v.reshape(nreq, ntok, -1).transpose(1, 0, 2)
            m_t = _pool_sgu_sample(
                w_s[i][:, :ntok, :ntok].reshape(-1), b_s[i][:, :ntok].reshape(-1),
                state_pool[i].transpose(1, 0, 2), tmaj(xcs), tmaj(us), tmaj(vns), wp, scale, past)
            mix_s = [m_t.transpose(1, 0, 2).reshape(nreq * ntok, 1024)]
            ext = jnp.concatenate([state_pool[i], xcs.reshape(nreq, ntok, MIX_C)], 1)
            outs["pool_s"].append(ext[:, ext.shape[1] - POOL_BUF:])
            outs["v_s"].append(vns.reshape(nreq, ntok, MIX_D))
        g0, b0, g1, b1 = row2(ln_g[l, 0]), row2(ln_b[l, 0]), row2(ln_g[l, 1]), row2(ln_b[l, 1])
        w1, w2 = w_mlp1[l].astype(BF), w_mlp2[l].astype(BF)
        xp = _outproj_res_ln(mix_p, w_out, xp, g0, b0)
        xp = _mlp_res_ln(xp, w1, w2, g1, b1)
        xs = _outproj_res_ln(mix_s, w_out, xs, g0, b0)
        xs = _mlp_res_ln(xs, w1, w2, g1, b1)

    st = lambda k: jnp.stack(outs[k])
    return (xp.reshape(batch, seq, D_MODEL), xs.reshape(nreq, ntok, D_MODEL), st("a_p"), st("b_p"),
            st("pool_p"), st("a_s"), st("b_s"), st("pool_s"), st("v_s"))
```

```python
import functools
import math

import jax
import jax.numpy as jnp
from jax import lax
from jax.experimental import pallas as pl
from jax.experimental.pallas import tpu as pltpu

D_MODEL = 1024
H_A = 4
DA = 64
DV_A = 128
H_B = 4
DB = 128
HI = 8
DI = 64
TOPK_MAX = 256
MIX_C = 512
MIX_D = 512
POOL_WINDOWS = (2, 4, 8, 16)
C_GROUP = 128
POOL_BUF = 15
CHUNK = 128
D_GROUPS = 4
D_FF = 4096
DEPTH = 2
ALPHA = (2 * DEPTH) ** 0.25
EPS = 1e-5

BF = jnp.bfloat16
F32 = jnp.float32
NEG = -1e30
INT_MIN = -(2 ** 31)
FLT_LOWEST = -3.4028234663852886e38
MANY = 1e9
VMEM_LIMIT_BYTES = 56 * 1024 * 1024


def _params(n_axes):
    return pltpu.CompilerParams(dimension_semantics=("arbitrary",) * n_axes,
                                vmem_limit_bytes=VMEM_LIMIT_BYTES)


def _nn(a, b):
    return jnp.dot(a, b, preferred_element_type=F32)


def _nt(a, b):
    return lax.dot_general(a, b, (((1,), (1,)), ((), ())), preferred_element_type=F32)


def _ln(z, g, b):
    mu = jnp.mean(z, -1, keepdims=True)
    d = z - mu
    var = jnp.mean(d * d, -1, keepdims=True)
    return d * lax.rsqrt(var + EPS) * g + b


def _const_spec(shape):
    nd = len(shape)
    return pl.BlockSpec(shape, lambda *_: (0,) * nd)


def _key_to_float(k):
    return lax.bitcast_convert_type(jnp.where(k < 0, k ^ jnp.int32(0x7FFFFFFF), k), F32)


def _kth_largest(count, shape, ksel):
    n_finite = count(jnp.full(shape, -jnp.inf, F32), True)
    c0 = count(jnp.zeros(shape, F32), False)
    nonneg = c0 >= ksel
    k0 = jnp.where(nonneg, jnp.int32(0), jnp.int32(INT_MIN))
    n0 = jnp.where(nonneg, c0, MANY)

    def bit_body(it, carry):
        k, n_ge = carry
        cand = k | lax.shift_left(jnp.int32(1), jnp.int32(30) - it)
        n_cand = count(_key_to_float(cand), False)
        take = n_cand >= ksel
        return jnp.where(take, cand, k), jnp.where(take, n_cand, n_ge)

    k, n_ge = lax.fori_loop(0, 31, bit_body, (k0, n0))
    short = n_finite < ksel
    t = jnp.where(short, FLT_LOWEST, _key_to_float(k))
    need = jnp.where(short, MANY, ksel - count(t, True))
    tied = jnp.where(short, 0.0, jnp.where(n_ge > ksel, 1.0, 0.0))
    return t, need, tied


def _proj_even_kernel(x_ref, wa_ref, wb_ref, wq_ref, ww_ref, qs_ref,
                      na_ref, nb_ref, abf_ref, bbf_ref, q_ref, wi_ref):
    x = x_ref[...].astype(BF)
    a = _nn(x, wa_ref[...])
    na_ref[...] = a
    abf_ref[...] = a.astype(BF)
    b = _nn(x, wb_ref[...])
    nb_ref[...] = b
    bbf_ref[...] = b.astype(BF)
    q_ref[...] = (_nn(x, wq_ref[...]) * qs_ref[...]).astype(BF)
    wi_ref[...] = _nn(x, ww_ref[...]) * (HI ** -0.5)


def _proj_even(x, wa, wb, wq, ww, qscale, tm=256):
    n = x.shape[0]
    tm = min(tm, n)
    row = lambda w: pl.BlockSpec((tm, w), lambda i: (i, 0))
    return pl.pallas_call(
        _proj_even_kernel,
        grid=(n // tm,),
        in_specs=[row(D_MODEL), _const_spec(wa.shape), _const_spec(wb.shape),
                  _const_spec(wq.shape), _const_spec(ww.shape), _const_spec(qscale.shape)],
        out_specs=[row(1024), row(320), row(1024), row(320), row(1536), row(128)],
        out_shape=[jax.ShapeDtypeStruct((n, 1024), F32), jax.ShapeDtypeStruct((n, 320), F32),
                   jax.ShapeDtypeStruct((n, 1024), BF), jax.ShapeDtypeStruct((n, 320), BF),
                   jax.ShapeDtypeStruct((n, 1536), BF), jax.ShapeDtypeStruct((n, 128), F32)],
        compiler_params=_params(1),
        name="proj_even",
    )(x, wa, wb, wq, ww, qscale)


def _proj_even_t_kernel(x_ref, wa_ref, wb_ref, wqt_ref, wvat_ref, wvbt_ref, wwt_ref,
                        na_ref, nb_ref, abf_ref, bbf_ref, qt_ref, vat_ref, vbt_ref, wit_ref):
    x = x_ref[...]
    xb = x.astype(BF)
    xt = x.T.astype(BF)
    a = _nn(xb, wa_ref[...])
    na_ref[...] = a
    abf_ref[...] = a.astype(BF)
    b = _nn(xb, wb_ref[...])
    nb_ref[...] = b
    bbf_ref[...] = b.astype(BF)
    qt = _nn(wqt_ref[...], xt)
    qt_ref[0:512, :] = (qt[0:512] * (DA ** -0.5)).astype(BF)
    qt_ref[512:1024, :] = (qt[512:1024] * (DB ** -0.5)).astype(BF)
    qt_ref[1024:1536, :] = (qt[1024:1536] * (DI ** -0.5)).astype(BF)
    vat_ref[...] = _nn(wvat_ref[...], xt).astype(BF)
    vbt_ref[...] = _nn(wvbt_ref[...], xt).astype(BF)
    wit_ref[...] = _nn(wwt_ref[...], xt) * (HI ** -0.5)


def _proj_even_t(x, wa, wb, wqt, wvat, wvbt, wwt, tm=256):
    n = x.shape[0]
    row = lambda w: pl.BlockSpec((tm, w), lambda i: (i, 0))
    col = lambda h: pl.BlockSpec((h, tm), lambda i: (0, i))
    return pl.pallas_call(
        _proj_even_t_kernel,
        grid=(n // tm,),
        in_specs=[row(D_MODEL)] + [_const_spec(w.shape) for w in (wa, wb, wqt, wvat, wvbt, wwt)],
        out_specs=[row(1024), row(320), row(1024), row(320), col(1536), col(512), col(128), col(16)],
        out_shape=[jax.ShapeDtypeStruct((n, 1024), F32), jax.ShapeDtypeStruct((n, 320), F32),
                   jax.ShapeDtypeStruct((n, 1024), BF), jax.ShapeDtypeStruct((n, 320), BF),
                   jax.ShapeDtypeStruct((1536, n), BF), jax.ShapeDtypeStruct((512, n), BF),
                   jax.ShapeDtypeStruct((128, n), BF), jax.ShapeDtypeStruct((16, n), F32)],
        compiler_params=_params(1),
        name="proj_even_t",
    )(x, wa, wb, wqt, wvat, wvbt, wwt)


def _lambda(lam_ref, lam_init):
    lp = lam_ref[...]
    return (jnp.exp(jnp.sum(lp[0:1] * lp[1:2], axis=-1, keepdims=True))
            - jnp.exp(jnp.sum(lp[2:3] * lp[3:4], axis=-1, keepdims=True)) + lam_init)


def _split_q12(q):
    qf = q.astype(F32)
    lane = lax.broadcasted_iota(jnp.int32, qf.shape, 1)
    return jnp.concatenate([jnp.where(lane < DA, qf, 0.0), jnp.where(lane >= DA, qf, 0.0)],
                           axis=0).astype(BF)


def _subln(o, lam, g, lam_init):
    r = o.shape[0] // 2
    d = o[:r] - lam * o[r:]
    ms = jnp.mean(d * d, -1, keepdims=True)
    return d * lax.rsqrt(ms + EPS) * g * (1.0 - lam_init)


def _kth_largest_rows(sc_ref, nch, tk, ksel):
    rows = sc_ref.shape[0]

    def count(cand, strict):
        def body(c, acc):
            x = sc_ref[:, pl.ds(pl.multiple_of(c * tk, tk), tk)]
            hit = (x > cand) if strict else (x >= cand)
            return acc + jnp.where(hit, 1.0, 0.0)
        acc = lax.fori_loop(0, nch, body, jnp.zeros((rows, tk), F32))
        return jnp.sum(acc, -1, keepdims=True)

    return _kth_largest(count, (rows, 1), ksel)


def _kth_largest_cols(sc_ref, nch, tk, ksel):
    cols = sc_ref.shape[1]

    def count(cand, strict):
        def body(c, acc):
            x = sc_ref[pl.ds(pl.multiple_of(c * tk, tk), tk), :]
            hit = (x > cand) if strict else (x >= cand)
            return acc + jnp.sum(jnp.where(hit, 1.0, 0.0).reshape(tk // 32, 32, cols), axis=0)
        acc = lax.fori_loop(0, nch, body, jnp.zeros((32, cols), F32))
        return jnp.sum(acc, 0, keepdims=True)

    return _kth_largest(count, (1, cols), ksel)


def _selected(x, t, need, eq_before, tri):
    eq = jnp.where(x == t, 1.0, 0.0)
    rank = eq_before + _nn(eq.astype(BF), tri)
    tie_taken = jnp.where(rank <= need, eq, 0.0)
    return jnp.where(x > t, 1.0, tie_taken), jnp.sum(eq, -1, keepdims=True)


def _tri(n, lower):
    r = lax.broadcasted_iota(jnp.int32, (n, n), 0)
    c = lax.broadcasted_iota(jnp.int32, (n, n), 1)
    return jnp.where((r >= c) if lower else (r <= c), 1.0, 0.0).astype(BF)


def _diff_kernel(lam_ref, g_ref, qt_ref, a_ref, vt_ref, o_ref, qh_ref, acc_ref, *, tq, tk, lam_init):
    i = pl.program_id(1)
    lam = _lambda(lam_ref, lam_init)
    n_full = (i * tq) // tk
    sub = lax.broadcasted_iota(jnp.int32, (128, tq), 0)
    key = lax.broadcasted_iota(jnp.int32, (tk, tq), 0)
    qry = i * tq + lax.broadcasted_iota(jnp.int32, (tk, tq), 1)
    nchain = 2 * H_A
    for h in range(H_A):
        qt = qt_ref[h * 128:(h + 1) * 128, :].astype(F32)
        qh_ref[2 * h] = jnp.where(sub < DA, qt, 0.0).astype(BF)
        qh_ref[2 * h + 1] = jnp.where(sub >= DA, qt, 0.0).astype(BF)
    acc_ref[...] = jnp.zeros(acc_ref.shape, F32)

    def step(c, carry, masked):
        ms, ls = carry
        r0 = pl.multiple_of(c * tk, tk)
        ok = (r0 + key <= qry) if masked else None
        ms_new, ls_new = [], []
        for j in range(nchain):
            h = j // 2
            s = _nn(a_ref[pl.ds(r0, tk), h * 256:h * 256 + 128], qh_ref[j])
            if masked:
                s = jnp.where(ok, s, NEG)
            m_new = jnp.maximum(ms[j], jnp.max(s, 0, keepdims=True))
            p = jnp.exp(s - m_new)
            alpha = jnp.exp(ms[j] - m_new)
            ls_new.append(alpha * ls[j] + jnp.sum(p, 0, keepdims=True))
            acc_ref[j] = alpha * acc_ref[j] + _nn(vt_ref[h * 128:(h + 1) * 128, pl.ds(r0, tk)], p.astype(BF))
            ms_new.append(m_new)
        return tuple(ms_new), tuple(ls_new)

    row = lambda v: tuple(jnp.full((1, tq), v, F32) for _ in range(nchain))
    carry = lax.fori_loop(0, n_full, lambda c, cr: step(c, cr, False), (row(NEG), row(0.0)))
    for d in range(tq // tk):
        carry = step(n_full + d, carry, True)
    _, ls = carry
    for h in range(H_A):
        d = acc_ref[2 * h] * (1.0 / ls[2 * h]) - lam * (acc_ref[2 * h + 1] * (1.0 / ls[2 * h + 1]))
        ms = jnp.mean(d * d, 0, keepdims=True)
        o = d * lax.rsqrt(ms + EPS) * g_ref[...] * (1.0 - lam_init)
        o_ref[:, h * 128:(h + 1) * 128] = o.T.astype(o_ref.dtype)


def _diff_attn_prompt(lam_e, g_col, qt, a_bf, vat, batch, seq, lam_init, tq=256, tk=128):
    nq = seq // tq
    return pl.pallas_call(
        functools.partial(_diff_kernel, tq=tq, tk=tk, lam_init=lam_init),
        grid=(batch, nq),
        in_specs=[_const_spec(lam_e.shape), _const_spec(g_col.shape),
                  pl.BlockSpec((512, tq), lambda b, i: (0, b * nq + i)),
                  pl.BlockSpec((seq, 1024), lambda b, i: (b, 0)),
                  pl.BlockSpec((512, seq), lambda b, i: (0, b))],
        out_specs=pl.BlockSpec((tq, 512), lambda b, i: (b * nq + i, 0)),
        out_shape=jax.ShapeDtypeStruct((batch * seq, 512), BF),
        scratch_shapes=[pltpu.VMEM((2 * H_A, 128, tq), BF), pltpu.VMEM((2 * H_A, DV_A, tq), F32)],
        compiler_params=_params(2),
        name="diff_attn_prompt",
    )(lam_e, g_col, qt, a_bf, vat)


def _sparse_kernel(qbt_ref, qit_ref, wit_ref, b_ref, vbt_ref, o_ref, sc_ref, acc_ref, *, tq, tk, ksel):
    i = pl.program_id(1)
    nch = (i + 1) * (tq // tk)
    half = tk // 2
    key = lax.broadcasted_iota(jnp.int32, (half, tq), 0)
    qry = i * tq + lax.broadcasted_iota(jnp.int32, (half, tq), 1)
    w = wit_ref[...]

    def score_chunk(c, _):
        for u in range(2):
            r0 = pl.multiple_of(c * tk + u * half, half)
            kidx = b_ref[pl.ds(r0, half), 2 * DB:2 * DB + DI]
            acc = jnp.zeros((half, tq), F32)
            for h in range(HI):
                d = _nn(kidx, qit_ref[h * DI:(h + 1) * DI, :])
                acc = acc + w[h:h + 1, :] * jnp.maximum(d, 0.0)
            sc_ref[pl.ds(r0, half), :] = jnp.where(r0 + key <= qry, acc, -jnp.inf)
        return 0

    lax.fori_loop(0, nch, score_chunk, 0)
    t, need, tied = _kth_largest_cols(sc_ref, nch, tk, ksel)
    acc_ref[...] = jnp.zeros(acc_ref.shape, F32)

    ta = half

    def attend(c, carry, exact_ties):
        ms, ls, eq_before = carry
        r0 = pl.multiple_of(c * ta, ta)
        if exact_ties:
            x = sc_ref[pl.ds(r0, ta), :]
            eq = jnp.where(x == t, 1.0, 0.0)
            rank = eq_before + _nn(_tri(ta, True), eq.astype(BF))
            bias = jnp.where(x > t, 0.0, jnp.where(x == t, jnp.where(rank <= need, 0.0, NEG), NEG))
            eq_before = eq_before + jnp.sum(eq, 0, keepdims=True)
        kb = b_ref[pl.ds(r0, ta), 0:DB]
        vt = vbt_ref[:, pl.ds(r0, ta)]
        ms_new, ls_new = [], []
        for h in range(H_B):
            s = _nn(kb, qbt_ref[h * DB:(h + 1) * DB, :])
            s = (s + bias) if exact_ties else jnp.where(sc_ref[pl.ds(r0, ta), :] >= t, s, NEG)
            m_new = jnp.maximum(ms[h], jnp.max(s, 0, keepdims=True))
            p = jnp.exp(s - m_new)
            alpha = jnp.exp(ms[h] - m_new)
            ls_new.append(alpha * ls[h] + jnp.sum(p, 0, keepdims=True))
            acc_ref[h] = alpha * acc_ref[h] + _nn(vt, p.astype(BF))
            ms_new.append(m_new)
        return tuple(ms_new), tuple(ls_new), eq_before

    row = lambda v: tuple(jnp.full((1, tq), v, F32) for _ in range(H_B))
    init = (row(NEG), row(0.0), jnp.zeros((1, tq), F32))
    _, ls, _ = lax.cond(
        jnp.max(tied) > 0.0,
        lambda: lax.fori_loop(0, 2 * nch, lambda c, cr: attend(c, cr, True), init),
        lambda: lax.fori_loop(0, 2 * nch, lambda c, cr: attend(c, cr, False), init))
    for h in range(H_B):
        o = acc_ref[h] * (1.0 / ls[h])
        o_ref[:, h * DB:(h + 1) * DB] = o.T.astype(o_ref.dtype)


def _sparse_attn_prompt(qt, wit, b_bf, vbt, batch, seq, ksel, tq=256, tk=256):
    nq = seq // tq
    return pl.pallas_call(
        functools.partial(_sparse_kernel, tq=tq, tk=tk, ksel=ksel),
        grid=(batch, nq),
        in_specs=[pl.BlockSpec((512, tq), lambda b, i: (1, b * nq + i)),
                  pl.BlockSpec((512, tq), lambda b, i: (2, b * nq + i)),
                  pl.BlockSpec((16, tq), lambda b, i: (0, b * nq + i)),
                  pl.BlockSpec((seq, 320), lambda b, i: (b, 0)),
                  pl.BlockSpec((128, seq), lambda b, i: (0, b))],
        out_specs=pl.BlockSpec((tq, 512), lambda b, i: (b * nq + i, 0)),
        out_shape=jax.ShapeDtypeStruct((batch * seq, 512), BF),
        scratch_shapes=[pltpu.VMEM((seq, tq), F32), pltpu.VMEM((H_B, DB, tq), F32)],
        compiler_params=_params(2),
        name="sparse_attn_prompt",
    )(qt, qt, wit, b_bf, vbt)


def _sample_even_kernel(pt_ref, lam_ref, g_ref, q_ref, wi_ref, *rest, npg, page, ksel, lam_init):
    del pt_ref
    nblk = npg + 1
    a_pages = rest[:nblk]
    b_pages = rest[nblk:2 * nblk]
    o_ref = rest[2 * nblk]
    sc_ref = rest[2 * nblk + 1]
    r = q_ref.shape[0]
    lam = _lambda(lam_ref, lam_init)
    q = q_ref[...]

    def new_ok(rows):
        tok = lax.broadcasted_iota(jnp.int32, (rows, page), 0) % r
        return lax.broadcasted_iota(jnp.int32, (rows, page), 1) <= tok

    ok2 = new_ok(2 * r)
    for h in range(H_A):
        qq = _split_q12(q[:, h * 128:(h + 1) * 128])
        ss = [_nt(qq, a_pages[p][pl.ds(h, page, stride=2 * H_A), :].astype(BF)) for p in range(nblk)]
        ss[npg] = jnp.where(ok2, ss[npg], NEG)
        m = functools.reduce(jnp.maximum, ss)
        m = jnp.max(m, -1, keepdims=True)
        ps = [jnp.exp(s - m) for s in ss]
        l = jnp.sum(functools.reduce(lambda x, y: x + y, ps), -1, keepdims=True)
        acc = functools.reduce(lambda x, y: x + y, [
            _nn(ps[p].astype(BF), a_pages[p][pl.ds(H_A + h, page, stride=2 * H_A), :].astype(BF))
            for p in range(nblk)])
        o_ref[:, h * 128:(h + 1) * 128] = _subln(acc / l, lam, g_ref[...], lam_init)

    qi = jnp.concatenate([q[:, 1024 + h * DI:1024 + (h + 1) * DI] for h in range(HI)], axis=0).astype(BF)
    w = wi_ref[...]
    wcol = jnp.concatenate([w[:, h:h + 1] for h in range(HI)], axis=0)
    ok1 = new_ok(r)
    for p in range(nblk):
        d = jnp.maximum(_nn(qi, b_pages[p][2 * DB:2 * DB + DI, :].astype(BF)), 0.0) * wcol
        sc = d[0:r]
        for h in range(1, HI):
            sc = sc + d[h * r:(h + 1) * r]
        if p == npg:
            sc = jnp.where(ok1, sc, -jnp.inf)
        sc_ref[:, p * page:(p + 1) * page] = sc
    t, need, _ = _kth_largest_rows(sc_ref, nblk, page, ksel)

    tri = _tri(page, False)
    q4 = jnp.concatenate([q[:, 512 + h * DB:512 + (h + 1) * DB] for h in range(H_B)], axis=0).astype(BF)
    eq_before = jnp.zeros((r, 1), F32)
    ss, sels = [], []
    for p in range(nblk):
        sel, n_eq = _selected(sc_ref[:, p * page:(p + 1) * page], t, need, eq_before, tri)
        eq_before = eq_before + n_eq
        if p == npg:
            sel = jnp.where(ok1, sel, 0.0)
        sels.append(jnp.concatenate([sel] * H_B, axis=0) > 0.5)
        ss.append(_nn(q4, b_pages[p][0:DB, :].astype(BF)))
    m = functools.reduce(jnp.maximum, [jnp.where(sl, s, NEG) for sl, s in zip(sels, ss)])
    m = jnp.max(m, -1, keepdims=True)
    ps = [jnp.where(sl, jnp.exp(s - m), 0.0) for sl, s in zip(sels, ss)]
    l = jnp.sum(functools.reduce(lambda x, y: x + y, ps), -1, keepdims=True)
    acc = functools.reduce(lambda x, y: x + y, [
        _nt(ps[p].astype(BF), b_pages[p][DB:2 * DB, :].astype(BF)) for p in range(nblk)])
    o = acc / l
    for h in range(H_B):
        o_ref[:, 512 + h * DB:512 + (h + 1) * DB] = o[h * r:(h + 1) * r]


def _sample_even(page_table, lam_e, g, qs, wis, anew_pg, bnew_pg, cache_a_pg, cache_b_pg, ksel, lam_init):
    nreq, npg = page_table.shape
    page = cache_b_pg.shape[2]
    r = qs.shape[1]
    req = lambda a: pl.BlockSpec((None,) + a.shape[1:], lambda i, pt: (i, 0, 0))

    def page_spec(a, p):
        return pl.BlockSpec((None,) + a.shape[1:], lambda i, pt, p=p: (pt[i, p], 0, 0))

    in_specs = [_const_spec(lam_e.shape), _const_spec(g.shape), req(qs), req(wis)]
    in_specs += [page_spec(cache_a_pg, p) for p in range(npg)] + [req(anew_pg)]
    in_specs += [page_spec(cache_b_pg, p) for p in range(npg)] + [req(bnew_pg)]
    grid_spec = pltpu.PrefetchScalarGridSpec(
        num_scalar_prefetch=1, grid=(nreq,), in_specs=in_specs,
        out_specs=pl.BlockSpec((None, r, 1024), lambda i, pt: (i, 0, 0)),
        scratch_shapes=[pltpu.VMEM((r, (npg + 1) * page), F32)])
    return pl.pallas_call(
        functools.partial(_sample_even_kernel, npg=npg, page=page, ksel=ksel, lam_init=lam_init),
        grid_spec=grid_spec,
        out_shape=jax.ShapeDtypeStruct((nreq, r, 1024), F32),
        compiler_params=_params(1),
        name="sample_even",
    )(page_table, lam_e, g, qs, wis, *([cache_a_pg] * npg), anew_pg, *([cache_b_pg] * npg), bnew_pg)


def _outproj_kernel(*refs, n_lhs):
    lhs = refs[:n_lhs]
    w_ref, x_ref, g_ref, b_ref, o_ref = refs[n_lhs:]
    y = None
    k0 = 0
    for a_ref in lhs:
        kw = a_ref.shape[1]
        part = _nn(a_ref[...].astype(BF), w_ref[k0:k0 + kw, :])
        y = part if y is None else y + part
        k0 += kw
    o_ref[...] = _ln(ALPHA * x_ref[...] + y, g_ref[...], b_ref[...])


def _outproj_res_ln(lhs, w, x, g, b, tm=256):
    n = x.shape[0]
    tm = min(tm, n)
    row = lambda wd: pl.BlockSpec((tm, wd), lambda i: (i, 0))
    return pl.pallas_call(
        functools.partial(_outproj_kernel, n_lhs=len(lhs)),
        grid=(n // tm,),
        in_specs=[row(a.shape[1]) for a in lhs] + [_const_spec(w.shape), row(D_MODEL),
                                                   _const_spec(g.shape), _const_spec(b.shape)],
        out_specs=row(D_MODEL),
        out_shape=jax.ShapeDtypeStruct((n, D_MODEL), F32),
        compiler_params=_params(1),
        name="outproj_res_ln",
    )(*lhs, w, x, g, b)


def _mlp_kernel(x_ref, w1_ref, w2_ref, g_ref, b_ref, o_ref, acc_ref, *, ck):
    x = x_ref[...]
    xb = x.astype(BF)
    for c in range(D_FF // ck):
        h = jnp.maximum(_nn(xb, w1_ref[:, c * ck:(c + 1) * ck]), 0.0)
        part = _nn((h * h).astype(BF), w2_ref[c * ck:(c + 1) * ck, :])
        if c == 0:
            acc_ref[...] = part
        else:
            acc_ref[...] += part
    o_ref[...] = _ln(ALPHA * x + acc_ref[...], g_ref[...], b_ref[...])


def _mlp_res_ln(x, w1, w2, g, b, tm=512, ck=512):
    n = x.shape[0]
    tm = min(tm, n)
    row = pl.BlockSpec((tm, D_MODEL), lambda i: (i, 0))
    resident = lambda s: pl.BlockSpec(s, lambda i: (0, 0), pipeline_mode=pl.Buffered(1))
    return pl.pallas_call(
        functools.partial(_mlp_kernel, ck=ck),
        grid=(n // tm,),
        in_specs=[row, resident(w1.shape), resident(w2.shape), _const_spec(g.shape), _const_spec(b.shape)],
        out_specs=row,
        out_shape=jax.ShapeDtypeStruct((n, D_MODEL), F32),
        scratch_shapes=[pltpu.VMEM((tm, D_MODEL), F32)],
        compiler_params=_params(1),
        name="mlp_res_ln",
    )(x, w1, w2, g, b)


def _gelu(x):
    return 0.5 * x * (1.0 + jnp.tanh(math.sqrt(2.0 / math.pi) * (x + 0.044715 * (x * x * x))))


def _proj_odd_kernel(x_ref, w_ref, g_ref, b_ref, xc_ref, u_ref, vn_ref):
    h = _nn(x_ref[...].astype(BF), w_ref[...])
    xc_ref[...] = h[:, :MIX_C]
    u_ref[...] = _gelu(h[:, MIX_C:MIX_C + MIX_D])
    vn_ref[...] = _ln(_gelu(h[:, MIX_C + MIX_D:]), g_ref[...], b_ref[...])


def _proj_odd(x, w, g, b, tm=256):
    n = x.shape[0]
    tm = min(tm, n)
    row = lambda wd: pl.BlockSpec((tm, wd), lambda i: (i, 0))
    return pl.pallas_call(
        _proj_odd_kernel,
        grid=(n // tm,),
        in_specs=[row(D_MODEL), _const_spec(w.shape), _const_spec(g.shape), _const_spec(b.shape)],
        out_specs=[row(512), row(512), row(512)],
        out_shape=[jax.ShapeDtypeStruct((n, 512), F32)] * 3,
        compiler_params=_params(1),
        name="proj_odd",
    )(x, w, g, b)


def _pool_sgu_kernel(prev_ref, halo_ref, xc_ref, u_ref, vn_ref, wp_ref, sc_ref, ws_ref, bs_ref,
                     o_ref, ext_ref, *, start):
    t = pl.program_id(1)
    hal = prev_ref.shape[0]
    ext_ref[0:hal, :] = jnp.where(t == 0, prev_ref[...], halo_ref[...])
    ext_ref[hal:hal + CHUNK, :] = xc_ref[...]
    pos = start + t * CHUNK + lax.broadcasted_iota(jnp.int32, (CHUNK, 1), 0)
    for g, w in enumerate(POOL_WINDOWS):
        gs = slice(g * C_GROUP, (g + 1) * C_GROUP)
        acc = ext_ref[hal:hal + CHUNK, gs]
        for s in range(1, w):
            acc = acc + ext_ref[hal - s:hal - s + CHUNK, gs]
        cnt = jnp.minimum(w, pos + 1).astype(F32)
        pooled = acc / cnt - xc_ref[:, gs]
        c = _nn(pooled.astype(BF), wp_ref[g]) * sc_ref[:, gs]
        o_ref[:, gs] = c.astype(o_ref.dtype)
    r = lax.broadcasted_iota(jnp.int32, (CHUNK, CHUNK), 0)
    cc = lax.broadcasted_iota(jnp.int32, (CHUNK, CHUNK), 1)
    for g in range(D_GROUPS):
        gs = slice(g * 128, (g + 1) * 128)
        ws = jnp.where(r >= cc, ws_ref[g], 0.0).astype(BF)
        s = _nn(ws, vn_ref[:, gs].astype(BF)) + bs_ref[:, g:g + 1]
        o_ref[:, MIX_C + g * 128:MIX_C + (g + 1) * 128] = (u_ref[:, gs] * s).astype(o_ref.dtype)


def _pool_sgu_prompt(prev16, xc, u, vn, wp, scale, ws, bs_t, batch, seq, start):
    nt = seq // CHUNK
    hal = prev16.shape[1]
    per = CHUNK // hal
    row = pl.BlockSpec((CHUNK, 512), lambda b, t: (b * nt + t, 0))
    return pl.pallas_call(
        functools.partial(_pool_sgu_kernel, start=start),
        grid=(batch, nt),
        in_specs=[pl.BlockSpec((None, hal, 512), lambda b, t: (b, 0, 0)),
                  pl.BlockSpec((hal, 512), lambda b, t: (jnp.maximum((b * nt + t) * per - 1, 0), 0)),
                  row, row, row, _const_spec(wp.shape), _const_spec(scale.shape),
                  _const_spec(ws.shape), _const_spec(bs_t.shape)],
        out_specs=pl.BlockSpec((CHUNK, 1024), lambda b, t: (b * nt + t, 0)),
        out_shape=jax.ShapeDtypeStruct((batch * seq, 1024), BF),
        scratch_shapes=[pltpu.VMEM((hal + CHUNK, 512), F32)],
        compiler_params=_params(2),
        name="pool_sgu_prompt",
    )(prev16, xc, xc, u, vn, wp, scale, ws, bs_t)


def _pool_sgu_sample_kernel(ws_ref, bs_ref, prev_ref, xc_ref, u_ref, vn_ref, wp_ref, sc_ref, o_ref,
                            *, start):
    nprev = prev_ref.shape[0]
    ntok = xc_ref.shape[0]
    for t in range(ntok):
        for g, w in enumerate(POOL_WINDOWS):
            gs = slice(g * C_GROUP, (g + 1) * C_GROUP)
            acc = None
            for s in range(w):
                j = nprev + t - s
                slab = prev_ref[j, :, gs] if j < nprev else xc_ref[j - nprev, :, gs]
                acc = slab if acc is None else acc + slab
            cnt = float(min(w, start + t + 1))
            pooled = acc / cnt - xc_ref[t, :, gs]
            o_ref[t, :, gs] = _nn(pooled.astype(BF), wp_ref[g]) * sc_ref[:, gs]
        for g in range(D_GROUPS):
            gs = slice(g * 128, (g + 1) * 128)
            s = None
            for j in range(t + 1):
                term = ws_ref[(g * ntok + t) * ntok + j] * vn_ref[j, :, gs]
                s = term if s is None else s + term
            s = s + bs_ref[g * ntok + t]
            o_ref[t, :, MIX_C + g * 128:MIX_C + (g + 1) * 128] = u_ref[t, :, gs] * s


def _pool_sgu_sample(ws_small, bs_small, prev_t, xc_t, u_t, vn_t, wp, scale, start):
    ntok, nreq, _ = xc_t.shape
    smem = pl.BlockSpec(memory_space=pltpu.SMEM)
    return pl.pallas_call(
        functools.partial(_pool_sgu_sample_kernel, start=start),
        grid=(1,),
        in_specs=[smem, smem, _const_spec(prev_t.shape), _const_spec(xc_t.shape), _const_spec(u_t.shape),
                  _const_spec(vn_t.shape), _const_spec(wp.shape), _const_spec(scale.shape)],
        out_specs=_const_spec((ntok, nreq, 1024)),
        out_shape=jax.ShapeDtypeStruct((ntok, nreq, 1024), F32),
        compiler_params=_params(1),
        name="pool_sgu_sample",
    )(ws_small, bs_small, prev_t, xc_t, u_t, vn_t, wp, scale)


def _even_weights(w_in):
    q_a, k_a, v_a, q_b, k_b, v_b, q_i, k_i, w_i = jnp.split(
        w_in, [512, 1024, 1536, 2048, 2176, 2304, 2816, 2880], axis=1)
    wa = jnp.concatenate([k_a.reshape(D_MODEL, H_A, 2 * DA), v_a.reshape(D_MODEL, H_A, DV_A)],
                         -1).reshape(D_MODEL, H_A * (2 * DA + DV_A))
    wb = jnp.concatenate([k_b, v_b, k_i], 1)
    wq = jnp.concatenate([q_a, q_b, q_i], 1)
    ww = jnp.concatenate([w_i, jnp.zeros((D_MODEL, 128 - HI), w_in.dtype)], 1)
    qscale = jnp.concatenate([jnp.full((512,), DA ** -0.5, F32), jnp.full((512,), DB ** -0.5, F32),
                              jnp.full((512,), DI ** -0.5, F32)]).reshape(1, 1536)
    wwt = jnp.concatenate([w_i.T, jnp.zeros((16 - HI, D_MODEL), w_in.dtype)], 0)
    natural = (wa.astype(BF), wb.astype(BF), wq.astype(BF), ww.astype(BF), qscale)
    feature_major = (wq.T.astype(BF), v_a.T.astype(BF), v_b.T.astype(BF), wwt.astype(BF))
    return natural, feature_major


def _a_pages(x):
    n, page, _ = x.shape
    return x.reshape(n, page, H_A, 2, 128).transpose(0, 1, 3, 2, 4).reshape(n, page * 2 * H_A, 128)


def _pad_rows(x, rows):
    return jnp.pad(x, ((0, 0), (0, rows - x.shape[1]), (0, 0)))


def kernel(x_prompt, x_sample, cache_a, cache_b, state_pool, page_table, w_in_e, lam_e, subln_g, w_out_e,
           w_in_o, w_pool, pool_scale, sgu_g, sgu_b, w_s, b_s, w_out_o, w_mlp1, w_mlp2, ln_g, ln_b):
    batch, seq, _ = x_prompt.shape
    nreq, ntok, _ = x_sample.shape
    npg = page_table.shape[1]
    page = cache_a.shape[2]
    past = npg * page
    xp = x_prompt.reshape(batch * seq, D_MODEL)
    xs = x_sample.reshape(nreq * ntok, D_MODEL)
    outs = {k: [] for k in ("a_p", "b_p", "pool_p", "a_s", "b_s", "pool_s", "v_s")}
    rpad = 8

    for l in range(DEPTH):
        i = l // 2
        row2 = lambda v: v.reshape(1, -1)
        if l % 2 == 0:
            lam_init = 0.8 - 0.6 * math.exp(-0.3 * l)
            (wa, wb, wq, ww, qscale), (wqt, wvat, wvbt, wwt) = _even_weights(w_in_e[i])
            g = row2(subln_g[i])
            w_out = w_out_e[i].astype(BF)
            na, nb, abf, bbf, qt, vat, vbt, wit = _proj_even_t(xp, wa, wb, wqt, wvat, wvbt, wwt)
            o_a = _diff_attn_prompt(lam_e[i], subln_g[i].reshape(-1, 1), qt, abf, vat, batch, seq, lam_init)
            o_b = _sparse_attn_prompt(qt, wit, bbf, vbt, batch, seq, min(TOPK_MAX, seq // 4))
            mix_p = [o_a, o_b]
            outs["a_p"].append(na.reshape(batch, seq, H_A, 2 * DA + DV_A))
            outs["b_p"].append(nb.reshape(batch, seq, 2 * DB + DI))
            nas, nbs, _, _, qs, wis = _proj_even(xs, wa, wb, wq, ww, qscale)
            o_s = _sample_even(
                page_table, lam_e[i], g,
                _pad_rows(qs.astype(F32).reshape(nreq, ntok, 1536), rpad),
                _pad_rows(wis.reshape(nreq, ntok, 128), rpad),
                _a_pages(_pad_rows(nas.reshape(nreq, ntok, 1024), page)),
                _pad_rows(nbs.reshape(nreq, ntok, 320), page).transpose(0, 2, 1),
                _a_pages(cache_a[i].reshape(-1, page, 1024)), cache_b[i].transpose(0, 2, 1),
                min(TOPK_MAX, (past + ntok) // 4), lam_init)
            mix_s = [o_s[:, :ntok].reshape(nreq * ntok, 1024)]
            outs["a_s"].append(nas.reshape(nreq, ntok, H_A, 2 * DA + DV_A))
            outs["b_s"].append(nbs.reshape(nreq, ntok, 2 * DB + DI))
        else:
            w_in = w_in_o[i].astype(BF)
            w_out = w_out_o[i].astype(BF)
            wp = w_pool[i].astype(BF)
            scale = row2(pool_scale[i])
            sg, sb = row2(sgu_g[i]), row2(sgu_b[i])
            xc, u, vn = _proj_odd(xp, w_in, sg, sb)
            prev16 = jnp.zeros((batch, POOL_BUF + 1, MIX_C), F32)
            mix_p = [_pool_sgu_prompt(prev16, xc, u, vn, wp, scale, w_s[i], b_s[i].T, batch, seq, 0)]
            outs["pool_p"].append(xc.reshape(batch, seq, MIX_C)[:, seq - POOL_BUF:])
            xcs, us, vns = _proj_odd(xs, w_in, sg, sb)
            tmaj = lambda v: v.reshape(nreq, ntok, -1).transpose(1, 0, 2)
            m_t = _pool_sgu_sample(
                w_s[i][:, :ntok, :ntok].reshape(-1), b_s[i][:, :ntok].reshape(-1),
                state_pool[i].transpose(1, 0, 2), tmaj(xcs), tmaj(us), tmaj(vns), wp, scale, past)
            mix_s = [m_t.transpose(1, 0, 2).reshape(nreq * ntok, 1024)]
            ext = jnp.concatenate([state_pool[i], xcs.reshape(nreq, ntok, MIX_C)], 1)
            outs["pool_s"].append(ext[:, ext.shape[1] - POOL_BUF:])
            outs["v_s"].append(vns.reshape(nreq, ntok, MIX_D))
        g0, b0, g1, b1 = row2(ln_g[l, 0]), row2(ln_b[l, 0]), row2(ln_g[l, 1]), row2(ln_b[l, 1])
        w1, w2 = w_mlp1[l].astype(BF), w_mlp2[l].astype(BF)
        xp = _outproj_res_ln(mix_p, w_out, xp, g0, b0)
        xp = _mlp_res_ln(xp, w1, w2, g1, b1)
        xs = _outproj_res_ln(mix_s, w_out, xs, g0, b0)
        xs = _mlp_res_ln(xs, w1, w2, g1, b1)

    st = lambda k: jnp.stack(outs[k])
    return (xp.reshape(batch, seq, D_MODEL), xs.reshape(nreq, ntok, D_MODEL), st("a_p"), st("b_p"),
            st("pool_p"), st("a_s"), st("b_s"), st("pool_s"), st("v_s"))
```

```python
import functools
import math

import jax
import jax.numpy as jnp
from jax import lax
from jax.experimental import pallas as pl
from jax.experimental.pallas import tpu as pltpu

D_MODEL = 1024
H_A = 4
DA = 64
DV_A = 128
H_B = 4
DB = 128
HI = 8
DI = 64
TOPK_MAX = 256
MIX_C = 512
MIX_D = 512
POOL_WINDOWS = (2, 4, 8, 16)
C_GROUP = 128
POOL_BUF = 15
CHUNK = 128
D_GROUPS = 4
D_FF = 4096
DEPTH = 2
ALPHA = (2 * DEPTH) ** 0.25
EPS = 1e-5

BF = jnp.bfloat16
F32 = jnp.float32
NEG = -1e30
INT_MIN = -(2 ** 31)
FLT_LOWEST = -3.4028234663852886e38
MANY = 1e9
VMEM_LIMIT_BYTES = 56 * 1024 * 1024


def _params(n_axes):
    return pltpu.CompilerParams(dimension_semantics=("arbitrary",) * n_axes,
                                vmem_limit_bytes=VMEM_LIMIT_BYTES)


def _nn(a, b):
    return jnp.dot(a, b, preferred_element_type=F32)


def _nt(a, b):
    return lax.dot_general(a, b, (((1,), (1,)), ((), ())), preferred_element_type=F32)


def _ln(z, g, b):
    mu = jnp.mean(z, -1, keepdims=True)
    d = z - mu
    var = jnp.mean(d * d, -1, keepdims=True)
    return d * lax.rsqrt(var + EPS) * g + b


def _const_spec(shape):
    nd = len(shape)
    return pl.BlockSpec(shape, lambda *_: (0,) * nd)


def _key_to_float(k):
    return lax.bitcast_convert_type(jnp.where(k < 0, k ^ jnp.int32(0x7FFFFFFF), k), F32)


def _kth_largest(count, shape, ksel, bits=1):
    n_finite, = count([jnp.full(shape, -jnp.inf, F32)], True)
    c0, = count([jnp.zeros(shape, F32)], False)
    nonneg = c0 >= ksel
    k0 = jnp.where(nonneg, jnp.int32(0), jnp.int32(INT_MIN))
    n0 = jnp.where(nonneg, c0, MANY)

    def refine(carry, shift, nbits):
        k, n_ge = carry
        cands = [k | lax.shift_left(jnp.int32(d), shift) for d in range(1, 2 ** nbits)]
        for cand, n_cand in zip(cands, count([_key_to_float(c) for c in cands], False)):
            take = n_cand >= ksel
            k, n_ge = jnp.where(take, cand, k), jnp.where(take, n_cand, n_ge)
        return k, n_ge

    npass, rest = divmod(31, bits)
    carry = lax.fori_loop(0, npass, lambda it, cr: refine(cr, jnp.int32(31 - bits) - bits * it, bits),
                          (k0, n0))
    if rest:
        carry = refine(carry, jnp.int32(0), rest)
    k, n_ge = carry
    short = n_finite < ksel
    t = jnp.where(short, FLT_LOWEST, _key_to_float(k))
    n_above, = count([t], True)
    need = jnp.where(short, MANY, ksel - n_above)
    tied = jnp.where(short, 0.0, jnp.where(n_ge > ksel, 1.0, 0.0))
    return t, need, tied


def _proj_even_kernel(x_ref, wa_ref, wb_ref, wq_ref, ww_ref, qs_ref,
                      na_ref, nb_ref, abf_ref, bbf_ref, q_ref, wi_ref):
    x = x_ref[...].astype(BF)
    a = _nn(x, wa_ref[...])
    na_ref[...] = a
    abf_ref[...] = a.astype(BF)
    b = _nn(x, wb_ref[...])
    nb_ref[...] = b
    bbf_ref[...] = b.astype(BF)
    q_ref[...] = (_nn(x, wq_ref[...]) * qs_ref[...]).astype(BF)
    wi_ref[...] = _nn(x, ww_ref[...]) * (HI ** -0.5)


def _proj_even(x, wa, wb, wq, ww, qscale, tm=256):
    n = x.shape[0]
    tm = min(tm, n)
    row = lambda w: pl.BlockSpec((tm, w), lambda i: (i, 0))
    return pl.pallas_call(
        _proj_even_kernel,
        grid=(n // tm,),
        in_specs=[row(D_MODEL), _const_spec(wa.shape), _const_spec(wb.shape),
                  _const_spec(wq.shape), _const_spec(ww.shape), _const_spec(qscale.shape)],
        out_specs=[row(1024), row(320), row(1024), row(320), row(1536), row(128)],
        out_shape=[jax.ShapeDtypeStruct((n, 1024), F32), jax.ShapeDtypeStruct((n, 320), F32),
                   jax.ShapeDtypeStruct((n, 1024), BF), jax.ShapeDtypeStruct((n, 320), BF),
                   jax.ShapeDtypeStruct((n, 1536), BF), jax.ShapeDtypeStruct((n, 128), F32)],
        compiler_params=_params(1),
        name="proj_even",
    )(x, wa, wb, wq, ww, qscale)


def _proj_even_t_kernel(x_ref, wa_ref, wb_ref, wqt_ref, wvat_ref, wvbt_ref, wwt_ref,
                        na_ref, nb_ref, abf_ref, bbf_ref, qt_ref, vat_ref, vbt_ref, wit_ref):
    x = x_ref[...]
    xb = x.astype(BF)
    xt = x.T.astype(BF)
    a = _nn(xb, wa_ref[...])
    na_ref[...] = a
    abf_ref[...] = a.astype(BF)
    b = _nn(xb, wb_ref[...])
    nb_ref[...] = b
    bbf_ref[...] = b.astype(BF)
    qt = _nn(wqt_ref[...], xt)
    qt_ref[0:512, :] = (qt[0:512] * (DA ** -0.5)).astype(BF)
    qt_ref[512:1024, :] = (qt[512:1024] * (DB ** -0.5)).astype(BF)
    qt_ref[1024:1536, :] = (qt[1024:1536] * (DI ** -0.5)).astype(BF)
    vat_ref[...] = _nn(wvat_ref[...], xt).astype(BF)
    vbt_ref[...] = _nn(wvbt_ref[...], xt).astype(BF)
    wit_ref[...] = _nn(wwt_ref[...], xt) * (HI ** -0.5)


def _proj_even_t(x, wa, wb, wqt, wvat, wvbt, wwt, tm=256):
    n = x.shape[0]
    row = lambda w: pl.BlockSpec((tm, w), lambda i: (i, 0))
    col = lambda h: pl.BlockSpec((h, tm), lambda i: (0, i))
    return pl.pallas_call(
        _proj_even_t_kernel,
        grid=(n // tm,),
        in_specs=[row(D_MODEL)] + [_const_spec(w.shape) for w in (wa, wb, wqt, wvat, wvbt, wwt)],
        out_specs=[row(1024), row(320), row(1024), row(320), col(1536), col(512), col(128), col(16)],
        out_shape=[jax.ShapeDtypeStruct((n, 1024), F32), jax.ShapeDtypeStruct((n, 320), F32),
                   jax.ShapeDtypeStruct((n, 1024), BF), jax.ShapeDtypeStruct((n, 320), BF),
                   jax.ShapeDtypeStruct((1536, n), BF), jax.ShapeDtypeStruct((512, n), BF),
                   jax.ShapeDtypeStruct((128, n), BF), jax.ShapeDtypeStruct((16, n), F32)],
        compiler_params=_params(1),
        name="proj_even_t",
    )(x, wa, wb, wqt, wvat, wvbt, wwt)


def _lambda(lam_ref, lam_init):
    lp = lam_ref[...]
    return (jnp.exp(jnp.sum(lp[0:1] * lp[1:2], axis=-1, keepdims=True))
            - jnp.exp(jnp.sum(lp[2:3] * lp[3:4], axis=-1, keepdims=True)) + lam_init)


def _split_q12(q):
    qf = q.astype(F32)
    lane = lax.broadcasted_iota(jnp.int32, qf.shape, 1)
    return jnp.concatenate([jnp.where(lane < DA, qf, 0.0), jnp.where(lane >= DA, qf, 0.0)],
                           axis=0).astype(BF)


def _subln(o, lam, g, lam_init):
    r = o.shape[0] // 2
    d = o[:r] - lam * o[r:]
    ms = jnp.mean(d * d, -1, keepdims=True)
    return d * lax.rsqrt(ms + EPS) * g * (1.0 - lam_init)


def _kth_largest_rows(sc_ref, nch, tk, ksel):
    rows = sc_ref.shape[0]

    def count(cands, strict):
        accs = [jnp.zeros((rows, tk), F32) for _ in cands]
        for c in range(nch):
            x = sc_ref[:, c * tk:(c + 1) * tk]
            accs = [a + jnp.where((x > cand) if strict else (x >= cand), 1.0, 0.0)
                    for a, cand in zip(accs, cands)]
        return [jnp.sum(a, -1, keepdims=True) for a in accs]

    return _kth_largest(count, (rows, 1), ksel, bits=2)


def _kth_largest_cols(sc_ref, nch, tk, ksel):
    cols = sc_ref.shape[1]

    def count(cands, strict):
        cand, = cands

        def body(c, acc):
            x = sc_ref[pl.ds(pl.multiple_of(c * tk, tk), tk), :]
            hit = (x > cand) if strict else (x >= cand)
            return acc + jnp.sum(jnp.where(hit, 1.0, 0.0).reshape(tk // 32, 32, cols), axis=0)
        acc = lax.fori_loop(0, nch, body, jnp.zeros((32, cols), F32))
        return [jnp.sum(acc, 0, keepdims=True)]

    return _kth_largest(count, (1, cols), ksel)


def _selected(x, t, need, eq_before, tri):
    eq = jnp.where(x == t, 1.0, 0.0)
    rank = eq_before + _nn(eq.astype(BF), tri)
    tie_taken = jnp.where(rank <= need, eq, 0.0)
    return jnp.where(x > t, 1.0, tie_taken), jnp.sum(eq, -1, keepdims=True)


def _tri(n, lower):
    r = lax.broadcasted_iota(jnp.int32, (n, n), 0)
    c = lax.broadcasted_iota(jnp.int32, (n, n), 1)
    return jnp.where((r >= c) if lower else (r <= c), 1.0, 0.0).astype(BF)


def _flash_key_major(npair, nchain, tq, qk, mask, v_t, s_ref, p_ref, al_ref, acc_ref):
    def scores(c, slot):
        for j in range(nchain):
            s_ref[slot, j] = qk(c, j)

    def softmax(c, slot, carry, last):
        ms, ls = carry
        ms_new, ls_new = [], []
        for j in range(nchain):
            s = mask(c, j, s_ref[slot, j], last)
            m_new = jnp.maximum(ms[j], jnp.max(s, 0, keepdims=True))
            p = jnp.exp(s - m_new)
            alpha = jnp.exp(ms[j] - m_new)
            al_ref[slot, j] = alpha
            ls_new.append(alpha * ls[j] + jnp.sum(p, 0, keepdims=True))
            p_ref[slot, j] = p.astype(BF)
            ms_new.append(m_new)
        return tuple(ms_new), tuple(ls_new)

    def fold(c, slot):
        for j in range(nchain):
            acc_ref[j] = al_ref[slot, j] * acc_ref[j] + _nn(v_t(c, j), p_ref[slot, j])

    acc_ref[...] = jnp.zeros(acc_ref.shape, F32)
    p_ref[1] = jnp.zeros(p_ref.shape[1:], BF)
    al_ref[1] = jnp.ones(al_ref.shape[1:], F32)
    scores(0, 0)

    def pair(c, carry, last):
        fold(jnp.maximum(c - 1, 0), 1)
        carry = softmax(c, 0, carry, last)
        scores(c + 1, 1)
        fold(c, 0)
        carry = softmax(c + 1, 1, carry, last)
        if not last:
            scores(c + 2, 0)
        return carry

    row = lambda v: tuple(jnp.full((1, tq), v, F32) for _ in range(nchain))
    carry = lax.fori_loop(0, npair - 1, lambda c, cr: pair(2 * c, cr, False), (row(NEG), row(0.0)))
    c_last = 2 * (npair - 1)
    _, ls = pair(c_last, carry, True)
    fold(c_last + 1, 1)
    return ls


def _diff_kernel(lam_ref, g_ref, qt_ref, a_ref, vt_ref, o_ref, qh_ref, s_ref, p_ref, al_ref, acc_ref,
                 *, tq, tk, lam_init):
    i = pl.program_id(1)
    lam = _lambda(lam_ref, lam_init)
    sub = lax.broadcasted_iota(jnp.int32, (128, tq), 0)
    key = lax.broadcasted_iota(jnp.int32, (tk, tq), 0)
    qry = i * tq + lax.broadcasted_iota(jnp.int32, (tk, tq), 1)
    nchain = 2 * H_A
    for h in range(H_A):
        qt = qt_ref[h * 128:(h + 1) * 128, :].astype(F32)
        qh_ref[2 * h] = jnp.where(sub < DA, qt, 0.0).astype(BF)
        qh_ref[2 * h + 1] = jnp.where(sub >= DA, qt, 0.0).astype(BF)

    def qk(c, j):
        return _nn(a_ref[pl.ds(pl.multiple_of(c * tk, tk), tk), (j // 2) * 256:(j // 2) * 256 + 128], qh_ref[j])

    def mask(c, j, s, last):
        return jnp.where(c * tk + key <= qry, s, NEG) if last else s

    def v_t(c, j):
        return vt_ref[(j // 2) * 128:(j // 2 + 1) * 128, pl.ds(pl.multiple_of(c * tk, tk), tk)]

    ls = _flash_key_major(i + 1, nchain, tq, qk, mask, v_t, s_ref, p_ref, al_ref, acc_ref)
    for h in range(H_A):
        d = acc_ref[2 * h] * (1.0 / ls[2 * h]) - lam * (acc_ref[2 * h + 1] * (1.0 / ls[2 * h + 1]))
        ms = jnp.mean(d * d, 0, keepdims=True)
        o = d * lax.rsqrt(ms + EPS) * g_ref[...] * (1.0 - lam_init)
        o_ref[:, h * 128:(h + 1) * 128] = o.T.astype(o_ref.dtype)


def _flash_scratch(nchain, feat, tk, tq):
    return [pltpu.VMEM((2, nchain, tk, tq), F32), pltpu.VMEM((2, nchain, tk, tq), BF),
            pltpu.VMEM((2, nchain, 1, tq), F32), pltpu.VMEM((nchain, feat, tq), F32)]


def _diff_attn_prompt(lam_e, g_col, qt, a_bf, vat, batch, seq, lam_init, tq=256, tk=128):
    nq = seq // tq
    return pl.pallas_call(
        functools.partial(_diff_kernel, tq=tq, tk=tk, lam_init=lam_init),
        grid=(batch, nq),
        in_specs=[_const_spec(lam_e.shape), _const_spec(g_col.shape),
                  pl.BlockSpec((512, tq), lambda b, i: (0, b * nq + i)),
                  pl.BlockSpec((seq, 1024), lambda b, i: (b, 0)),
                  pl.BlockSpec((512, seq), lambda b, i: (0, b))],
        out_specs=pl.BlockSpec((tq, 512), lambda b, i: (b * nq + i, 0)),
        out_shape=jax.ShapeDtypeStruct((batch * seq, 512), BF),
        scratch_shapes=[pltpu.VMEM((2 * H_A, 128, tq), BF)] + _flash_scratch(2 * H_A, DV_A, tk, tq),
        compiler_params=_params(2),
        name="diff_attn_prompt",
    )(lam_e, g_col, qt, a_bf, vat)


def _sparse_kernel(qbt_ref, qit_ref, wit_ref, b_ref, vbt_ref, o_ref, sc_ref, acc_ref, *, tq, tk, ksel):
    i = pl.program_id(1)
    nch = (i + 1) * (tq // tk)
    half = tk // 2
    key = lax.broadcasted_iota(jnp.int32, (half, tq), 0)
    qry = i * tq + lax.broadcasted_iota(jnp.int32, (half, tq), 1)
    w = wit_ref[...]

    def score_chunk(c, _):
        for u in range(2):
            r0 = pl.multiple_of(c * tk + u * half, half)
            kidx = b_ref[pl.ds(r0, half), 2 * DB:2 * DB + DI]
            acc = jnp.zeros((half, tq), F32)
            for h in range(HI):
                d = _nn(kidx, qit_ref[h * DI:(h + 1) * DI, :])
                acc = acc + w[h:h + 1, :] * jnp.maximum(d, 0.0)
            sc_ref[pl.ds(r0, half), :] = jnp.where(r0 + key <= qry, acc, -jnp.inf)
        return 0

    lax.fori_loop(0, nch, score_chunk, 0)
    t, need, tied = _kth_largest_cols(sc_ref, nch, tk, ksel)
    acc_ref[...] = jnp.zeros(acc_ref.shape, F32)

    ta = half

    def attend(c, carry, exact_ties):
        ms, ls, eq_before = carry
        r0 = pl.multiple_of(c * ta, ta)
        if exact_ties:
            x = sc_ref[pl.ds(r0, ta), :]
            eq = jnp.where(x == t, 1.0, 0.0)
            rank = eq_before + _nn(_tri(ta, True), eq.astype(BF))
            bias = jnp.where(x > t, 0.0, jnp.where(x == t, jnp.where(rank <= need, 0.0, NEG), NEG))
            eq_before = eq_before + jnp.sum(eq, 0, keepdims=True)
        kb = b_ref[pl.ds(r0, ta), 0:DB]
        vt = vbt_ref[:, pl.ds(r0, ta)]
        ms_new, ls_new = [], []
        for h in range(H_B):
            s = _nn(kb, qbt_ref[h * DB:(h + 1) * DB, :])
            s = (s + bias) if exact_ties else jnp.where(sc_ref[pl.ds(r0, ta), :] >= t, s, NEG)
            m_new = jnp.maximum(ms[h], jnp.max(s, 0, keepdims=True))
            p = jnp.exp(s - m_new)
            alpha = jnp.exp(ms[h] - m_new)
            ls_new.append(alpha * ls[h] + jnp.sum(p, 0, keepdims=True))
            acc_ref[h] = alpha * acc_ref[h] + _nn(vt, p.astype(BF))
            ms_new.append(m_new)
        return tuple(ms_new), tuple(ls_new), eq_before

    row = lambda v: tuple(jnp.full((1, tq), v, F32) for _ in range(H_B))
    init = (row(NEG), row(0.0), jnp.zeros((1, tq), F32))
    _, ls, _ = lax.cond(
        jnp.max(tied) > 0.0,
        lambda: lax.fori_loop(0, 2 * nch, lambda c, cr: attend(c, cr, True), init),
        lambda: lax.fori_loop(0, 2 * nch, lambda c, cr: attend(c, cr, False), init))
    for h in range(H_B):
        o = acc_ref[h] * (1.0 / ls[h])
        o_ref[:, h * DB:(h + 1) * DB] = o.T.astype(o_ref.dtype)


def _sparse_attn_prompt(qt, wit, b_bf, vbt, batch, seq, ksel, tq=256, tk=256):
    nq = seq // tq
    return pl.pallas_call(
        functools.partial(_sparse_kernel, tq=tq, tk=tk, ksel=ksel),
        grid=(batch, nq),
        in_specs=[pl.BlockSpec((512, tq), lambda b, i: (1, b * nq + i)),
                  pl.BlockSpec((512, tq), lambda b, i: (2, b * nq + i)),
                  pl.BlockSpec((16, tq), lambda b, i: (0, b * nq + i)),
                  pl.BlockSpec((seq, 320), lambda b, i: (b, 0)),
                  pl.BlockSpec((128, seq), lambda b, i: (0, b))],
        out_specs=pl.BlockSpec((tq, 512), lambda b, i: (b * nq + i, 0)),
        out_shape=jax.ShapeDtypeStruct((batch * seq, 512), BF),
        scratch_shapes=[pltpu.VMEM((seq, tq), F32), pltpu.VMEM((H_B, DB, tq), F32)],
        compiler_params=_params(2),
        name="sparse_attn_prompt",
    )(qt, qt, wit, b_bf, vbt)


def _sample_even_kernel(pt_ref, lam_ref, g_ref, q_ref, wi_ref, *rest, npg, page, ksel, lam_init):
    del pt_ref
    nblk = npg + 1
    a_pages = rest[:nblk]
    b_pages = rest[nblk:2 * nblk]
    o_ref = rest[2 * nblk]
    sc_ref = rest[2 * nblk + 1]
    r = q_ref.shape[0]
    lam = _lambda(lam_ref, lam_init)
    q = q_ref[...]

    def new_ok(rows):
        tok = lax.broadcasted_iota(jnp.int32, (rows, page), 0) % r
        return lax.broadcasted_iota(jnp.int32, (rows, page), 1) <= tok

    ok2 = new_ok(2 * r)
    for h in range(H_A):
        qq = _split_q12(q[:, h * 128:(h + 1) * 128])
        ss = [_nt(qq, a_pages[p][pl.ds(h, page, stride=2 * H_A), :].astype(BF)) for p in range(nblk)]
        ss[npg] = jnp.where(ok2, ss[npg], NEG)
        m = functools.reduce(jnp.maximum, ss)
        m = jnp.max(m, -1, keepdims=True)
        ps = [jnp.exp(s - m) for s in ss]
        l = jnp.sum(functools.reduce(lambda x, y: x + y, ps), -1, keepdims=True)
        acc = functools.reduce(lambda x, y: x + y, [
            _nn(ps[p].astype(BF), a_pages[p][pl.ds(H_A + h, page, stride=2 * H_A), :].astype(BF))
            for p in range(nblk)])
        o_ref[:, h * 128:(h + 1) * 128] = _subln(acc / l, lam, g_ref[...], lam_init)

    qi = jnp.concatenate([q[:, 1024 + h * DI:1024 + (h + 1) * DI] for h in range(HI)], axis=0).astype(BF)
    w = wi_ref[...]
    wcol = jnp.concatenate([w[:, h:h + 1] for h in range(HI)], axis=0)
    ok1 = new_ok(r)
    for p in range(nblk):
        d = jnp.maximum(_nn(qi, b_pages[p][2 * DB:2 * DB + DI, :].astype(BF)), 0.0) * wcol
        sc = d[0:r]
        for h in range(1, HI):
            sc = sc + d[h * r:(h + 1) * r]
        if p == npg:
            sc = jnp.where(ok1, sc, -jnp.inf)
        sc_ref[:, p * page:(p + 1) * page] = sc
    t, need, _ = _kth_largest_rows(sc_ref, nblk, page, ksel)

    tri = _tri(page, False)
    q4 = jnp.concatenate([q[:, 512 + h * DB:512 + (h + 1) * DB] for h in range(H_B)], axis=0).astype(BF)
    eq_before = jnp.zeros((r, 1), F32)
    ss, sels = [], []
    for p in range(nblk):
        sel, n_eq = _selected(sc_ref[:, p * page:(p + 1) * page], t, need, eq_before, tri)
        eq_before = eq_before + n_eq
        if p == npg:
            sel = jnp.where(ok1, sel, 0.0)
        sels.append(jnp.concatenate([sel] * H_B, axis=0) > 0.5)
        ss.append(_nn(q4, b_pages[p][0:DB, :].astype(BF)))
    m = functools.reduce(jnp.maximum, [jnp.where(sl, s, NEG) for sl, s in zip(sels, ss)])
    m = jnp.max(m, -1, keepdims=True)
    ps = [jnp.where(sl, jnp.exp(s - m), 0.0) for sl, s in zip(sels, ss)]
    l = jnp.sum(functools.reduce(lambda x, y: x + y, ps), -1, keepdims=True)
    acc = functools.reduce(lambda x, y: x + y, [
        _nt(ps[p].astype(BF), b_pages[p][DB:2 * DB, :].astype(BF)) for p in range(nblk)])
    o = acc / l
    for h in range(H_B):
        o_ref[:, 512 + h * DB:512 + (h + 1) * DB] = o[h * r:(h + 1) * r]


def _sample_even(page_table, lam_e, g, qs, wis, anew_pg, bnew_pg, cache_a_pg, cache_b_pg, ksel, lam_init):
    nreq, npg = page_table.shape
    page = cache_b_pg.shape[2]
    r = qs.shape[1]
    req = lambda a: pl.BlockSpec((None,) + a.shape[1:], lambda i, pt: (i, 0, 0))

    def page_spec(a, p):
        return pl.BlockSpec((None,) + a.shape[1:], lambda i, pt, p=p: (pt[i, p], 0, 0))

    in_specs = [_const_spec(lam_e.shape), _const_spec(g.shape), req(qs), req(wis)]
    in_specs += [page_spec(cache_a_pg, p) for p in range(npg)] + [req(anew_pg)]
    in_specs += [page_spec(cache_b_pg, p) for p in range(npg)] + [req(bnew_pg)]
    grid_spec = pltpu.PrefetchScalarGridSpec(
        num_scalar_prefetch=1, grid=(nreq,), in_specs=in_specs,
        out_specs=pl.BlockSpec((None, r, 1024), lambda i, pt: (i, 0, 0)),
        scratch_shapes=[pltpu.VMEM((r, (npg + 1) * page), F32)])
    return pl.pallas_call(
        functools.partial(_sample_even_kernel, npg=npg, page=page, ksel=ksel, lam_init=lam_init),
        grid_spec=grid_spec,
        out_shape=jax.ShapeDtypeStruct((nreq, r, 1024), F32),
        compiler_params=_params(1),
        name="sample_even",
    )(page_table, lam_e, g, qs, wis, *([cache_a_pg] * npg), anew_pg, *([cache_b_pg] * npg), bnew_pg)


def _outproj_kernel(*refs, n_lhs):
    lhs = refs[:n_lhs]
    w_ref, x_ref, g_ref, b_ref, o_ref = refs[n_lhs:]
    y = None
    k0 = 0
    for a_ref in lhs:
        kw = a_ref.shape[1]
        part = _nn(a_ref[...].astype(BF), w_ref[k0:k0 + kw, :])
        y = part if y is None else y + part
        k0 += kw
    o_ref[...] = _ln(ALPHA * x_ref[...] + y, g_ref[...], b_ref[...])


def _outproj_res_ln(lhs, w, x, g, b, tm=512):
    n = x.shape[0]
    tm = min(tm, n)
    row = lambda wd: pl.BlockSpec((tm, wd), lambda i: (i, 0))
    return pl.pallas_call(
        functools.partial(_outproj_kernel, n_lhs=len(lhs)),
        grid=(n // tm,),
        in_specs=[row(a.shape[1]) for a in lhs] + [_const_spec(w.shape), row(D_MODEL),
                                                   _const_spec(g.shape), _const_spec(b.shape)],
        out_specs=row(D_MODEL),
        out_shape=jax.ShapeDtypeStruct((n, D_MODEL), F32),
        compiler_params=_params(1),
        name="outproj_res_ln",
    )(*lhs, w, x, g, b)


def _mlp_kernel(x_ref, w1_ref, w2_ref, g_ref, b_ref, o_ref, acc_ref, *, ck):
    x = x_ref[...]
    xb = x.astype(BF)
    for c in range(D_FF // ck):
        h = jnp.maximum(_nn(xb, w1_ref[:, c * ck:(c + 1) * ck]), 0.0)
        part = _nn((h * h).astype(BF), w2_ref[c * ck:(c + 1) * ck, :])
        if c == 0:
            acc_ref[...] = part
        else:
            acc_ref[...] += part
    o_ref[...] = _ln(ALPHA * x + acc_ref[...], g_ref[...], b_ref[...])


def _mlp_res_ln(x, w1, w2, g, b, tm=512, ck=512):
    n = x.shape[0]
    tm = min(tm, n)
    row = pl.BlockSpec((tm, D_MODEL), lambda i: (i, 0))
    resident = lambda s: pl.BlockSpec(s, lambda i: (0, 0), pipeline_mode=pl.Buffered(1))
    return pl.pallas_call(
        functools.partial(_mlp_kernel, ck=ck),
        grid=(n // tm,),
        in_specs=[row, resident(w1.shape), resident(w2.shape), _const_spec(g.shape), _const_spec(b.shape)],
        out_specs=row,
        out_shape=jax.ShapeDtypeStruct((n, D_MODEL), F32),
        scratch_shapes=[pltpu.VMEM((tm, D_MODEL), F32)],
        compiler_params=_params(1),
        name="mlp_res_ln",
    )(x, w1, w2, g, b)


def _gelu(x):
    return 0.5 * x * (1.0 + jnp.tanh(math.sqrt(2.0 / math.pi) * (x + 0.044715 * (x * x * x))))


def _proj_odd_kernel(x_ref, w_ref, g_ref, b_ref, xc_ref, u_ref, vn_ref):
    h = _nn(x_ref[...].astype(BF), w_ref[...])
    xc_ref[...] = h[:, :MIX_C]
    u_ref[...] = _gelu(h[:, MIX_C:MIX_C + MIX_D])
    vn_ref[...] = _ln(_gelu(h[:, MIX_C + MIX_D:]), g_ref[...], b_ref[...])


def _proj_odd(x, w, g, b, tm=512):
    n = x.shape[0]
    tm = min(tm, n)
    row = lambda wd: pl.BlockSpec((tm, wd), lambda i: (i, 0))
    return pl.pallas_call(
        _proj_odd_kernel,
        grid=(n // tm,),
        in_specs=[row(D_MODEL), _const_spec(w.shape), _const_spec(g.shape), _const_spec(b.shape)],
        out_specs=[row(512), row(512), row(512)],
        out_shape=[jax.ShapeDtypeStruct((n, 512), F32)] * 3,
        compiler_params=_params(1),
        name="proj_odd",
    )(x, w, g, b)


def _pool_sgu_kernel(prev_ref, halo_ref, xc_ref, u_ref, vn_ref, wp_ref, sc_ref, ws_ref, bs_ref,
                     o_ref, ext_ref, *, start):
    t = pl.program_id(1)
    hal = prev_ref.shape[0]
    ext_ref[0:hal, :] = jnp.where(t == 0, prev_ref[...], halo_ref[...])
    ext_ref[hal:hal + CHUNK, :] = xc_ref[...]
    pos = start + t * CHUNK + lax.broadcasted_iota(jnp.int32, (CHUNK, 1), 0)
    for g, w in enumerate(POOL_WINDOWS):
        gs = slice(g * C_GROUP, (g + 1) * C_GROUP)
        acc = ext_ref[hal:hal + CHUNK, gs]
        for s in range(1, w):
            acc = acc + ext_ref[hal - s:hal - s + CHUNK, gs]
        cnt = jnp.minimum(w, pos + 1).astype(F32)
        pooled = acc / cnt - xc_ref[:, gs]
        c = _nn(pooled.astype(BF), wp_ref[g]) * sc_ref[:, gs]
        o_ref[:, gs] = c.astype(o_ref.dtype)
    r = lax.broadcasted_iota(jnp.int32, (CHUNK, CHUNK), 0)
    cc = lax.broadcasted_iota(jnp.int32, (CHUNK, CHUNK), 1)
    for g in range(D_GROUPS):
        gs = slice(g * 128, (g + 1) * 128)
        ws = jnp.where(r >= cc, ws_ref[g], 0.0).astype(BF)
        s = _nn(ws, vn_ref[:, gs].astype(BF)) + bs_ref[:, g:g + 1]
        o_ref[:, MIX_C + g * 128:MIX_C + (g + 1) * 128] = (u_ref[:, gs] * s).astype(o_ref.dtype)


def _pool_sgu_prompt(prev16, xc, u, vn, wp, scale, ws, bs_t, batch, seq, start):
    nt = seq // CHUNK
    hal = prev16.shape[1]
    per = CHUNK // hal
    row = pl.BlockSpec((CHUNK, 512), lambda b, t: (b * nt + t, 0))
    return pl.pallas_call(
        functools.partial(_pool_sgu_kernel, start=start),
        grid=(batch, nt),
        in_specs=[pl.BlockSpec((None, hal, 512), lambda b, t: (b, 0, 0)),
                  pl.BlockSpec((hal, 512), lambda b, t: (jnp.maximum((b * nt + t) * per - 1, 0), 0)),
                  row, row, row, _const_spec(wp.shape), _const_spec(scale.shape),
                  _const_spec(ws.shape), _const_spec(bs_t.shape)],
        out_specs=pl.BlockSpec((CHUNK, 1024), lambda b, t: (b * nt + t, 0)),
        out_shape=jax.ShapeDtypeStruct((batch * seq, 1024), BF),
        scratch_shapes=[pltpu.VMEM((hal + CHUNK, 512), F32)],
        compiler_params=_params(2),
        name="pool_sgu_prompt",
    )(prev16, xc, xc, u, vn, wp, scale, ws, bs_t)


def _pool_sgu_sample_kernel(ws_ref, bs_ref, prev_ref, xc_ref, u_ref, vn_ref, wp_ref, sc_ref, o_ref,
                            *, start):
    nprev = prev_ref.shape[0]
    ntok = xc_ref.shape[0]
    for t in range(ntok):
        for g, w in enumerate(POOL_WINDOWS):
            gs = slice(g * C_GROUP, (g + 1) * C_GROUP)
            acc = None
            for s in range(w):
                j = nprev + t - s
                slab = prev_ref[j, :, gs] if j < nprev else xc_ref[j - nprev, :, gs]
                acc = slab if acc is None else acc + slab
            cnt = float(min(w, start + t + 1))
            pooled = acc / cnt - xc_ref[t, :, gs]
            o_ref[t, :, gs] = _nn(pooled.astype(BF), wp_ref[g]) * sc_ref[:, gs]
        for g in range(D_GROUPS):
            gs = slice(g * 128, (g + 1) * 128)
            s = None
            for j in range(t + 1):
                term = ws_ref[(g * ntok + t) * ntok + j] * vn_ref[j, :, gs]
                s = term if s is None else s + term
            s = s + bs_ref[g * ntok + t]
            o_ref[t, :, MIX_C + g * 128:MIX_C + (g + 1) * 128] = u_ref[t, :, gs] * s


def _pool_sgu_sample(ws_small, bs_small, prev_t, xc_t, u_t, vn_t, wp, scale, start):
    ntok, nreq, _ = xc_t.shape
    smem = pl.BlockSpec(memory_space=pltpu.SMEM)
    return pl.pallas_call(
        functools.partial(_pool_sgu_sample_kernel, start=start),
        grid=(1,),
        in_specs=[smem, smem, _const_spec(prev_t.shape), _const_spec(xc_t.shape), _const_spec(u_t.shape),
                  _const_spec(vn_t.shape), _const_spec(wp.shape), _const_spec(scale.shape)],
        out_specs=_const_spec((ntok, nreq, 1024)),
        out_shape=jax.ShapeDtypeStruct((ntok, nreq, 1024), F32),
        compiler_params=_params(1),
        name="pool_sgu_sample",
    )(ws_small, bs_small, prev_t, xc_t, u_t, vn_t, wp, scale)


def _even_weights(w_in):
    q_a, k_a, v_a, q_b, k_b, v_b, q_i, k_i, w_i = jnp.split(
        w_in, [512, 1024, 1536, 2048, 2176, 2304, 2816, 2880], axis=1)
    wa = jnp.concatenate([k_a.reshape(D_MODEL, H_A, 2 * DA), v_a.reshape(D_MODEL, H_A, DV_A)],
                         -1).reshape(D_MODEL, H_A * (2 * DA + DV_A))
    wb = jnp.concatenate([k_b, v_b, k_i], 1)
    wq = jnp.concatenate([q_a, q_b, q_i], 1)
    ww = jnp.concatenate([w_i, jnp.zeros((D_MODEL, 128 - HI), w_in.dtype)], 1)
    qscale = jnp.concatenate([jnp.full((512,), DA ** -0.5, F32), jnp.full((512,), DB ** -0.5, F32),
                              jnp.full((512,), DI ** -0.5, F32)]).reshape(1, 1536)
    wwt = jnp.concatenate([w_i.T, jnp.zeros((16 - HI, D_MODEL), w_in.dtype)], 0)
    natural = (wa.astype(BF), wb.astype(BF), wq.astype(BF), ww.astype(BF), qscale)
    feature_major = (wq.T.astype(BF), v_a.T.astype(BF), v_b.T.astype(BF), wwt.astype(BF))
    return natural, feature_major


def _a_pages(x):
    n, page, _ = x.shape
    return x.reshape(n, page, H_A, 2, 128).transpose(0, 1, 3, 2, 4).reshape(n, page * 2 * H_A, 128)


def _pad_rows(x, rows):
    return jnp.pad(x, ((0, 0), (0, rows - x.shape[1]), (0, 0)))


def kernel(x_prompt, x_sample, cache_a, cache_b, state_pool, page_table, w_in_e, lam_e, subln_g, w_out_e,
           w_in_o, w_pool, pool_scale, sgu_g, sgu_b, w_s, b_s, w_out_o, w_mlp1, w_mlp2, ln_g, ln_b):
    batch, seq, _ = x_prompt.shape
    nreq, ntok, _ = x_sample.shape
    npg = page_table.shape[1]
    page = cache_a.shape[2]
    past = npg * page
    xp = x_prompt.reshape(batch * seq, D_MODEL)
    xs = x_sample.reshape(nreq * ntok, D_MODEL)
    outs = {k: [] for k in ("a_p", "b_p", "pool_p", "a_s", "b_s", "pool_s", "v_s")}
    rpad = 8

    for l in range(DEPTH):
        i = l // 2
        row2 = lambda v: v.reshape(1, -1)
        if l % 2 == 0:
            lam_init = 0.8 - 0.6 * math.exp(-0.3 * l)
            (wa, wb, wq, ww, qscale), (wqt, wvat, wvbt, wwt) = _even_weights(w_in_e[i])
            g = row2(subln_g[i])
            w_out = w_out_e[i].astype(BF)
            na, nb, abf, bbf, qt, vat, vbt, wit = _proj_even_t(xp, wa, wb, wqt, wvat, wvbt, wwt)
            o_a = _diff_attn_prompt(lam_e[i], subln_g[i].reshape(-1, 1), qt, abf, vat, batch, seq, lam_init)
            o_b = _sparse_attn_prompt(qt, wit, bbf, vbt, batch, seq, min(TOPK_MAX, seq // 4))
            mix_p = [o_a, o_b]
            outs["a_p"].append(na.reshape(batch, seq, H_A, 2 * DA + DV_A))
            outs["b_p"].append(nb.reshape(batch, seq, 2 * DB + DI))
            nas, nbs, _, _, qs, wis = _proj_even(xs, wa, wb, wq, ww, qscale)
            o_s = _sample_even(
                page_table, lam_e[i], g,
                _pad_rows(qs.astype(F32).reshape(nreq, ntok, 1536), rpad),
                _pad_rows(wis.reshape(nreq, ntok, 128), rpad),
                _a_pages(_pad_rows(nas.reshape(nreq, ntok, 1024), page)),
                _pad_rows(nbs.reshape(nreq, ntok, 320), page).transpose(0, 2, 1),
                _a_pages(cache_a[i].reshape(-1, page, 1024)), cache_b[i].transpose(0, 2, 1),
                min(TOPK_MAX, (past + ntok) // 4), lam_init)
            mix_s = [o_s[:, :ntok].reshape(nreq * ntok, 1024)]
            outs["a_s"].append(nas.reshape(nreq, ntok, H_A, 2 * DA + DV_A))
            outs["b_s"].append(nbs.reshape(nreq, ntok, 2 * DB + DI))
        else:
            w_in = w_in_o[i].astype(BF)
            w_out = w_out_o[i].astype(BF)
            wp = w_pool[i].astype(BF)
            scale = row2(pool_scale[i])
            sg, sb = row2(sgu_g[i]), row2(sgu_b[i])
            xc, u, vn = _proj_odd(xp, w_in, sg, sb)
            prev16 = jnp.zeros((batch, POOL_BUF + 1, MIX_C), F32)
            mix_p = [_pool_sgu_prompt(prev16, xc, u, vn, wp, scale, w_s[i], b_s[i].T, batch, seq, 0)]
            outs["pool_p"].append(xc.reshape(batch, seq, MIX_C)[:, seq - POOL_BUF:])
            xcs, us, vns = _proj_odd(xs, w_in, sg, sb)
            tmaj = lambda v: v.reshape(nreq, ntok, -1).transpose(1, 0, 2)
            m_t = _pool_sgu_sample(
                w_s[i][:, :ntok, :ntok].reshape(-1), b_s[i][:, :ntok].reshape(-1),
                state_pool[i].transpose(1, 0, 2), tmaj(xcs), tmaj(us), tmaj(vns), wp, scale, past)
            mix_s = [m_t.transpose(1, 0, 2).reshape(nreq * ntok, 1024)]
            ext = jnp.concatenate([state_pool[i], xcs.reshape(nreq, ntok, MIX_C)], 1)
            outs["pool_s"].append(ext[:, ext.shape[1] - POOL_BUF:])
            outs["v_s"].append(vns.reshape(nreq, ntok, MIX_D))
        g0, b0, g1, b1 = row2(ln_g[l, 0]), row2(ln_b[l, 0]), row2(ln_g[l, 1]), row2(ln_b[l, 1])
        w1, w2 = w_mlp1[l].astype(BF), w_mlp2[l].astype(BF)
        xp = _outproj_res_ln(mix_p, w_out, xp, g0, b0)
        xp = _mlp_res_ln(xp, w1, w2, g1, b1)
        xs = _outproj_res_ln(mix_s, w_out, xs, g0, b0)
        xs = _mlp_res_ln(xs, w1, w2, g1, b1)

    st = lambda k: jnp.stack(outs[k])
    return (xp.reshape(batch, seq, D_MODEL), xs.reshape(nreq, ntok, D_MODEL), st("a_p"), st("b_p"),
            st("pool_p"), st("a_s"), st("b_s"), st("pool_s"), st("v_s"))
```

```python
import functools
import math

import jax
import jax.numpy as jnp
from jax import lax
from jax.experimental import pallas as pl
from jax.experimental.pallas import tpu as pltpu

D_MODEL = 1024
H_A = 4
DA = 64
DV_A = 128
H_B = 4
DB = 128
HI = 8
DI = 64
TOPK_MAX = 256
MIX_C = 512
MIX_D = 512
POOL_WINDOWS = (2, 4, 8, 16)
C_GROUP = 128
POOL_BUF = 15
CHUNK = 128
D_GROUPS = 4
D_FF = 4096
DEPTH = 2
ALPHA = (2 * DEPTH) ** 0.25
EPS = 1e-5

BF = jnp.bfloat16
F32 = jnp.float32
NEG = -1e30
INT_MIN = -(2 ** 31)
FLT_LOWEST = -3.4028234663852886e38
MANY = 1e9
VMEM_LIMIT_BYTES = 56 * 1024 * 1024


def _params(n_axes):
    return pltpu.CompilerParams(dimension_semantics=("arbitrary",) * n_axes,
                                vmem_limit_bytes=VMEM_LIMIT_BYTES)


def _nn(a, b):
    return jnp.dot(a, b, preferred_element_type=F32)


def _nt(a, b):
    return lax.dot_general(a, b, (((1,), (1,)), ((), ())), preferred_element_type=F32)


def _ln(z, g, b):
    mu = jnp.mean(z, -1, keepdims=True)
    d = z - mu
    var = jnp.mean(d * d, -1, keepdims=True)
    return d * lax.rsqrt(var + EPS) * g + b


def _const_spec(shape):
    nd = len(shape)
    return pl.BlockSpec(shape, lambda *_: (0,) * nd)


def _key_to_float(k):
    return lax.bitcast_convert_type(jnp.where(k < 0, k ^ jnp.int32(0x7FFFFFFF), k), F32)


def _kth_largest(count, shape, ksel, bits=1):
    n_finite, = count([jnp.full(shape, -jnp.inf, F32)], True)
    c0, = count([jnp.zeros(shape, F32)], False)
    nonneg = c0 >= ksel
    k0 = jnp.where(nonneg, jnp.int32(0), jnp.int32(INT_MIN))
    n0 = jnp.where(nonneg, c0, MANY)

    def refine(carry, shift, nbits):
        k, n_ge = carry
        cands = [k | lax.shift_left(jnp.int32(d), shift) for d in range(1, 2 ** nbits)]
        for cand, n_cand in zip(cands, count([_key_to_float(c) for c in cands], False)):
            take = n_cand >= ksel
            k, n_ge = jnp.where(take, cand, k), jnp.where(take, n_cand, n_ge)
        return k, n_ge

    npass, rest = divmod(31, bits)
    carry = lax.fori_loop(0, npass, lambda it, cr: refine(cr, jnp.int32(31 - bits) - bits * it, bits),
                          (k0, n0))
    if rest:
        carry = refine(carry, jnp.int32(0), rest)
    k, n_ge = carry
    short = n_finite < ksel
    t = jnp.where(short, FLT_LOWEST, _key_to_float(k))
    n_above, = count([t], True)
    need = jnp.where(short, MANY, ksel - n_above)
    tied = jnp.where(short, 0.0, jnp.where(n_ge > ksel, 1.0, 0.0))
    return t, need, tied


def _proj_even_kernel(x_ref, wa_ref, wb_ref, wq_ref, ww_ref, qs_ref,
                      na_ref, nb_ref, abf_ref, bbf_ref, q_ref, wi_ref):
    x = x_ref[...].astype(BF)
    a = _nn(x, wa_ref[...])
    na_ref[...] = a
    abf_ref[...] = a.astype(BF)
    b = _nn(x, wb_ref[...])
    nb_ref[...] = b
    bbf_ref[...] = b.astype(BF)
    q_ref[...] = (_nn(x, wq_ref[...]) * qs_ref[...]).astype(BF)
    wi_ref[...] = _nn(x, ww_ref[...]) * (HI ** -0.5)


def _proj_even(x, wa, wb, wq, ww, qscale, tm=256):
    n = x.shape[0]
    tm = min(tm, n)
    row = lambda w: pl.BlockSpec((tm, w), lambda i: (i, 0))
    return pl.pallas_call(
        _proj_even_kernel,
        grid=(n // tm,),
        in_specs=[row(D_MODEL), _const_spec(wa.shape), _const_spec(wb.shape),
                  _const_spec(wq.shape), _const_spec(ww.shape), _const_spec(qscale.shape)],
        out_specs=[row(1024), row(320), row(1024), row(320), row(1536), row(128)],
        out_shape=[jax.ShapeDtypeStruct((n, 1024), F32), jax.ShapeDtypeStruct((n, 320), F32),
                   jax.ShapeDtypeStruct((n, 1024), BF), jax.ShapeDtypeStruct((n, 320), BF),
                   jax.ShapeDtypeStruct((n, 1536), BF), jax.ShapeDtypeStruct((n, 128), F32)],
        compiler_params=_params(1),
        name="proj_even",
    )(x, wa, wb, wq, ww, qscale)


def _proj_even_t_kernel(x_ref, wa_ref, wb_ref, wqt_ref, wvat_ref, wvbt_ref, wwt_ref,
                        na_ref, nb_ref, abf_ref, bbf_ref, qt_ref, vat_ref, vbt_ref, wit_ref):
    x = x_ref[...]
    xb = x.astype(BF)
    xt = x.T.astype(BF)
    a = _nn(xb, wa_ref[...])
    tm = a.shape[0]
    for h in range(H_A):
        for part in range(2):
            na_ref[pl.ds(H_A * part + h, tm, stride=2 * H_A), :] = a[:, h * 256 + part * 128:h * 256 + (part + 1) * 128]
    abf_ref[...] = a.astype(BF)
    b = _nn(xb, wb_ref[...])
    nb_ref[...] = b
    bbf_ref[...] = b.astype(BF)
    qt = _nn(wqt_ref[...], xt)
    qt_ref[0:512, :] = (qt[0:512] * (DA ** -0.5)).astype(BF)
    qt_ref[512:1024, :] = (qt[512:1024] * (DB ** -0.5)).astype(BF)
    qt_ref[1024:1536, :] = (qt[1024:1536] * (DI ** -0.5)).astype(BF)
    vat_ref[...] = _nn(wvat_ref[...], xt).astype(BF)
    vbt_ref[...] = _nn(wvbt_ref[...], xt).astype(BF)
    wit_ref[...] = _nn(wwt_ref[...], xt) * (HI ** -0.5)


def _proj_even_t(x, wa, wb, wqt, wvat, wvbt, wwt, tm=512):
    n = x.shape[0]
    row = lambda w: pl.BlockSpec((tm, w), lambda i: (i, 0))
    col = lambda h: pl.BlockSpec((h, tm), lambda i: (0, i))
    return pl.pallas_call(
        _proj_even_t_kernel,
        grid=(n // tm,),
        in_specs=[row(D_MODEL)] + [_const_spec(w.shape) for w in (wa, wb, wqt, wvat, wvbt, wwt)],
        out_specs=[pl.BlockSpec((2 * H_A * tm, 128), lambda i: (i, 0)), row(320), row(1024), row(320),
                   col(1536), col(512), col(128), col(16)],
        out_shape=[jax.ShapeDtypeStruct((2 * H_A * n, 128), F32), jax.ShapeDtypeStruct((n, 320), F32),
                   jax.ShapeDtypeStruct((n, 1024), BF), jax.ShapeDtypeStruct((n, 320), BF),
                   jax.ShapeDtypeStruct((1536, n), BF), jax.ShapeDtypeStruct((512, n), BF),
                   jax.ShapeDtypeStruct((128, n), BF), jax.ShapeDtypeStruct((16, n), F32)],
        compiler_params=_params(1),
        name="proj_even_t",
    )(x, wa, wb, wqt, wvat, wvbt, wwt)


def _lambda(lam_ref, lam_init):
    lp = lam_ref[...]
    return (jnp.exp(jnp.sum(lp[0:1] * lp[1:2], axis=-1, keepdims=True))
            - jnp.exp(jnp.sum(lp[2:3] * lp[3:4], axis=-1, keepdims=True)) + lam_init)


def _split_q12(q):
    qf = q.astype(F32)
    lane = lax.broadcasted_iota(jnp.int32, qf.shape, 1)
    return jnp.concatenate([jnp.where(lane < DA, qf, 0.0), jnp.where(lane >= DA, qf, 0.0)],
                           axis=0).astype(BF)


def _subln(o, lam, g, lam_init):
    r = o.shape[0] // 2
    d = o[:r] - lam * o[r:]
    ms = jnp.mean(d * d, -1, keepdims=True)
    return d * lax.rsqrt(ms + EPS) * g * (1.0 - lam_init)


def _kth_largest_rows(sc_ref, nch, tk, ksel):
    rows = sc_ref.shape[0]

    def count(cands, strict):
        accs = [jnp.zeros((rows, tk), F32) for _ in cands]
        for c in range(nch):
            x = sc_ref[:, c * tk:(c + 1) * tk]
            accs = [a + jnp.where((x > cand) if strict else (x >= cand), 1.0, 0.0)
                    for a, cand in zip(accs, cands)]
        return [jnp.sum(a, -1, keepdims=True) for a in accs]

    return _kth_largest(count, (rows, 1), ksel, bits=2)


def _kth_largest_cols(sc_ref, nch, tk, ksel):
    cols = sc_ref.shape[1]

    def count(cands, strict):
        cand, = cands

        def body(c, acc):
            x = sc_ref[pl.ds(pl.multiple_of(c * tk, tk), tk), :]
            hit = (x > cand) if strict else (x >= cand)
            return acc + jnp.sum(jnp.where(hit, 1.0, 0.0).reshape(tk // 32, 32, cols), axis=0)
        acc = lax.fori_loop(0, nch, body, jnp.zeros((32, cols), F32))
        return [jnp.sum(acc, 0, keepdims=True)]

    return _kth_largest(count, (1, cols), ksel)


def _selected(x, t, need, eq_before, tri):
    eq = jnp.where(x == t, 1.0, 0.0)
    rank = eq_before + _nn(eq.astype(BF), tri)
    tie_taken = jnp.where(rank <= need, eq, 0.0)
    return jnp.where(x > t, 1.0, tie_taken), jnp.sum(eq, -1, keepdims=True)


def _tri(n, lower):
    r = lax.broadcasted_iota(jnp.int32, (n, n), 0)
    c = lax.broadcasted_iota(jnp.int32, (n, n), 1)
    return jnp.where((r >= c) if lower else (r <= c), 1.0, 0.0).astype(BF)


def _flash_key_major(npair, nlast, nchain, tq, qk, mask, v_t, s_ref, p_ref, al_ref, acc_ref):
    def scores(c, slot):
        for j in range(nchain):
            s_ref[slot, j] = qk(c, j)

    def softmax(c, slot, carry, last):
        ms, ls = carry
        ms_new, ls_new = [], []
        for j in range(nchain):
            s = mask(c, j, s_ref[slot, j], last)
            m_new = jnp.maximum(ms[j], jnp.max(s, 0, keepdims=True))
            p = jnp.exp(s - m_new)
            alpha = jnp.exp(ms[j] - m_new)
            al_ref[slot, j] = alpha
            ls_new.append(alpha * ls[j] + jnp.sum(p, 0, keepdims=True))
            p_ref[slot, j] = p.astype(BF)
            ms_new.append(m_new)
        return tuple(ms_new), tuple(ls_new)

    def fold(c, slot):
        for j in range(nchain):
            acc_ref[j] = al_ref[slot, j] * acc_ref[j] + _nn(v_t(c, j), p_ref[slot, j])

    acc_ref[...] = jnp.zeros(acc_ref.shape, F32)
    p_ref[1] = jnp.zeros(p_ref.shape[1:], BF)
    al_ref[1] = jnp.ones(al_ref.shape[1:], F32)
    scores(0, 0)

    def pair(c, carry, last, final):
        fold(jnp.maximum(c - 1, 0), 1)
        carry = softmax(c, 0, carry, last)
        scores(c + 1, 1)
        fold(c, 0)
        carry = softmax(c + 1, 1, carry, last)
        if not final:
            scores(c + 2, 0)
        return carry

    row = lambda v: tuple(jnp.full((1, tq), v, F32) for _ in range(nchain))
    n_plain = npair - nlast
    carry = lax.fori_loop(0, n_plain, lambda c, cr: pair(2 * c, cr, False, False), (row(NEG), row(0.0)))
    for d in range(nlast):
        carry = pair(2 * (n_plain + d), carry, True, d == nlast - 1)
    fold(2 * npair - 1, 1)
    return carry[1]


def _diff_kernel(lam_ref, g_ref, qt_ref, a_ref, vt_ref, o_ref, qh_ref, s_ref, p_ref, al_ref, acc_ref,
                 *, tq, tk, lam_init):
    i = pl.program_id(1)
    lam = _lambda(lam_ref, lam_init)
    sub = lax.broadcasted_iota(jnp.int32, (128, tq), 0)
    key = lax.broadcasted_iota(jnp.int32, (tk, 2 * tq), 0)
    qry = i * tq + lax.broadcasted_iota(jnp.int32, (tk, 2 * tq), 1) % tq
    for h in range(H_A):
        qt = qt_ref[h * 128:(h + 1) * 128, :].astype(F32)
        qh_ref[h, :, 0:tq] = jnp.where(sub < DA, qt, 0.0).astype(BF)
        qh_ref[h, :, tq:2 * tq] = jnp.where(sub >= DA, qt, 0.0).astype(BF)

    def qk(c, h):
        return _nn(a_ref[pl.ds(pl.multiple_of(c * tk, tk), tk), h * 256:h * 256 + 128], qh_ref[h])

    def mask(c, h, s, last):
        return jnp.where(c * tk + key <= qry, s, NEG) if last else s

    def v_t(c, h):
        return vt_ref[h * 128:(h + 1) * 128, pl.ds(pl.multiple_of(c * tk, tk), tk)]

    nlast = tq // (2 * tk)
    ls = _flash_key_major((i + 1) * nlast, nlast, H_A, 2 * tq, qk, mask, v_t, s_ref, p_ref, al_ref, acc_ref)
    for h in range(H_A):
        o12 = acc_ref[h] * (1.0 / ls[h])
        d = o12[:, 0:tq] - lam * o12[:, tq:2 * tq]
        ms = jnp.mean(d * d, 0, keepdims=True)
        o = d * lax.rsqrt(ms + EPS) * g_ref[...] * (1.0 - lam_init)
        o_ref[:, h * 128:(h + 1) * 128] = o.T.astype(o_ref.dtype)


def _flash_scratch(nchain, feat, tk, tq):
    return [pltpu.VMEM((2, nchain, tk, tq), F32), pltpu.VMEM((2, nchain, tk, tq), BF),
            pltpu.VMEM((2, nchain, 1, tq), F32), pltpu.VMEM((nchain, feat, tq), F32)]


def _diff_attn_prompt(lam_e, g_col, qt, a_bf, vat, batch, seq, lam_init, tq=256, tk=128):
    nq = seq // tq
    return pl.pallas_call(
        functools.partial(_diff_kernel, tq=tq, tk=tk, lam_init=lam_init),
        grid=(batch, nq),
        in_specs=[_const_spec(lam_e.shape), _const_spec(g_col.shape),
                  pl.BlockSpec((512, tq), lambda b, i: (0, b * nq + i)),
                  pl.BlockSpec((seq, 1024), lambda b, i: (b, 0)),
                  pl.BlockSpec((512, seq), lambda b, i: (0, b))],
        out_specs=pl.BlockSpec((tq, 512), lambda b, i: (b * nq + i, 0)),
        out_shape=jax.ShapeDtypeStruct((batch * seq, 512), BF),
        scratch_shapes=[pltpu.VMEM((H_A, 128, 2 * tq), BF)] + _flash_scratch(H_A, DV_A, tk, 2 * tq),
        compiler_params=_params(2),
        name="diff_attn_prompt",
    )(lam_e, g_col, qt, a_bf, vat)


def _sparse_kernel(qbt_ref, qit_ref, wit_ref, b_ref, vbt_ref, o_ref, sc_ref, qiw_ref, qbw_ref,
                   s_ref, p_ref, al_ref, acc_ref, *, tq, tk, ksel):
    i = pl.program_id(1)
    nch = (i + 1) * (tq // tk)
    ta = tk // 2
    key = lax.broadcasted_iota(jnp.int32, (ta, tq), 0)
    qry = i * tq + lax.broadcasted_iota(jnp.int32, (ta, tq), 1)
    w = wit_ref[...]
    for h in range(HI):
        qiw_ref[:, h * tq:(h + 1) * tq] = qit_ref[h * DI:(h + 1) * DI, :]
    for h in range(H_B):
        qbw_ref[:, h * tq:(h + 1) * tq] = qbt_ref[h * DB:(h + 1) * DB, :]
    heads = lambda x: jnp.concatenate([x] * H_B, axis=1)

    def score_chunk(c, _):
        for u in range(2):
            r0 = pl.multiple_of(c * tk + u * ta, ta)
            d = jnp.maximum(_nn(b_ref[pl.ds(r0, ta), 2 * DB:2 * DB + DI], qiw_ref[...]), 0.0)
            acc = w[0:1, :] * d[:, 0:tq]
            for h in range(1, HI):
                acc = acc + w[h:h + 1, :] * d[:, h * tq:(h + 1) * tq]
            sc_ref[pl.ds(r0, ta), :] = jnp.where(r0 + key <= qry, acc, -jnp.inf)
        return 0

    lax.fori_loop(0, nch, score_chunk, 0)
    t, need, tied = _kth_largest_cols(sc_ref, nch, tk, ksel)

    def qk(c, _):
        return _nn(b_ref[pl.ds(pl.multiple_of(c * ta, ta), ta), 0:DB], qbw_ref[...])

    def v_t(c, _):
        return vbt_ref[:, pl.ds(pl.multiple_of(c * ta, ta), ta)]

    def to_bias(exact_ties):
        def body(c, eq_before):
            for u in range(2):
                rows = pl.ds(pl.multiple_of(c * tk + u * ta, ta), ta)
                x = sc_ref[rows, :]
                if exact_ties:
                    eq = jnp.where(x == t, 1.0, 0.0)
                    rank = eq_before + _nn(_tri(ta, True), eq.astype(BF))
                    sc_ref[rows, :] = jnp.where(
                        x > t, 0.0, jnp.where(x == t, jnp.where(rank <= need, 0.0, NEG), NEG))
                    eq_before = eq_before + jnp.sum(eq, 0, keepdims=True)
                else:
                    sc_ref[rows, :] = jnp.where(x >= t, 0.0, NEG)
            return eq_before
        return lax.fori_loop(0, nch, body, jnp.zeros((1, tq), F32))

    lax.cond(jnp.max(tied) > 0.0, lambda: to_bias(True), lambda: to_bias(False))

    def add_bias(c, _, s, last):
        return s + heads(sc_ref[pl.ds(pl.multiple_of(c * ta, ta), ta), :])

    l, = _flash_key_major(nch, 1, 1, H_B * tq, qk, add_bias, v_t, s_ref, p_ref, al_ref, acc_ref)
    o = acc_ref[0] * (1.0 / l)
    for h in range(H_B):
        o_ref[:, h * DB:(h + 1) * DB] = o[:, h * tq:(h + 1) * tq].T.astype(o_ref.dtype)


def _sparse_attn_prompt(qt, wit, b_bf, vbt, batch, seq, ksel, tq=256, tk=256):
    nq = seq // tq
    return pl.pallas_call(
        functools.partial(_sparse_kernel, tq=tq, tk=tk, ksel=ksel),
        grid=(batch, nq),
        in_specs=[pl.BlockSpec((512, tq), lambda b, i: (1, b * nq + i)),
                  pl.BlockSpec((512, tq), lambda b, i: (2, b * nq + i)),
                  pl.BlockSpec((16, tq), lambda b, i: (0, b * nq + i)),
                  pl.BlockSpec((seq, 320), lambda b, i: (b, 0)),
                  pl.BlockSpec((128, seq), lambda b, i: (0, b))],
        out_specs=pl.BlockSpec((tq, 512), lambda b, i: (b * nq + i, 0)),
        out_shape=jax.ShapeDtypeStruct((batch * seq, 512), BF),
        scratch_shapes=[pltpu.VMEM((seq, tq), F32), pltpu.VMEM((DI, HI * tq), BF),
                        pltpu.VMEM((DB, H_B * tq), BF)] + _flash_scratch(1, DB, tk // 2, H_B * tq),
        compiler_params=_params(2),
        name="sparse_attn_prompt",
    )(qt, qt, wit, b_bf, vbt)


def _sample_even_kernel(pt_ref, lam_ref, g_ref, q_ref, wi_ref, *rest, npg, page, ksel, lam_init):
    del pt_ref
    nblk = npg + 1
    a_pages = rest[:nblk]
    b_pages = rest[nblk:2 * nblk]
    o_ref = rest[2 * nblk]
    sc_ref = rest[2 * nblk + 1]
    r = q_ref.shape[0]
    lam = _lambda(lam_ref, lam_init)
    q = q_ref[...]

    def new_ok(rows):
        tok = lax.broadcasted_iota(jnp.int32, (rows, page), 0) % r
        return lax.broadcasted_iota(jnp.int32, (rows, page), 1) <= tok

    ok2 = new_ok(2 * r)
    for h in range(H_A):
        qq = _split_q12(q[:, h * 128:(h + 1) * 128])
        ss = [_nt(qq, a_pages[p][pl.ds(h, page, stride=2 * H_A), :].astype(BF)) for p in range(nblk)]
        ss[npg] = jnp.where(ok2, ss[npg], NEG)
        m = functools.reduce(jnp.maximum, ss)
        m = jnp.max(m, -1, keepdims=True)
        ps = [jnp.exp(s - m) for s in ss]
        l = jnp.sum(functools.reduce(lambda x, y: x + y, ps), -1, keepdims=True)
        acc = functools.reduce(lambda x, y: x + y, [
            _nn(ps[p].astype(BF), a_pages[p][pl.ds(H_A + h, page, stride=2 * H_A), :].astype(BF))
            for p in range(nblk)])
        o_ref[:, h * 128:(h + 1) * 128] = _subln(acc / l, lam, g_ref[...], lam_init)

    qi = jnp.concatenate([q[:, 1024 + h * DI:1024 + (h + 1) * DI] for h in range(HI)], axis=0).astype(BF)
    w = wi_ref[...]
    wcol = jnp.concatenate([w[:, h:h + 1] for h in range(HI)], axis=0)
    ok1 = new_ok(r)
    for p in range(nblk):
        d = jnp.maximum(_nn(qi, b_pages[p][2 * DB:2 * DB + DI, :].astype(BF)), 0.0) * wcol
        sc = d[0:r]
        for h in range(1, HI):
            sc = sc + d[h * r:(h + 1) * r]
        if p == npg:
            sc = jnp.where(ok1, sc, -jnp.inf)
        sc_ref[:, p * page:(p + 1) * page] = sc
    t, need, _ = _kth_largest_rows(sc_ref, nblk, page, ksel)

    tri = _tri(page, False)
    q4 = jnp.concatenate([q[:, 512 + h * DB:512 + (h + 1) * DB] for h in range(H_B)], axis=0).astype(BF)
    eq_before = jnp.zeros((r, 1), F32)
    ss, sels = [], []
    for p in range(nblk):
        sel, n_eq = _selected(sc_ref[:, p * page:(p + 1) * page], t, need, eq_before, tri)
        eq_before = eq_before + n_eq
        if p == npg:
            sel = jnp.where(ok1, sel, 0.0)
        sels.append(jnp.concatenate([sel] * H_B, axis=0) > 0.5)
        ss.append(_nn(q4, b_pages[p][0:DB, :].astype(BF)))
    m = functools.reduce(jnp.maximum, [jnp.where(sl, s, NEG) for sl, s in zip(sels, ss)])
    m = jnp.max(m, -1, keepdims=True)
    ps = [jnp.where(sl, jnp.exp(s - m), 0.0) for sl, s in zip(sels, ss)]
    l = jnp.sum(functools.reduce(lambda x, y: x + y, ps), -1, keepdims=True)
    acc = functools.reduce(lambda x, y: x + y, [
        _nt(ps[p].astype(BF), b_pages[p][DB:2 * DB, :].astype(BF)) for p in range(nblk)])
    o = acc / l
    for h in range(H_B):
        o_ref[:, 512 + h * DB:512 + (h + 1) * DB] = o[h * r:(h + 1) * r]


def _sample_even(page_table, lam_e, g, qs, wis, anew_pg, bnew_pg, cache_a_pg, cache_b_pg, ksel, lam_init):
    nreq, npg = page_table.shape
    page = cache_b_pg.shape[2]
    r = qs.shape[1]
    req = lambda a: pl.BlockSpec((None,) + a.shape[1:], lambda i, pt: (i, 0, 0))

    def page_spec(a, p):
        return pl.BlockSpec((None,) + a.shape[1:], lambda i, pt, p=p: (pt[i, p], 0, 0))

    in_specs = [_const_spec(lam_e.shape), _const_spec(g.shape), req(qs), req(wis)]
    in_specs += [page_spec(cache_a_pg, p) for p in range(npg)] + [req(anew_pg)]
    in_specs += [page_spec(cache_b_pg, p) for p in range(npg)] + [req(bnew_pg)]
    grid_spec = pltpu.PrefetchScalarGridSpec(
        num_scalar_prefetch=1, grid=(nreq,), in_specs=in_specs,
        out_specs=pl.BlockSpec((None, r, 1024), lambda i, pt: (i, 0, 0)),
        scratch_shapes=[pltpu.VMEM((r, (npg + 1) * page), F32)])
    return pl.pallas_call(
        functools.partial(_sample_even_kernel, npg=npg, page=page, ksel=ksel, lam_init=lam_init),
        grid_spec=grid_spec,
        out_shape=jax.ShapeDtypeStruct((nreq, r, 1024), F32),
        compiler_params=_params(1),
        name="sample_even",
    )(page_table, lam_e, g, qs, wis, *([cache_a_pg] * npg), anew_pg, *([cache_b_pg] * npg), bnew_pg)


def _outproj_kernel(*refs, n_lhs):
    lhs = refs[:n_lhs]
    w_ref, x_ref, g_ref, b_ref, o_ref = refs[n_lhs:]
    y = None
    k0 = 0
    for a_ref in lhs:
        kw = a_ref.shape[1]
        part = _nn(a_ref[...].astype(BF), w_ref[k0:k0 + kw, :])
        y = part if y is None else y + part
        k0 += kw
    o_ref[...] = _ln(ALPHA * x_ref[...] + y, g_ref[...], b_ref[...])


def _outproj_res_ln(lhs, w, x, g, b, tm=512):
    n = x.shape[0]
    tm = min(tm, n)
    row = lambda wd: pl.BlockSpec((tm, wd), lambda i: (i, 0))
    return pl.pallas_call(
        functools.partial(_outproj_kernel, n_lhs=len(lhs)),
        grid=(n // tm,),
        in_specs=[row(a.shape[1]) for a in lhs] + [_const_spec(w.shape), row(D_MODEL),
                                                   _const_spec(g.shape), _const_spec(b.shape)],
        out_specs=row(D_MODEL),
        out_shape=jax.ShapeDtypeStruct((n, D_MODEL), F32),
        compiler_params=_params(1),
        name="outproj_res_ln",
    )(*lhs, w, x, g, b)


def _mlp_kernel(x_ref, w1_ref, w2_ref, g_ref, b_ref, o_ref, acc_ref, *, ck):
    x = x_ref[...]
    xb = x.astype(BF)
    for c in range(D_FF // ck):
        h = jnp.maximum(_nn(xb, w1_ref[:, c * ck:(c + 1) * ck]), 0.0)
        part = _nn((h * h).astype(BF), w2_ref[c * ck:(c + 1) * ck, :])
        if c == 0:
            acc_ref[...] = part
        else:
            acc_ref[...] += part
    o_ref[...] = _ln(ALPHA * x + acc_ref[...], g_ref[...], b_ref[...])


def _mlp_res_ln(x, w1, w2, g, b, tm=512, ck=512):
    n = x.shape[0]
    tm = min(tm, n)
    row = pl.BlockSpec((tm, D_MODEL), lambda i: (i, 0))
    resident = lambda s: pl.BlockSpec(s, lambda i: (0, 0), pipeline_mode=pl.Buffered(1))
    return pl.pallas_call(
        functools.partial(_mlp_kernel, ck=ck),
        grid=(n // tm,),
        in_specs=[row, resident(w1.shape), resident(w2.shape), _const_spec(g.shape), _const_spec(b.shape)],
        out_specs=row,
        out_shape=jax.ShapeDtypeStruct((n, D_MODEL), F32),
        scratch_shapes=[pltpu.VMEM((tm, D_MODEL), F32)],
        compiler_params=_params(1),
        name="mlp_res_ln",
    )(x, w1, w2, g, b)


def _gelu(x):
    return 0.5 * x * (1.0 + jnp.tanh(math.sqrt(2.0 / math.pi) * (x + 0.044715 * (x * x * x))))


def _proj_odd_kernel(x_ref, w_ref, g_ref, b_ref, xc_ref, u_ref, vn_ref):
    h = _nn(x_ref[...].astype(BF), w_ref[...])
    xc_ref[...] = h[:, :MIX_C]
    u_ref[...] = _gelu(h[:, MIX_C:MIX_C + MIX_D])
    vn_ref[...] = _ln(_gelu(h[:, MIX_C + MIX_D:]), g_ref[...], b_ref[...])


def _proj_odd(x, w, g, b, tm=512):
    n = x.shape[0]
    tm = min(tm, n)
    row = lambda wd: pl.BlockSpec((tm, wd), lambda i: (i, 0))
    return pl.pallas_call(
        _proj_odd_kernel,
        grid=(n // tm,),
        in_specs=[row(D_MODEL), _const_spec(w.shape), _const_spec(g.shape), _const_spec(b.shape)],
        out_specs=[row(512), row(512), row(512)],
        out_shape=[jax.ShapeDtypeStruct((n, 512), F32)] * 3,
        compiler_params=_params(1),
        name="proj_odd",
    )(x, w, g, b)


def _pool_sgu_kernel(prev_ref, halo_ref, xc_ref, u_ref, vn_ref, wp_ref, sc_ref, ws_ref, bs_ref,
                     o_ref, ext_ref, *, start):
    t = pl.program_id(1)
    hal = prev_ref.shape[0]
    ext_ref[0:hal, :] = jnp.where(t == 0, prev_ref[...], halo_ref[...])
    ext_ref[hal:hal + CHUNK, :] = xc_ref[...]
    pos = start + t * CHUNK + lax.broadcasted_iota(jnp.int32, (CHUNK, 1), 0)
    for g, w in enumerate(POOL_WINDOWS):
        gs = slice(g * C_GROUP, (g + 1) * C_GROUP)
        acc = ext_ref[hal:hal + CHUNK, gs]
        for s in range(1, w):
            acc = acc + ext_ref[hal - s:hal - s + CHUNK, gs]
        cnt = jnp.minimum(w, pos + 1).astype(F32)
        pooled = acc / cnt - xc_ref[:, gs]
        c = _nn(pooled.astype(BF), wp_ref[g]) * sc_ref[:, gs]
        o_ref[:, gs] = c.astype(o_ref.dtype)
    r = lax.broadcasted_iota(jnp.int32, (CHUNK, CHUNK), 0)
    cc = lax.broadcasted_iota(jnp.int32, (CHUNK, CHUNK), 1)
    for g in range(D_GROUPS):
        gs = slice(g * 128, (g + 1) * 128)
        ws = jnp.where(r >= cc, ws_ref[g], 0.0).astype(BF)
        s = _nn(ws, vn_ref[:, gs].astype(BF)) + bs_ref[:, g:g + 1]
        o_ref[:, MIX_C + g * 128:MIX_C + (g + 1) * 128] = (u_ref[:, gs] * s).astype(o_ref.dtype)


def _pool_sgu_prompt(prev16, xc, u, vn, wp, scale, ws, bs_t, batch, seq, start):
    nt = seq // CHUNK
    hal = prev16.shape[1]
    per = CHUNK // hal
    row = pl.BlockSpec((CHUNK, 512), lambda b, t: (b * nt + t, 0))
    return pl.pallas_call(
        functools.partial(_pool_sgu_kernel, start=start),
        grid=(batch, nt),
        in_specs=[pl.BlockSpec((None, hal, 512), lambda b, t: (b, 0, 0)),
                  pl.BlockSpec((hal, 512), lambda b, t: (jnp.maximum((b * nt + t) * per - 1, 0), 0)),
                  row, row, row, _const_spec(wp.shape), _const_spec(scale.shape),
                  _const_spec(ws.shape), _const_spec(bs_t.shape)],
        out_specs=pl.BlockSpec((CHUNK, 1024), lambda b, t: (b * nt + t, 0)),
        out_shape=jax.ShapeDtypeStruct((batch * seq, 1024), BF),
        scratch_shapes=[pltpu.VMEM((hal + CHUNK, 512), F32)],
        compiler_params=_params(2),
        name="pool_sgu_prompt",
    )(prev16, xc, xc, u, vn, wp, scale, ws, bs_t)


def _pool_sgu_sample_kernel(ws_ref, bs_ref, prev_ref, xc_ref, u_ref, vn_ref, wp_ref, sc_ref, o_ref,
                            *, start):
    nprev = prev_ref.shape[0]
    ntok = xc_ref.shape[0]
    for t in range(ntok):
        for g, w in enumerate(POOL_WINDOWS):
            gs = slice(g * C_GROUP, (g + 1) * C_GROUP)
            acc = None
            for s in range(w):
                j = nprev + t - s
                slab = prev_ref[j, :, gs] if j < nprev else xc_ref[j - nprev, :, gs]
                acc = slab if acc is None else acc + slab
            cnt = float(min(w, start + t + 1))
            pooled = acc / cnt - xc_ref[t, :, gs]
            o_ref[t, :, gs] = _nn(pooled.astype(BF), wp_ref[g]) * sc_ref[:, gs]
        for g in range(D_GROUPS):
            gs = slice(g * 128, (g + 1) * 128)
            s = None
            for j in range(t + 1):
                term = ws_ref[(g * ntok + t) * ntok + j] * vn_ref[j, :, gs]
                s = term if s is None else s + term
            s = s + bs_ref[g * ntok + t]
            o_ref[t, :, MIX_C + g * 128:MIX_C + (g + 1) * 128] = u_ref[t, :, gs] * s


def _pool_sgu_sample(ws_small, bs_small, prev_t, xc_t, u_t, vn_t, wp, scale, start):
    ntok, nreq, _ = xc_t.shape
    smem = pl.BlockSpec(memory_space=pltpu.SMEM)
    return pl.pallas_call(
        functools.partial(_pool_sgu_sample_kernel, start=start),
        grid=(1,),
        in_specs=[smem, smem, _const_spec(prev_t.shape), _const_spec(xc_t.shape), _const_spec(u_t.shape),
                  _const_spec(vn_t.shape), _const_spec(wp.shape), _const_spec(scale.shape)],
        out_specs=_const_spec((ntok, nreq, 1024)),
        out_shape=jax.ShapeDtypeStruct((ntok, nreq, 1024), F32),
        compiler_params=_params(1),
        name="pool_sgu_sample",
    )(ws_small, bs_small, prev_t, xc_t, u_t, vn_t, wp, scale)


def _even_weights(w_in):
    q_a, k_a, v_a, q_b, k_b, v_b, q_i, k_i, w_i = jnp.split(
        w_in, [512, 1024, 1536, 2048, 2176, 2304, 2816, 2880], axis=1)
    wa = jnp.concatenate([k_a.reshape(D_MODEL, H_A, 2 * DA), v_a.reshape(D_MODEL, H_A, DV_A)],
                         -1).reshape(D_MODEL, H_A * (2 * DA + DV_A))
    wb = jnp.concatenate([k_b, v_b, k_i], 1)
    wq = jnp.concatenate([q_a, q_b, q_i], 1)
    ww = jnp.concatenate([w_i, jnp.zeros((D_MODEL, 128 - HI), w_in.dtype)], 1)
    qscale = jnp.concatenate([jnp.full((512,), DA ** -0.5, F32), jnp.full((512,), DB ** -0.5, F32),
                              jnp.full((512,), DI ** -0.5, F32)]).reshape(1, 1536)
    wwt = jnp.concatenate([w_i.T, jnp.zeros((16 - HI, D_MODEL), w_in.dtype)], 0)
    natural = (wa.astype(BF), wb.astype(BF), wq.astype(BF), ww.astype(BF), qscale)
    feature_major = (wq.T.astype(BF), v_a.T.astype(BF), v_b.T.astype(BF), wwt.astype(BF))
    return natural, feature_major


def _a_pages(x):
    n, page, _ = x.shape
    return x.reshape(n, page, H_A, 2, 128).transpose(0, 1, 3, 2, 4).reshape(n, page * 2 * H_A, 128)


def _pad_rows(x, rows):
    return jnp.pad(x, ((0, 0), (0, rows - x.shape[1]), (0, 0)))


def kernel(x_prompt, x_sample, cache_a, cache_b, state_pool, page_table, w_in_e, lam_e, subln_g, w_out_e,
           w_in_o, w_pool, pool_scale, sgu_g, sgu_b, w_s, b_s, w_out_o, w_mlp1, w_mlp2, ln_g, ln_b):
    batch, seq, _ = x_prompt.shape
    nreq, ntok, _ = x_sample.shape
    npg = page_table.shape[1]
    page = cache_a.shape[2]
    past = npg * page
    xp = x_prompt.reshape(batch * seq, D_MODEL)
    xs = x_sample.reshape(nreq * ntok, D_MODEL)
    outs = {k: [] for k in ("a_p", "b_p", "pool_p", "a_s", "b_s", "pool_s", "v_s")}
    rpad = 8

    for l in range(DEPTH):
        i = l // 2
        row2 = lambda v: v.reshape(1, -1)
        if l % 2 == 0:
            lam_init = 0.8 - 0.6 * math.exp(-0.3 * l)
            (wa, wb, wq, ww, qscale), (wqt, wvat, wvbt, wwt) = _even_weights(w_in_e[i])
            g = row2(subln_g[i])
            w_out = w_out_e[i].astype(BF)
            na, nb, abf, bbf, qt, vat, vbt, wit = _proj_even_t(xp, wa, wb, wqt, wvat, wvbt, wwt)
            o_a = _diff_attn_prompt(lam_e[i], subln_g[i].reshape(-1, 1), qt, abf, vat, batch, seq, lam_init)
            o_b = _sparse_attn_prompt(qt, wit, bbf, vbt, batch, seq, min(TOPK_MAX, seq // 4))
            mix_p = [o_a, o_b]
            outs["a_p"].append(na.reshape(batch, seq, 2, H_A, 128).transpose(0, 1, 3, 2, 4)
                               .reshape(batch, seq, H_A, 2 * DA + DV_A))
            outs["b_p"].append(nb.reshape(batch, seq, 2 * DB + DI))
            nas, nbs, _, _, qs, wis = _proj_even(xs, wa, wb, wq, ww, qscale)
            o_s = _sample_even(
                page_table, lam_e[i], g,
                _pad_rows(qs.astype(F32).reshape(nreq, ntok, 1536), rpad),
                _pad_rows(wis.reshape(nreq, ntok, 128), rpad),
                _a_pages(_pad_rows(nas.reshape(nreq, ntok, 1024), page)),
                _pad_rows(nbs.reshape(nreq, ntok, 320), page).transpose(0, 2, 1),
                _a_pages(cache_a[i].reshape(-1, page, 1024)), cache_b[i].transpose(0, 2, 1),
                min(TOPK_MAX, (past + ntok) // 4), lam_init)
            mix_s = [o_s[:, :ntok].reshape(nreq * ntok, 1024)]
            outs["a_s"].append(nas.reshape(nreq, ntok, H_A, 2 * DA + DV_A))
            outs["b_s"].append(nbs.reshape(nreq, ntok, 2 * DB + DI))
        else:
            w_in = w_in_o[i].astype(BF)
            w_out = w_out_o[i].astype(BF)
            wp = w_pool[i].astype(BF)
            scale = row2(pool_scale[i])
            sg, sb = row2(sgu_g[i]), row2(sgu_b[i])
            xc, u, vn = _proj_odd(xp, w_in, sg, sb)
            prev16 = jnp.zeros((batch, POOL_BUF + 1, MIX_C), F32)
            mix_p = [_pool_sgu_prompt(prev16, xc, u, vn, wp, scale, w_s[i], b_s[i].T, batch, seq, 0)]
            outs["pool_p"].append(xc.reshape(batch, seq, MIX_C)[:, seq - POOL_BUF:])
            xcs, us, vns = _proj_odd(xs, w_in, sg, sb)
            tmaj = lambda v: v.reshape(nreq, ntok, -1).transpose(1, 0, 2)
            m_t = _pool_sgu_sample(
                w_s[i][:, :ntok, :ntok].reshape(-1), b_s[i][:, :ntok].reshape(-1),
                state_pool[i].transpose(1, 0, 2), tmaj(xcs), tmaj(us), tmaj(vns), wp, scale, past)
            mix_s = [m_t.transpose(1, 0, 2).reshape(nreq * ntok, 1024)]
            ext = jnp.concatenate([state_pool[i], xcs.reshape(nreq, ntok, MIX_C)], 1)
            outs["pool_s"].append(ext[:, ext.shape[1] - POOL_BUF:])
            outs["v_s"].append(vns.reshape(nreq, ntok, MIX_D))
        g0, b0, g1, b1 = row2(ln_g[l, 0]), row2(ln_b[l, 0]), row2(ln_g[l, 1]), row2(ln_b[l, 1])
        w1, w2 = w_mlp1[l].astype(BF), w_mlp2[l].astype(BF)
        xp = _outproj_res_ln(mix_p, w_out, xp, g0, b0)
        xp = _mlp_res_ln(xp, w1, w2, g1, b1)
        xs = _outproj_res_ln(mix_s, w_out, xs, g0, b0)
        xs = _mlp_res_ln(xs, w1, w2, g1, b1)

    st = lambda k: jnp.stack(outs[k])
    return (xp.reshape(batch, seq, D_MODEL), xs.reshape(nreq, ntok, D_MODEL), st("a_p"), st("b_p"),
            st("pool_p"), st("a_s"), st("b_s"), st("pool_s"), st("v_s"))
```

```python
import functools
import math

import jax
import jax.numpy as jnp
from jax import lax
from jax.experimental import pallas as pl
from jax.experimental.pallas import tpu as pltpu

D_MODEL = 1024
H_A = 4
DA = 64
DV_A = 128
H_B = 4
DB = 128
HI = 8
DI = 64
TOPK_MAX = 256
MIX_C = 512
MIX_D = 512
POOL_WINDOWS = (2, 4, 8, 16)
C_GROUP = 128
POOL_BUF = 15
CHUNK = 128
D_GROUPS = 4
D_FF = 4096
DEPTH = 2
ALPHA = (2 * DEPTH) ** 0.25
EPS = 1e-5

BF = jnp.bfloat16
F32 = jnp.float32
NEG = -1e30
INT_MIN = -(2 ** 31)
FLT_LOWEST = -3.4028234663852886e38
MANY = 1e9
LOG2E = math.log2(math.e)
V_ROWS = 144
VMEM_LIMIT_BYTES = 56 * 1024 * 1024


def _params(n_axes):
    return pltpu.CompilerParams(dimension_semantics=("arbitrary",) * n_axes,
                                vmem_limit_bytes=VMEM_LIMIT_BYTES)


def _nn(a, b):
    return jnp.dot(a, b, preferred_element_type=F32)


def _nt(a, b):
    return lax.dot_general(a, b, (((1,), (1,)), ((), ())), preferred_element_type=F32)


def _ln(z, g, b):
    mu = jnp.mean(z, -1, keepdims=True)
    d = z - mu
    var = jnp.mean(d * d, -1, keepdims=True)
    return d * lax.rsqrt(var + EPS) * g + b


def _const_spec(shape):
    nd = len(shape)
    return pl.BlockSpec(shape, lambda *_: (0,) * nd)


def _key_to_float(k):
    return lax.bitcast_convert_type(jnp.where(k < 0, k ^ jnp.int32(0x7FFFFFFF), k), F32)


def _kth_largest(count, shape, ksel, bits=1):
    n_finite, = count([jnp.full(shape, -jnp.inf, F32)], True)
    c0, = count([jnp.zeros(shape, F32)], False)
    nonneg = c0 >= ksel
    k0 = jnp.where(nonneg, jnp.int32(0), jnp.int32(INT_MIN))
    n0 = jnp.where(nonneg, c0, MANY)

    def refine(carry, shift, nbits):
        k, n_ge = carry
        cands = [k | lax.shift_left(jnp.int32(d), shift) for d in range(1, 2 ** nbits)]
        for cand, n_cand in zip(cands, count([_key_to_float(c) for c in cands], False)):
            take = n_cand >= ksel
            k, n_ge = jnp.where(take, cand, k), jnp.where(take, n_cand, n_ge)
        return k, n_ge

    npass, rest = divmod(31, bits)
    carry = lax.fori_loop(0, npass, lambda it, cr: refine(cr, jnp.int32(31 - bits) - bits * it, bits),
                          (k0, n0))
    if rest:
        carry = refine(carry, jnp.int32(0), rest)
    k, n_ge = carry
    short = n_finite < ksel
    t = jnp.where(short, FLT_LOWEST, _key_to_float(k))
    n_above, = count([t], True)
    need = jnp.where(short, MANY, ksel - n_above)
    tied = jnp.where(short, 0.0, jnp.where(n_ge > ksel, 1.0, 0.0))
    return t, need, tied


def _proj_even_kernel(x_ref, wa_ref, wb_ref, wq_ref, ww_ref, qs_ref,
                      na_ref, nb_ref, abf_ref, bbf_ref, q_ref, wi_ref):
    x = x_ref[...].astype(BF)
    a = _nn(x, wa_ref[...])
    na_ref[...] = a
    abf_ref[...] = a.astype(BF)
    b = _nn(x, wb_ref[...])
    nb_ref[...] = b
    bbf_ref[...] = b.astype(BF)
    q_ref[...] = (_nn(x, wq_ref[...]) * qs_ref[...]).astype(BF)
    wi_ref[...] = _nn(x, ww_ref[...]) * (HI ** -0.5)


def _proj_even(x, wa, wb, wq, ww, qscale, tm=256):
    n = x.shape[0]
    tm = min(tm, n)
    row = lambda w: pl.BlockSpec((tm, w), lambda i: (i, 0))
    return pl.pallas_call(
        _proj_even_kernel,
        grid=(n // tm,),
        in_specs=[row(D_MODEL), _const_spec(wa.shape), _const_spec(wb.shape),
                  _const_spec(wq.shape), _const_spec(ww.shape), _const_spec(qscale.shape)],
        out_specs=[row(1024), row(320), row(1024), row(320), row(1536), row(128)],
        out_shape=[jax.ShapeDtypeStruct((n, 1024), F32), jax.ShapeDtypeStruct((n, 320), F32),
                   jax.ShapeDtypeStruct((n, 1024), BF), jax.ShapeDtypeStruct((n, 320), BF),
                   jax.ShapeDtypeStruct((n, 1536), BF), jax.ShapeDtypeStruct((n, 128), F32)],
        compiler_params=_params(1),
        name="proj_even",
    )(x, wa, wb, wq, ww, qscale)


def _proj_even_t_kernel(x_ref, wa_ref, wb_ref, wqt_ref, wvat_ref, wvbt_ref, wwt_ref,
                        na_ref, nb_ref, abf_ref, bbf_ref, qt_ref, vat_ref, vbt_ref, wit_ref):
    x = x_ref[...]
    xb = x.astype(BF)
    xt = x.T.astype(BF)
    a = _nn(xb, wa_ref[...])
    tm = a.shape[0]
    for h in range(H_A):
        for part in range(2):
            na_ref[pl.ds(H_A * part + h, tm, stride=2 * H_A), :] = a[:, h * 256 + part * 128:h * 256 + (part + 1) * 128]
    abf_ref[...] = a.astype(BF)
    b = _nn(xb, wb_ref[...])
    nb_ref[...] = b
    bbf_ref[...] = b.astype(BF)
    qt = _nn(wqt_ref[...], xt)
    qt_ref[0:512, :] = (qt[0:512] * (DA ** -0.5 * LOG2E)).astype(BF)
    qt_ref[512:1024, :] = (qt[512:1024] * (DB ** -0.5 * LOG2E)).astype(BF)
    qt_ref[1024:1536, :] = (qt[1024:1536] * (DI ** -0.5)).astype(BF)
    pad = V_ROWS - 128
    ones_rows = jnp.where(lax.broadcasted_iota(jnp.int32, (pad, tm), 0) == 0, 1.0, 0.0).astype(BF)
    va = _nn(wvat_ref[...], xt).astype(BF)
    for h in range(H_A):
        vat_ref[h * V_ROWS:h * V_ROWS + 128, :] = va[h * 128:(h + 1) * 128]
        vat_ref[h * V_ROWS + 128:(h + 1) * V_ROWS, :] = ones_rows
    vbt_ref[0:128, :] = _nn(wvbt_ref[...], xt).astype(BF)
    vbt_ref[128:V_ROWS, :] = ones_rows
    wit_ref[...] = _nn(wwt_ref[...], xt) * (HI ** -0.5)


def _proj_even_t(x, wa, wb, wqt, wvat, wvbt, wwt, tm=512):
    n = x.shape[0]
    row = lambda w: pl.BlockSpec((tm, w), lambda i: (i, 0))
    col = lambda h: pl.BlockSpec((h, tm), lambda i: (0, i))
    return pl.pallas_call(
        _proj_even_t_kernel,
        grid=(n // tm,),
        in_specs=[row(D_MODEL)] + [_const_spec(w.shape) for w in (wa, wb, wqt, wvat, wvbt, wwt)],
        out_specs=[pl.BlockSpec((2 * H_A * tm, 128), lambda i: (i, 0)), row(320), row(1024), row(320),
                   col(1536), col(H_A * V_ROWS), col(V_ROWS), col(16)],
        out_shape=[jax.ShapeDtypeStruct((2 * H_A * n, 128), F32), jax.ShapeDtypeStruct((n, 320), F32),
                   jax.ShapeDtypeStruct((n, 1024), BF), jax.ShapeDtypeStruct((n, 320), BF),
                   jax.ShapeDtypeStruct((1536, n), BF), jax.ShapeDtypeStruct((H_A * V_ROWS, n), BF),
                   jax.ShapeDtypeStruct((V_ROWS, n), BF), jax.ShapeDtypeStruct((16, n), F32)],
        compiler_params=_params(1),
        name="proj_even_t",
    )(x, wa, wb, wqt, wvat, wvbt, wwt)


def _lambda(lam_ref, lam_init):
    lp = lam_ref[...]
    return (jnp.exp(jnp.sum(lp[0:1] * lp[1:2], axis=-1, keepdims=True))
            - jnp.exp(jnp.sum(lp[2:3] * lp[3:4], axis=-1, keepdims=True)) + lam_init)


def _split_q12(q):
    qf = q.astype(F32)
    lane = lax.broadcasted_iota(jnp.int32, qf.shape, 1)
    return jnp.concatenate([jnp.where(lane < DA, qf, 0.0), jnp.where(lane >= DA, qf, 0.0)],
                           axis=0).astype(BF)


def _subln(o, lam, g, lam_init):
    r = o.shape[0] // 2
    d = o[:r] - lam * o[r:]
    ms = jnp.mean(d * d, -1, keepdims=True)
    return d * lax.rsqrt(ms + EPS) * g * (1.0 - lam_init)


def _kth_largest_rows(sc_ref, nch, tk, ksel):
    rows = sc_ref.shape[0]

    def count(cands, strict):
        accs = [jnp.zeros((rows, tk), F32) for _ in cands]
        for c in range(nch):
            x = sc_ref[:, c * tk:(c + 1) * tk]
            accs = [a + jnp.where((x > cand) if strict else (x >= cand), 1.0, 0.0)
                    for a, cand in zip(accs, cands)]
        return [jnp.sum(a, -1, keepdims=True) for a in accs]

    return _kth_largest(count, (rows, 1), ksel, bits=2)


def _kth_largest_cols(sc_ref, nch, tk, ksel):
    cols = sc_ref.shape[1]

    def count(cands, strict):
        cand, = cands

        def body(c, acc):
            x = sc_ref[pl.ds(pl.multiple_of(c * tk, tk), tk), :]
            hit = (x > cand) if strict else (x >= cand)
            return acc + jnp.sum(jnp.where(hit, 1.0, 0.0).reshape(tk // 32, 32, cols), axis=0)
        acc = lax.fori_loop(0, nch, body, jnp.zeros((32, cols), F32))
        return [jnp.sum(acc, 0, keepdims=True)]

    return _kth_largest(count, (1, cols), ksel)


def _selected(x, t, need, eq_before, tri):
    eq = jnp.where(x == t, 1.0, 0.0)
    rank = eq_before + _nn(eq.astype(BF), tri)
    tie_taken = jnp.where(rank <= need, eq, 0.0)
    return jnp.where(x > t, 1.0, tie_taken), jnp.sum(eq, -1, keepdims=True)


def _tri(n, lower):
    r = lax.broadcasted_iota(jnp.int32, (n, n), 0)
    c = lax.broadcasted_iota(jnp.int32, (n, n), 1)
    return jnp.where((r >= c) if lower else (r <= c), 1.0, 0.0).astype(BF)


def _flash_key_major(npair, nlast, nchain, tq, qk, mask, v_t, s_ref, p_ref, al_ref, acc_ref):
    def scores(c, slot):
        for j in range(nchain):
            s_ref[slot, j] = qk(c, j)

    def softmax(c, slot, ms, last):
        ms_new = []
        for j in range(nchain):
            s = mask(c, j, s_ref[slot, j], last)
            m_new = jnp.maximum(ms[j], jnp.max(s, 0, keepdims=True))
            p_ref[slot, j] = jnp.exp2(s - m_new).astype(BF)
            al_ref[slot, j] = jnp.exp2(ms[j] - m_new)
            ms_new.append(m_new)
        return tuple(ms_new)

    def fold(c, slot):
        for j in range(nchain):
            acc_ref[j] = al_ref[slot, j] * acc_ref[j] + _nn(v_t(c, j), p_ref[slot, j])

    acc_ref[...] = jnp.zeros(acc_ref.shape, F32)
    p_ref[1] = jnp.zeros(p_ref.shape[1:], BF)
    al_ref[1] = jnp.ones(al_ref.shape[1:], F32)
    scores(0, 0)

    def pair(c, carry, last, final):
        fold(jnp.maximum(c - 1, 0), 1)
        carry = softmax(c, 0, carry, last)
        scores(c + 1, 1)
        fold(c, 0)
        carry = softmax(c + 1, 1, carry, last)
        if not final:
            scores(c + 2, 0)
        return carry

    n_plain = npair - nlast
    carry = lax.fori_loop(0, n_plain, lambda c, cr: pair(2 * c, cr, False, False),
                          tuple(jnp.full((1, tq), NEG, F32) for _ in range(nchain)))
    for d in range(nlast):
        carry = pair(2 * (n_plain + d), carry, True, d == nlast - 1)
    fold(2 * npair - 1, 1)


def _diff_kernel(lam_ref, g_ref, qt_ref, a_ref, vt_ref, o_ref, qh_ref, s_ref, p_ref, al_ref, acc_ref,
                 *, tq, tk, lam_init):
    i = pl.program_id(1)
    lam = _lambda(lam_ref, lam_init)
    sub = lax.broadcasted_iota(jnp.int32, (128, tq), 0)
    key = lax.broadcasted_iota(jnp.int32, (tk, 2 * tq), 0)
    qry = i * tq + lax.broadcasted_iota(jnp.int32, (tk, 2 * tq), 1) % tq
    for h in range(H_A):
        qt = qt_ref[h * 128:(h + 1) * 128, :].astype(F32)
        qh_ref[h, :, 0:tq] = jnp.where(sub < DA, qt, 0.0).astype(BF)
        qh_ref[h, :, tq:2 * tq] = jnp.where(sub >= DA, qt, 0.0).astype(BF)

    def qk(c, h):
        return _nn(a_ref[pl.ds(pl.multiple_of(c * tk, tk), tk), h * 256:h * 256 + 128], qh_ref[h])

    def mask(c, h, s, last):
        return jnp.where(c * tk + key <= qry, s, NEG) if last else s

    def v_t(c, h):
        return vt_ref[h * V_ROWS:(h + 1) * V_ROWS, pl.ds(pl.multiple_of(c * tk, tk), tk)]

    nlast = tq // (2 * tk)
    _flash_key_major((i + 1) * nlast, nlast, H_A, 2 * tq, qk, mask, v_t, s_ref, p_ref, al_ref, acc_ref)
    for h in range(H_A):
        o12 = acc_ref[h, 0:DV_A, :] * (1.0 / acc_ref[h, DV_A:DV_A + 1, :])
        d = o12[:, 0:tq] - lam * o12[:, tq:2 * tq]
        ms = jnp.mean(d * d, 0, keepdims=True)
        o = d * lax.rsqrt(ms + EPS) * g_ref[...] * (1.0 - lam_init)
        o_ref[:, h * 128:(h + 1) * 128] = o.T.astype(o_ref.dtype)


def _flash_scratch(nchain, feat, tk, tq):
    return [pltpu.VMEM((2, nchain, tk, tq), F32), pltpu.VMEM((2, nchain, tk, tq), BF),
            pltpu.VMEM((2, nchain, 1, tq), F32), pltpu.VMEM((nchain, feat, tq), F32)]


def _diff_attn_prompt(lam_e, g_col, qt, a_bf, vat, batch, seq, lam_init, tq=256, tk=128):
    nq = seq // tq
    return pl.pallas_call(
        functools.partial(_diff_kernel, tq=tq, tk=tk, lam_init=lam_init),
        grid=(batch, nq),
        in_specs=[_const_spec(lam_e.shape), _const_spec(g_col.shape),
                  pl.BlockSpec((512, tq), lambda b, i: (0, b * nq + i)),
                  pl.BlockSpec((seq, 1024), lambda b, i: (b, 0)),
                  pl.BlockSpec((H_A * V_ROWS, seq), lambda b, i: (0, b))],
        out_specs=pl.BlockSpec((tq, 512), lambda b, i: (b * nq + i, 0)),
        out_shape=jax.ShapeDtypeStruct((batch * seq, 512), BF),
        scratch_shapes=[pltpu.VMEM((H_A, 128, 2 * tq), BF)] + _flash_scratch(H_A, V_ROWS, tk, 2 * tq),
        compiler_params=_params(2),
        name="diff_attn_prompt",
    )(lam_e, g_col, qt, a_bf, vat)


def _sparse_kernel(qbt_ref, qit_ref, wit_ref, b_ref, vbt_ref, o_ref, sc_ref, qiw_ref, qbw_ref,
                   s_ref, p_ref, al_ref, acc_ref, *, tq, tk, ksel):
    i = pl.program_id(1)
    nch = (i + 1) * (tq // tk)
    ta = tk // 2
    key = lax.broadcasted_iota(jnp.int32, (ta, tq), 0)
    qry = i * tq + lax.broadcasted_iota(jnp.int32, (ta, tq), 1)
    w = wit_ref[...]
    for h in range(HI):
        qiw_ref[:, h * tq:(h + 1) * tq] = qit_ref[h * DI:(h + 1) * DI, :]
    for h in range(H_B):
        qbw_ref[:, h * tq:(h + 1) * tq] = qbt_ref[h * DB:(h + 1) * DB, :]
    heads = lambda x: jnp.concatenate([x] * H_B, axis=1)

    def score_chunk(c, _):
        for u in range(2):
            r0 = pl.multiple_of(c * tk + u * ta, ta)
            d = jnp.maximum(_nn(b_ref[pl.ds(r0, ta), 2 * DB:2 * DB + DI], qiw_ref[...]), 0.0)
            acc = w[0:1, :] * d[:, 0:tq]
            for h in range(1, HI):
                acc = acc + w[h:h + 1, :] * d[:, h * tq:(h + 1) * tq]
            sc_ref[pl.ds(r0, ta), :] = jnp.where(r0 + key <= qry, acc, -jnp.inf)
        return 0

    lax.fori_loop(0, nch, score_chunk, 0)
    t, need, tied = _kth_largest_cols(sc_ref, nch, tk, ksel)

    def qk(c, _):
        return _nn(b_ref[pl.ds(pl.multiple_of(c * ta, ta), ta), 0:DB], qbw_ref[...])

    def v_t(c, _):
        return vbt_ref[:, pl.ds(pl.multiple_of(c * ta, ta), ta)]

    def to_bias(exact_ties):
        def body(c, eq_before):
            for u in range(2):
                rows = pl.ds(pl.multiple_of(c * tk + u * ta, ta), ta)
                x = sc_ref[rows, :]
                if exact_ties:
                    eq = jnp.where(x == t, 1.0, 0.0)
                    rank = eq_before + _nn(_tri(ta, True), eq.astype(BF))
                    sc_ref[rows, :] = jnp.where(
                        x > t, 0.0, jnp.where(x == t, jnp.where(rank <= need, 0.0, NEG), NEG))
                    eq_before = eq_before + jnp.sum(eq, 0, keepdims=True)
                else:
                    sc_ref[rows, :] = jnp.where(x >= t, 0.0, NEG)
            return eq_before
        return lax.fori_loop(0, nch, body, jnp.zeros((1, tq), F32))

    lax.cond(jnp.max(tied) > 0.0, lambda: to_bias(True), lambda: to_bias(False))

    def add_bias(c, _, s, last):
        return s + heads(sc_ref[pl.ds(pl.multiple_of(c * ta, ta), ta), :])

    _flash_key_major(nch, 1, 1, H_B * tq, qk, add_bias, v_t, s_ref, p_ref, al_ref, acc_ref)
    o = acc_ref[0, 0:DB, :] * (1.0 / acc_ref[0, DB:DB + 1, :])
    for h in range(H_B):
        o_ref[:, h * DB:(h + 1) * DB] = o[:, h * tq:(h + 1) * tq].T.astype(o_ref.dtype)


def _sparse_attn_prompt(qt, wit, b_bf, vbt, batch, seq, ksel, tq=256, tk=256):
    nq = seq // tq
    return pl.pallas_call(
        functools.partial(_sparse_kernel, tq=tq, tk=tk, ksel=ksel),
        grid=(batch, nq),
        in_specs=[pl.BlockSpec((512, tq), lambda b, i: (1, b * nq + i)),
                  pl.BlockSpec((512, tq), lambda b, i: (2, b * nq + i)),
                  pl.BlockSpec((16, tq), lambda b, i: (0, b * nq + i)),
                  pl.BlockSpec((seq, 320), lambda b, i: (b, 0)),
                  pl.BlockSpec((V_ROWS, seq), lambda b, i: (0, b))],
        out_specs=pl.BlockSpec((tq, 512), lambda b, i: (b * nq + i, 0)),
        out_shape=jax.ShapeDtypeStruct((batch * seq, 512), BF),
        scratch_shapes=[pltpu.VMEM((seq, tq), F32), pltpu.VMEM((DI, HI * tq), BF),
                        pltpu.VMEM((DB, H_B * tq), BF)] + _flash_scratch(1, V_ROWS, tk // 2, H_B * tq),
        compiler_params=_params(2),
        name="sparse_attn_prompt",
    )(qt, qt, wit, b_bf, vbt)


def _sample_even_kernel(pt_ref, lam_ref, g_ref, q_ref, wi_ref, *rest, npg, page, ksel, lam_init):
    del pt_ref
    nblk = npg + 1
    a_pages = rest[:nblk]
    b_pages = rest[nblk:2 * nblk]
    o_ref = rest[2 * nblk]
    sc_ref = rest[2 * nblk + 1]
    r = q_ref.shape[0]
    lam = _lambda(lam_ref, lam_init)
    q = q_ref[...]

    def a_rows(p, j):
        n_pos = a_pages[p].shape[0] // (2 * H_A)
        x = a_pages[p][pl.ds(j, n_pos, stride=2 * H_A), :]
        if n_pos < page:
            x = jnp.concatenate([x, jnp.zeros((page - n_pos, 128), F32)], axis=0)
        return x.astype(BF)

    def new_ok(rows):
        tok = lax.broadcasted_iota(jnp.int32, (rows, page), 0) % r
        return lax.broadcasted_iota(jnp.int32, (rows, page), 1) <= tok

    ok2 = new_ok(2 * r)
    for h in range(H_A):
        qq = _split_q12(q[:, h * 128:(h + 1) * 128])
        ss = [_nt(qq, a_rows(p, h)) for p in range(nblk)]
        ss[npg] = jnp.where(ok2, ss[npg], NEG)
        m = functools.reduce(jnp.maximum, ss)
        m = jnp.max(m, -1, keepdims=True)
        ps = [jnp.exp(s - m) for s in ss]
        l = jnp.sum(functools.reduce(lambda x, y: x + y, ps), -1, keepdims=True)
        acc = functools.reduce(lambda x, y: x + y, [
            _nn(ps[p].astype(BF), a_rows(p, H_A + h))
            for p in range(nblk)])
        o_ref[:, h * 128:(h + 1) * 128] = _subln(acc / l, lam, g_ref[...], lam_init)

    qi = jnp.concatenate([q[:, 1024 + h * DI:1024 + (h + 1) * DI] for h in range(HI)], axis=0).astype(BF)
    w = wi_ref[...]
    wcol = jnp.concatenate([w[:, h:h + 1] for h in range(HI)], axis=0)
    ok1 = new_ok(r)
    for p in range(nblk):
        d = jnp.maximum(_nn(qi, b_pages[p][2 * DB:2 * DB + DI, :].astype(BF)), 0.0) * wcol
        sc = d[0:r]
        for h in range(1, HI):
            sc = sc + d[h * r:(h + 1) * r]
        if p == npg:
            sc = jnp.where(ok1, sc, -jnp.inf)
        sc_ref[:, p * page:(p + 1) * page] = sc
    t, need, _ = _kth_largest_rows(sc_ref, nblk, page, ksel)

    tri = _tri(page, False)
    q4 = jnp.concatenate([q[:, 512 + h * DB:512 + (h + 1) * DB] for h in range(H_B)], axis=0).astype(BF)
    eq_before = jnp.zeros((r, 1), F32)
    ss, sels = [], []
    for p in range(nblk):
        sel, n_eq = _selected(sc_ref[:, p * page:(p + 1) * page], t, need, eq_before, tri)
        eq_before = eq_before + n_eq
        if p == npg:
            sel = jnp.where(ok1, sel, 0.0)
        sels.append(jnp.concatenate([sel] * H_B, axis=0) > 0.5)
        ss.append(_nn(q4, b_pages[p][0:DB, :].astype(BF)))
    m = functools.reduce(jnp.maximum, [jnp.where(sl, s, NEG) for sl, s in zip(sels, ss)])
    m = jnp.max(m, -1, keepdims=True)
    ps = [jnp.where(sl, jnp.exp(s - m), 0.0) for sl, s in zip(sels, ss)]
    l = jnp.sum(functools.reduce(lambda x, y: x + y, ps), -1, keepdims=True)
    acc = functools.reduce(lambda x, y: x + y, [
        _nt(ps[p].astype(BF), b_pages[p][DB:2 * DB, :].astype(BF)) for p in range(nblk)])
    o = acc / l
    for h in range(H_B):
        o_ref[:, 512 + h * DB:512 + (h + 1) * DB] = o[h * r:(h + 1) * r]


def _sample_even(page_table, lam_e, g, qs, wis, anew_pg, bnew_pg, cache_a_pg, cache_b_pg, ksel, lam_init):
    nreq, npg = page_table.shape
    page = cache_b_pg.shape[2]
    r = qs.shape[1]
    req = lambda a: pl.BlockSpec((None,) + a.shape[1:], lambda i, pt: (i, 0, 0))

    def page_spec(a, p):
        return pl.BlockSpec((None,) + a.shape[1:], lambda i, pt, p=p: (pt[i, p], 0, 0))

    in_specs = [_const_spec(lam_e.shape), _const_spec(g.shape), req(qs), req(wis)]
    in_specs += [page_spec(cache_a_pg, p) for p in range(npg)] + [req(anew_pg)]
    in_specs += [page_spec(cache_b_pg, p) for p in range(npg)] + [req(bnew_pg)]
    grid_spec = pltpu.PrefetchScalarGridSpec(
        num_scalar_prefetch=1, grid=(nreq,), in_specs=in_specs,
        out_specs=pl.BlockSpec((None, r, 1024), lambda i, pt: (i, 0, 0)),
        scratch_shapes=[pltpu.VMEM((r, (npg + 1) * page), F32)])
    return pl.pallas_call(
        functools.partial(_sample_even_kernel, npg=npg, page=page, ksel=ksel, lam_init=lam_init),
        grid_spec=grid_spec,
        out_shape=jax.ShapeDtypeStruct((nreq, r, 1024), F32),
        compiler_params=_params(1),
        name="sample_even",
    )(page_table, lam_e, g, qs, wis, *([cache_a_pg] * npg), anew_pg, *([cache_b_pg] * npg), bnew_pg)


def _outproj_kernel(*refs, n_lhs):
    lhs = refs[:n_lhs]
    w_ref, x_ref, g_ref, b_ref, o_ref = refs[n_lhs:]
    y = None
    k0 = 0
    for a_ref in lhs:
        kw = a_ref.shape[1]
        part = _nn(a_ref[...].astype(BF), w_ref[k0:k0 + kw, :])
        y = part if y is None else y + part
        k0 += kw
    o_ref[...] = _ln(ALPHA * x_ref[...] + y, g_ref[...], b_ref[...])


def _outproj_res_ln(lhs, w, x, g, b, tm=512):
    n = x.shape[0]
    tm = min(tm, n)
    row = lambda wd: pl.BlockSpec((tm, wd), lambda i: (i, 0))
    return pl.pallas_call(
        functools.partial(_outproj_kernel, n_lhs=len(lhs)),
        grid=(n // tm,),
        in_specs=[row(a.shape[1]) for a in lhs] + [_const_spec(w.shape), row(D_MODEL),
                                                   _const_spec(g.shape), _const_spec(b.shape)],
        out_specs=row(D_MODEL),
        out_shape=jax.ShapeDtypeStruct((n, D_MODEL), F32),
        compiler_params=_params(1),
        name="outproj_res_ln",
    )(*lhs, w, x, g, b)


def _mlp_kernel(x_ref, w1_ref, w2_ref, g_ref, b_ref, o_ref, acc_ref, *, ck):
    x = x_ref[...]
    xb = x.astype(BF)
    for c in range(D_FF // ck):
        h = jnp.maximum(_nn(xb, w1_ref[:, c * ck:(c + 1) * ck]), 0.0)
        part = _nn((h * h).astype(BF), w2_ref[c * ck:(c + 1) * ck, :])
        if c == 0:
            acc_ref[...] = part
        else:
            acc_ref[...] += part
    o_ref[...] = _ln(ALPHA * x + acc_ref[...], g_ref[...], b_ref[...])


def _mlp_res_ln(x, w1, w2, g, b, tm=512, ck=512):
    n = x.shape[0]
    tm = min(tm, n)
    row = pl.BlockSpec((tm, D_MODEL), lambda i: (i, 0))
    resident = lambda s: pl.BlockSpec(s, lambda i: (0, 0), pipeline_mode=pl.Buffered(1))
    return pl.pallas_call(
        functools.partial(_mlp_kernel, ck=ck),
        grid=(n // tm,),
        in_specs=[row, resident(w1.shape), resident(w2.shape), _const_spec(g.shape), _const_spec(b.shape)],
        out_specs=row,
        out_shape=jax.ShapeDtypeStruct((n, D_MODEL), F32),
        scratch_shapes=[pltpu.VMEM((tm, D_MODEL), F32)],
        compiler_params=_params(1),
        name="mlp_res_ln",
    )(x, w1, w2, g, b)


def _gelu(x):
    return 0.5 * x * (1.0 + jnp.tanh(math.sqrt(2.0 / math.pi) * (x + 0.044715 * (x * x * x))))


def _proj_odd_kernel(x_ref, w_ref, g_ref, b_ref, xc_ref, u_ref, vn_ref):
    h = _nn(x_ref[...].astype(BF), w_ref[...])
    xc_ref[...] = h[:, :MIX_C]
    u_ref[...] = _gelu(h[:, MIX_C:MIX_C + MIX_D])
    vn_ref[...] = _ln(_gelu(h[:, MIX_C + MIX_D:]), g_ref[...], b_ref[...])


def _proj_odd(x, w, g, b, tm=512):
    n = x.shape[0]
    tm = min(tm, n)
    row = lambda wd: pl.BlockSpec((tm, wd), lambda i: (i, 0))
    return pl.pallas_call(
        _proj_odd_kernel,
        grid=(n // tm,),
        in_specs=[row(D_MODEL), _const_spec(w.shape), _const_spec(g.shape), _const_spec(b.shape)],
        out_specs=[row(512), row(512), row(512)],
        out_shape=[jax.ShapeDtypeStruct((n, 512), F32)] * 3,
        compiler_params=_params(1),
        name="proj_odd",
    )(x, w, g, b)


def _pool_sgu_kernel(prev_ref, halo_ref, xc_ref, u_ref, vn_ref, wp_ref, sc_ref, ws_ref, bs_ref,
                     o_ref, ext_ref, *, start):
    t = pl.program_id(1)
    hal = prev_ref.shape[0]
    rows = xc_ref.shape[0]
    ext_ref[0:hal, :] = jnp.where(t == 0, prev_ref[...], halo_ref[...])
    ext_ref[hal:hal + rows, :] = xc_ref[...]
    pos = start + t * rows + lax.broadcasted_iota(jnp.int32, (rows, 1), 0)
    for g, w in enumerate(POOL_WINDOWS):
        gs = slice(g * C_GROUP, (g + 1) * C_GROUP)
        acc = ext_ref[hal:hal + rows, gs]
        for s in range(1, w):
            acc = acc + ext_ref[hal - s:hal - s + rows, gs]
        cnt = jnp.minimum(w, pos + 1).astype(F32)
        pooled = acc / cnt - xc_ref[:, gs]
        c = _nn(pooled.astype(BF), wp_ref[g]) * sc_ref[:, gs]
        o_ref[:, gs] = c.astype(o_ref.dtype)
    r = lax.broadcasted_iota(jnp.int32, (CHUNK, CHUNK), 0)
    cc = lax.broadcasted_iota(jnp.int32, (CHUNK, CHUNK), 1)
    for g in range(D_GROUPS):
        gs = slice(g * 128, (g + 1) * 128)
        ws = jnp.where(r >= cc, ws_ref[g], 0.0).astype(BF)
        for k in range(rows // CHUNK):
            ks = slice(k * CHUNK, (k + 1) * CHUNK)
            s = _nn(ws, vn_ref[ks, gs].astype(BF)) + bs_ref[:, g:g + 1]
            o_ref[ks, MIX_C + g * 128:MIX_C + (g + 1) * 128] = (u_ref[ks, gs] * s).astype(o_ref.dtype)


def _pool_sgu_prompt(prev16, xc, u, vn, wp, scale, ws, bs_t, batch, seq, start, rows=512):
    rows = min(rows, seq)
    nt = seq // rows
    hal = prev16.shape[1]
    per = rows // hal
    row = pl.BlockSpec((rows, 512), lambda b, t: (b * nt + t, 0))
    return pl.pallas_call(
        functools.partial(_pool_sgu_kernel, start=start),
        grid=(batch, nt),
        in_specs=[pl.BlockSpec((None, hal, 512), lambda b, t: (b, 0, 0)),
                  pl.BlockSpec((hal, 512), lambda b, t: (jnp.maximum((b * nt + t) * per - 1, 0), 0)),
                  row, row, row, _const_spec(wp.shape), _const_spec(scale.shape),
                  _const_spec(ws.shape), _const_spec(bs_t.shape)],
        out_specs=pl.BlockSpec((rows, 1024), lambda b, t: (b * nt + t, 0)),
        out_shape=jax.ShapeDtypeStruct((batch * seq, 1024), BF),
        scratch_shapes=[pltpu.VMEM((hal + rows, 512), F32)],
        compiler_params=_params(2),
        name="pool_sgu_prompt",
    )(prev16, xc, xc, u, vn, wp, scale, ws, bs_t)


def _pool_sgu_sample_kernel(ws_ref, bs_ref, prev_ref, xc_ref, u_ref, vn_ref, wp_ref, sc_ref, o_ref,
                            *, start):
    nprev = prev_ref.shape[0]
    ntok = xc_ref.shape[0]
    for t in range(ntok):
        for g, w in enumerate(POOL_WINDOWS):
            gs = slice(g * C_GROUP, (g + 1) * C_GROUP)
            acc = None
            for s in range(w):
                j = nprev + t - s
                slab = prev_ref[j, :, gs] if j < nprev else xc_ref[j - nprev, :, gs]
                acc = slab if acc is None else acc + slab
            cnt = float(min(w, start + t + 1))
            pooled = acc / cnt - xc_ref[t, :, gs]
            o_ref[t, :, gs] = _nn(pooled.astype(BF), wp_ref[g]) * sc_ref[:, gs]
        for g in range(D_GROUPS):
            gs = slice(g * 128, (g + 1) * 128)
            s = None
            for j in range(t + 1):
                term = ws_ref[(g * ntok + t) * ntok + j] * vn_ref[j, :, gs]
                s = term if s is None else s + term
            s = s + bs_ref[g * ntok + t]
            o_ref[t, :, MIX_C + g * 128:MIX_C + (g + 1) * 128] = u_ref[t, :, gs] * s


def _pool_sgu_sample(ws_small, bs_small, prev_t, xc_t, u_t, vn_t, wp, scale, start):
    ntok, nreq, _ = xc_t.shape
    smem = pl.BlockSpec(memory_space=pltpu.SMEM)
    return pl.pallas_call(
        functools.partial(_pool_sgu_sample_kernel, start=start),
        grid=(1,),
        in_specs=[smem, smem, _const_spec(prev_t.shape), _const_spec(xc_t.shape), _const_spec(u_t.shape),
                  _const_spec(vn_t.shape), _const_spec(wp.shape), _const_spec(scale.shape)],
        out_specs=_const_spec((ntok, nreq, 1024)),
        out_shape=jax.ShapeDtypeStruct((ntok, nreq, 1024), F32),
        compiler_params=_params(1),
        name="pool_sgu_sample",
    )(ws_small, bs_small, prev_t, xc_t, u_t, vn_t, wp, scale)


def _even_weights(w_in):
    q_a, k_a, v_a, q_b, k_b, v_b, q_i, k_i, w_i = jnp.split(
        w_in, [512, 1024, 1536, 2048, 2176, 2304, 2816, 2880], axis=1)
    wa = jnp.concatenate([k_a.reshape(D_MODEL, H_A, 2 * DA), v_a.reshape(D_MODEL, H_A, DV_A)],
                         -1).reshape(D_MODEL, H_A * (2 * DA + DV_A))
    wb = jnp.concatenate([k_b, v_b, k_i], 1)
    wq = jnp.concatenate([q_a, q_b, q_i], 1)
    ww = jnp.concatenate([w_i, jnp.zeros((D_MODEL, 128 - HI), w_in.dtype)], 1)
    qscale = jnp.concatenate([jnp.full((512,), DA ** -0.5, F32), jnp.full((512,), DB ** -0.5, F32),
                              jnp.full((512,), DI ** -0.5, F32)]).reshape(1, 1536)
    wwt = jnp.concatenate([w_i.T, jnp.zeros((16 - HI, D_MODEL), w_in.dtype)], 0)
    natural = (wa.astype(BF), wb.astype(BF), wq.astype(BF), ww.astype(BF), qscale)
    feature_major = (wq.T.astype(BF), v_a.T.astype(BF), v_b.T.astype(BF), wwt.astype(BF))
    return natural, feature_major


def _a_pages(x):
    n, page, _ = x.shape
    return x.reshape(n, page, H_A, 2, 128).transpose(0, 1, 3, 2, 4).reshape(n, page * 2 * H_A, 128)


def _pad_rows(x, rows):
    return jnp.pad(x, ((0, 0), (0, rows - x.shape[1]), (0, 0)))


def kernel(x_prompt, x_sample, cache_a, cache_b, state_pool, page_table, w_in_e, lam_e, subln_g, w_out_e,
           w_in_o, w_pool, pool_scale, sgu_g, sgu_b, w_s, b_s, w_out_o, w_mlp1, w_mlp2, ln_g, ln_b):
    batch, seq, _ = x_prompt.shape
    nreq, ntok, _ = x_sample.shape
    npg = page_table.shape[1]
    page = cache_a.shape[2]
    past = npg * page
    xp = x_prompt.reshape(batch * seq, D_MODEL)
    xs = x_sample.reshape(nreq * ntok, D_MODEL)
    outs = {k: [] for k in ("a_p", "b_p", "pool_p", "a_s", "b_s", "pool_s", "v_s")}
    rpad = 8

    for l in range(DEPTH):
        i = l // 2
        row2 = lambda v: v.reshape(1, -1)
        if l % 2 == 0:
            lam_init = 0.8 - 0.6 * math.exp(-0.3 * l)
            (wa, wb, wq, ww, qscale), (wqt, wvat, wvbt, wwt) = _even_weights(w_in_e[i])
            g = row2(subln_g[i])
            w_out = w_out_e[i].astype(BF)
            na, nb, abf, bbf, qt, vat, vbt, wit = _proj_even_t(xp, wa, wb, wqt, wvat, wvbt, wwt)
            o_a = _diff_attn_prompt(lam_e[i], subln_g[i].reshape(-1, 1), qt, abf, vat, batch, seq, lam_init)
            o_b = _sparse_attn_prompt(qt, wit, bbf, vbt, batch, seq, min(TOPK_MAX, seq // 4))
            mix_p = [o_a, o_b]
            outs["a_p"].append(na.reshape(batch, seq, 2, H_A, 128).transpose(0, 1, 3, 2, 4)
                               .reshape(batch, seq, H_A, 2 * DA + DV_A))
            outs["b_p"].append(nb.reshape(batch, seq, 2 * DB + DI))
            nas, nbs, _, _, qs, wis = _proj_even(xs, wa, wb, wq, ww, qscale)
            o_s = _sample_even(
                page_table, lam_e[i], g,
                _pad_rows(qs.astype(F32).reshape(nreq, ntok, 1536), rpad),
                _pad_rows(wis.reshape(nreq, ntok, 128), rpad),
                _a_pages(_pad_rows(nas.reshape(nreq, ntok, 1024), rpad)),
                _pad_rows(nbs.reshape(nreq, ntok, 320), page).transpose(0, 2, 1),
                _a_pages(cache_a[i].reshape(-1, page, 1024)), cache_b[i].transpose(0, 2, 1),
                min(TOPK_MAX, (past + ntok) // 4), lam_init)
            mix_s = [o_s[:, :ntok].reshape(nreq * ntok, 1024)]
            outs["a_s"].append(nas.reshape(nreq, ntok, H_A, 2 * DA + DV_A))
            outs["b_s"].append(nbs.reshape(nreq, ntok, 2 * DB + DI))
        else:
            w_in = w_in_o[i].astype(BF)
            w_out = w_out_o[i].astype(BF)
            wp = w_pool[i].astype(BF)
            scale = row2(pool_scale[i])
            sg, sb = row2(sgu_g[i]), row2(sgu_b[i])
            xc, u, vn = _proj_odd(xp, w_in, sg, sb)
            prev16 = jnp.zeros((batch, POOL_BUF + 1, MIX_C), F32)
            mix_p = [_pool_sgu_prompt(prev16, xc, u, vn, wp, scale, w_s[i], b_s[i].T, batch, seq, 0)]
            outs["pool_p"].append(xc.reshape(batch, seq, MIX_C)[:, seq - POOL_BUF:])
            xcs, us, vns = _proj_odd(xs, w_in, sg, sb)
            tmaj = lambda v: v.reshape(nreq, ntok, -1).transpose(1, 0, 2)
            m_t = _pool_sgu_sample(
                w_s[i][:, :ntok, :ntok].reshape(-1), b_s[i][:, :ntok].reshape(-1),
                state_pool[i].transpose(1, 0, 2), tmaj(xcs), tmaj(us), tmaj(vns), wp, scale, past)
            mix_s = [m_t.transpose(1, 0, 2).reshape(nreq * ntok, 1024)]
            ext = jnp.concatenate([state_pool[i], xcs.reshape(nreq, ntok, MIX_C)], 1)
            outs["pool_s"].append(ext[:, ext.shape[1] - POOL_BUF:])
            outs["v_s"].append(vns.reshape(nreq, ntok, MIX_D))
        g0, b0, g1, b1 = row2(ln_g[l, 0]), row2(ln_b[l, 0]), row2(ln_g[l, 1]), row2(ln_b[l, 1])
        w1, w2 = w_mlp1[l].astype(BF), w_mlp2[l].astype(BF)
        xp = _outproj_res_ln(mix_p, w_out, xp, g0, b0)
        xp = _mlp_res_ln(xp, w1, w2, g1, b1)
        xs = _outproj_res_ln(mix_s, w_out, xs, g0, b0)
        xs = _mlp_res_ln(xs, w1, w2, g1, b1)

    st = lambda k: jnp.stack(outs[k])
    return (xp.reshape(batch, seq, D_MODEL), xs.reshape(nreq, ntok, D_MODEL), st("a_p"), st("b_p"),
            st("pool_p"), st("a_s"), st("b_s"), st("pool_s"), st("v_s"))
```

```python
import functools
import math

import jax
import jax.numpy as jnp
from jax import lax
from jax.experimental import pallas as pl
from jax.experimental.pallas import tpu as pltpu

D_MODEL = 1024
H_A = 4
DA = 64
DV_A = 128
H_B = 4
DB = 128
HI = 8
DI = 64
TOPK_MAX = 256
MIX_C = 512
MIX_D = 512
POOL_WINDOWS = (2, 4, 8, 16)
C_GROUP = 128
POOL_BUF = 15
CHUNK = 128
D_GROUPS = 4
D_FF = 4096
DEPTH = 2
ALPHA = (2 * DEPTH) ** 0.25
EPS = 1e-5

BF = jnp.bfloat16
F32 = jnp.float32
NEG = -1e30
INT_MIN = -(2 ** 31)
FLT_LOWEST = -3.4028234663852886e38
MANY = 1e9
LOG2E = math.log2(math.e)
V_ROWS = 144
VMEM_LIMIT_BYTES = 56 * 1024 * 1024


def _params(n_axes):
    return pltpu.CompilerParams(dimension_semantics=("arbitrary",) * n_axes,
                                vmem_limit_bytes=VMEM_LIMIT_BYTES)


def _nn(a, b):
    return jnp.dot(a, b, preferred_element_type=F32)


def _nt(a, b):
    return lax.dot_general(a, b, (((1,), (1,)), ((), ())), preferred_element_type=F32)


def _ln(z, g, b):
    mu = jnp.mean(z, -1, keepdims=True)
    d = z - mu
    var = jnp.mean(d * d, -1, keepdims=True)
    return d * lax.rsqrt(var + EPS) * g + b


def _const_spec(shape):
    nd = len(shape)
    return pl.BlockSpec(shape, lambda *_: (0,) * nd)


def _key_to_float(k):
    return lax.bitcast_convert_type(jnp.where(k < 0, k ^ jnp.int32(0x7FFFFFFF), k), F32)


def _kth_largest(count, shape, ksel, bits=1, n_finite=None, n_nonneg=None):
    if n_finite is None:
        n_finite, = count([jnp.full(shape, -jnp.inf, F32)], True)
    if n_nonneg is None:
        n_nonneg, = count([jnp.zeros(shape, F32)], False)
    nonneg = n_nonneg >= ksel
    start = (jnp.where(nonneg, jnp.int32(0), jnp.int32(INT_MIN)), jnp.where(nonneg, n_nonneg, MANY),
             jnp.where(nonneg, 0.0, n_nonneg))

    def refine(carry, shift, nbits):
        k, n_ge, n_above = carry
        cands = [k | lax.shift_left(jnp.int32(d), shift) for d in range(1, 2 ** nbits)]
        prev_ok = None
        for cand, n_cand in zip(cands, count([_key_to_float(c) for c in cands], False)):
            ok = n_cand >= ksel
            k, n_ge = jnp.where(ok, cand, k), jnp.where(ok, n_cand, n_ge)
            first_miss = jnp.logical_not(ok) if prev_ok is None else jnp.logical_and(prev_ok, jnp.logical_not(ok))
            n_above = jnp.where(first_miss, n_cand, n_above)
            prev_ok = ok
        return k, n_ge, n_above

    npass, rest = divmod(31, bits)
    carry = lax.fori_loop(0, npass, lambda it, cr: refine(cr, jnp.int32(31 - bits) - bits * it, bits), start)
    if rest:
        carry = refine(carry, jnp.int32(0), rest)
    k, n_ge, n_above = carry
    short = n_finite < ksel
    t = jnp.where(short, FLT_LOWEST, _key_to_float(k))
    need = jnp.where(short, MANY, ksel - n_above)
    tied = jnp.where(short, 0.0, jnp.where(n_ge > ksel, 1.0, 0.0))
    return t, need, tied


def _proj_even_kernel(x_ref, wa_ref, wb_ref, wq_ref, ww_ref, qs_ref,
                      na_ref, nb_ref, abf_ref, bbf_ref, q_ref, wi_ref):
    x = x_ref[...].astype(BF)
    a = _nn(x, wa_ref[...])
    na_ref[...] = a
    abf_ref[...] = a.astype(BF)
    b = _nn(x, wb_ref[...])
    nb_ref[...] = b
    bbf_ref[...] = b.astype(BF)
    q_ref[...] = (_nn(x, wq_ref[...]) * qs_ref[...]).astype(BF)
    wi_ref[...] = _nn(x, ww_ref[...]) * (HI ** -0.5)


def _proj_even(x, wa, wb, wq, ww, qscale, tm=256):
    n = x.shape[0]
    tm = min(tm, n)
    row = lambda w: pl.BlockSpec((tm, w), lambda i: (i, 0))
    return pl.pallas_call(
        _proj_even_kernel,
        grid=(n // tm,),
        in_specs=[row(D_MODEL), _const_spec(wa.shape), _const_spec(wb.shape),
                  _const_spec(wq.shape), _const_spec(ww.shape), _const_spec(qscale.shape)],
        out_specs=[row(1024), row(320), row(1024), row(320), row(1536), row(128)],
        out_shape=[jax.ShapeDtypeStruct((n, 1024), F32), jax.ShapeDtypeStruct((n, 320), F32),
                   jax.ShapeDtypeStruct((n, 1024), BF), jax.ShapeDtypeStruct((n, 320), BF),
                   jax.ShapeDtypeStruct((n, 1536), BF), jax.ShapeDtypeStruct((n, 128), F32)],
        compiler_params=_params(1),
        name="proj_even",
    )(x, wa, wb, wq, ww, qscale)


def _proj_even_t_kernel(x_ref, wa_ref, wb_ref, wqt_ref, wvat_ref, wvbt_ref, wwt_ref,
                        na_ref, nb_ref, abf_ref, bbf_ref, qt_ref, vat_ref, vbt_ref, wit_ref):
    x = x_ref[...]
    xb = x.astype(BF)
    xt = x.T.astype(BF)
    a = _nn(xb, wa_ref[...])
    tm = a.shape[0]
    for h in range(H_A):
        for part in range(2):
            na_ref[pl.ds(H_A * part + h, tm, stride=2 * H_A), :] = a[:, h * 256 + part * 128:h * 256 + (part + 1) * 128]
    abf_ref[...] = a.astype(BF)
    b = _nn(xb, wb_ref[...])
    nb_ref[...] = b
    bbf_ref[...] = b.astype(BF)
    qt = _nn(wqt_ref[...], xt)
    qt_ref[0:512, :] = (qt[0:512] * (DA ** -0.5 * LOG2E)).astype(BF)
    qt_ref[512:1024, :] = (qt[512:1024] * (DB ** -0.5 * LOG2E)).astype(BF)
    qt_ref[1024:1536, :] = (qt[1024:1536] * (DI ** -0.5)).astype(BF)
    pad = V_ROWS - 128
    ones_rows = jnp.where(lax.broadcasted_iota(jnp.int32, (pad, tm), 0) == 0, 1.0, 0.0).astype(BF)
    va = _nn(wvat_ref[...], xt).astype(BF)
    for h in range(H_A):
        vat_ref[h * V_ROWS:h * V_ROWS + 128, :] = va[h * 128:(h + 1) * 128]
        vat_ref[h * V_ROWS + 128:(h + 1) * V_ROWS, :] = ones_rows
    vbt_ref[0:128, :] = _nn(wvbt_ref[...], xt).astype(BF)
    vbt_ref[128:V_ROWS, :] = ones_rows
    wit_ref[...] = _nn(wwt_ref[...], xt) * (HI ** -0.5)


def _proj_even_t(x, wa, wb, wqt, wvat, wvbt, wwt, tm=512):
    n = x.shape[0]
    row = lambda w: pl.BlockSpec((tm, w), lambda i: (i, 0))
    col = lambda h: pl.BlockSpec((h, tm), lambda i: (0, i))
    return pl.pallas_call(
        _proj_even_t_kernel,
        grid=(n // tm,),
        in_specs=[row(D_MODEL)] + [_const_spec(w.shape) for w in (wa, wb, wqt, wvat, wvbt, wwt)],
        out_specs=[pl.BlockSpec((2 * H_A * tm, 128), lambda i: (i, 0)), row(320), row(1024), row(320),
                   col(1536), col(H_A * V_ROWS), col(V_ROWS), col(16)],
        out_shape=[jax.ShapeDtypeStruct((2 * H_A * n, 128), F32), jax.ShapeDtypeStruct((n, 320), F32),
                   jax.ShapeDtypeStruct((n, 1024), BF), jax.ShapeDtypeStruct((n, 320), BF),
                   jax.ShapeDtypeStruct((1536, n), BF), jax.ShapeDtypeStruct((H_A * V_ROWS, n), BF),
                   jax.ShapeDtypeStruct((V_ROWS, n), BF), jax.ShapeDtypeStruct((16, n), F32)],
        compiler_params=_params(1),
        name="proj_even_t",
    )(x, wa, wb, wqt, wvat, wvbt, wwt)


def _lambda(lam_ref, lam_init):
    lp = lam_ref[...]
    return (jnp.exp(jnp.sum(lp[0:1] * lp[1:2], axis=-1, keepdims=True))
            - jnp.exp(jnp.sum(lp[2:3] * lp[3:4], axis=-1, keepdims=True)) + lam_init)


def _split_q12(q):
    qf = q.astype(F32)
    lane = lax.broadcasted_iota(jnp.int32, qf.shape, 1)
    return jnp.concatenate([jnp.where(lane < DA, qf, 0.0), jnp.where(lane >= DA, qf, 0.0)],
                           axis=0).astype(BF)


def _subln(o, lam, g, lam_init):
    r = o.shape[0] // 2
    d = o[:r] - lam * o[r:]
    ms = jnp.mean(d * d, -1, keepdims=True)
    return d * lax.rsqrt(ms + EPS) * g * (1.0 - lam_init)


def _kth_largest_rows(sc_ref, nch, tk, ksel):
    rows = sc_ref.shape[0]

    def count(cands, strict):
        accs = [jnp.zeros((rows, tk), F32) for _ in cands]
        for c in range(nch):
            x = sc_ref[:, c * tk:(c + 1) * tk]
            accs = [a + jnp.where((x > cand) if strict else (x >= cand), 1.0, 0.0)
                    for a, cand in zip(accs, cands)]
        return [jnp.sum(a, -1, keepdims=True) for a in accs]

    return _kth_largest(count, (rows, 1), ksel, bits=2)


def _kth_largest_cols(sc_ref, nch, tk, ksel, n_finite=None, n_nonneg=None):
    cols = sc_ref.shape[1]

    def count(cands, strict):
        cand, = cands

        def body(c, acc):
            x = sc_ref[pl.ds(pl.multiple_of(c * tk, tk), tk), :]
            hit = (x > cand) if strict else (x >= cand)
            return acc + jnp.sum(jnp.where(hit, 1.0, 0.0).reshape(tk // 32, 32, cols), axis=0)
        acc = lax.fori_loop(0, nch, body, jnp.zeros((32, cols), F32))
        return [jnp.sum(acc, 0, keepdims=True)]

    return _kth_largest(count, (1, cols), ksel, 1, n_finite, n_nonneg)


def _selected(x, t, need, eq_before, tri):
    eq = jnp.where(x == t, 1.0, 0.0)
    rank = eq_before + _nn(eq.astype(BF), tri)
    tie_taken = jnp.where(rank <= need, eq, 0.0)
    return jnp.where(x > t, 1.0, tie_taken), jnp.sum(eq, -1, keepdims=True)


def _tri(n, lower):
    r = lax.broadcasted_iota(jnp.int32, (n, n), 0)
    c = lax.broadcasted_iota(jnp.int32, (n, n), 1)
    return jnp.where((r >= c) if lower else (r <= c), 1.0, 0.0).astype(BF)


def _flash_key_major(npair, nlast, nchain, tq, qk, mask, v_t, s_ref, p_ref, al_ref, acc_ref):
    def scores(c, slot):
        for j in range(nchain):
            s_ref[slot, j] = qk(c, j)

    def softmax(c, slot, ms, last):
        ms_new = []
        for j in range(nchain):
            s = mask(c, j, s_ref[slot, j], last)
            m_new = jnp.maximum(ms[j], jnp.max(s, 0, keepdims=True))
            p_ref[slot, j] = jnp.exp2(s - m_new).astype(BF)
            al_ref[slot, j] = jnp.exp2(ms[j] - m_new)
            ms_new.append(m_new)
        return tuple(ms_new)

    def fold(c, slot):
        for j in range(nchain):
            acc_ref[j] = al_ref[slot, j] * acc_ref[j] + _nn(v_t(c, j), p_ref[slot, j])

    acc_ref[...] = jnp.zeros(acc_ref.shape, F32)
    p_ref[1] = jnp.zeros(p_ref.shape[1:], BF)
    al_ref[1] = jnp.ones(al_ref.shape[1:], F32)
    scores(0, 0)

    def pair(c, carry, last, final):
        fold(jnp.maximum(c - 1, 0), 1)
        carry = softmax(c, 0, carry, last)
        scores(c + 1, 1)
        fold(c, 0)
        carry = softmax(c + 1, 1, carry, last)
        if not final:
            scores(c + 2, 0)
        return carry

    n_plain = npair - nlast
    carry = lax.fori_loop(0, n_plain, lambda c, cr: pair(2 * c, cr, False, False),
                          tuple(jnp.full((1, tq), NEG, F32) for _ in range(nchain)))
    for d in range(nlast):
        carry = pair(2 * (n_plain + d), carry, True, d == nlast - 1)
    fold(2 * npair - 1, 1)


def _diff_kernel(lam_ref, g_ref, qt_ref, a_ref, vt_ref, o_ref, qh_ref, s_ref, p_ref, al_ref, acc_ref,
                 *, tq, tk, lam_init):
    i = pl.program_id(1)
    lam = _lambda(lam_ref, lam_init)
    sub = lax.broadcasted_iota(jnp.int32, (128, tq), 0)
    key = lax.broadcasted_iota(jnp.int32, (tk, 2 * tq), 0)
    qry = i * tq + lax.broadcasted_iota(jnp.int32, (tk, 2 * tq), 1) % tq
    for h in range(H_A):
        qt = qt_ref[h * 128:(h + 1) * 128, :].astype(F32)
        qh_ref[h, :, 0:tq] = jnp.where(sub < DA, qt, 0.0).astype(BF)
        qh_ref[h, :, tq:2 * tq] = jnp.where(sub >= DA, qt, 0.0).astype(BF)

    def qk(c, h):
        return _nn(a_ref[pl.ds(pl.multiple_of(c * tk, tk), tk), h * 256:h * 256 + 128], qh_ref[h])

    def mask(c, h, s, last):
        return jnp.where(c * tk + key <= qry, s, NEG) if last else s

    def v_t(c, h):
        return vt_ref[h * V_ROWS:(h + 1) * V_ROWS, pl.ds(pl.multiple_of(c * tk, tk), tk)]

    nlast = tq // (2 * tk)
    _flash_key_major((i + 1) * nlast, nlast, H_A, 2 * tq, qk, mask, v_t, s_ref, p_ref, al_ref, acc_ref)
    for h in range(H_A):
        o12 = acc_ref[h, 0:DV_A, :] * (1.0 / acc_ref[h, DV_A:DV_A + 1, :])
        d = o12[:, 0:tq] - lam * o12[:, tq:2 * tq]
        ms = jnp.mean(d * d, 0, keepdims=True)
        o = d * lax.rsqrt(ms + EPS) * g_ref[...] * (1.0 - lam_init)
        o_ref[:, h * 128:(h + 1) * 128] = o.T.astype(o_ref.dtype)


def _flash_scratch(nchain, feat, tk, tq):
    return [pltpu.VMEM((2, nchain, tk, tq), F32), pltpu.VMEM((2, nchain, tk, tq), BF),
            pltpu.VMEM((2, nchain, 1, tq), F32), pltpu.VMEM((nchain, feat, tq), F32)]


def _diff_attn_prompt(lam_e, g_col, qt, a_bf, vat, batch, seq, lam_init, tq=512, tk=256):
    nq = seq // tq
    return pl.pallas_call(
        functools.partial(_diff_kernel, tq=tq, tk=tk, lam_init=lam_init),
        grid=(batch, nq),
        in_specs=[_const_spec(lam_e.shape), _const_spec(g_col.shape),
                  pl.BlockSpec((512, tq), lambda b, i: (0, b * nq + i)),
                  pl.BlockSpec((seq, 1024), lambda b, i: (b, 0)),
                  pl.BlockSpec((H_A * V_ROWS, seq), lambda b, i: (0, b))],
        out_specs=pl.BlockSpec((tq, 512), lambda b, i: (b * nq + i, 0)),
        out_shape=jax.ShapeDtypeStruct((batch * seq, 512), BF),
        scratch_shapes=[pltpu.VMEM((H_A, 128, 2 * tq), BF)] + _flash_scratch(H_A, V_ROWS, tk, 2 * tq),
        compiler_params=_params(2),
        name="diff_attn_prompt",
    )(lam_e, g_col, qt, a_bf, vat)


def _sparse_kernel(qbt_ref, qit_ref, wit_ref, b_ref, vbt_ref, o_ref, sc_ref, qiw_ref, qbw_ref,
                   s_ref, p_ref, al_ref, acc_ref, *, tq, tk, ksel):
    i = pl.program_id(1)
    nch = (i + 1) * (tq // tk)
    ta = tk // 2
    key = lax.broadcasted_iota(jnp.int32, (ta, tq), 0)
    qry = i * tq + lax.broadcasted_iota(jnp.int32, (ta, tq), 1)
    w = wit_ref[...]
    for h in range(HI):
        qiw_ref[:, h * tq:(h + 1) * tq] = qit_ref[h * DI:(h + 1) * DI, :]
    for h in range(H_B):
        qbw_ref[:, h * tq:(h + 1) * tq] = qbt_ref[h * DB:(h + 1) * DB, :]
    heads = lambda x: jnp.concatenate([x] * H_B, axis=1)

    def score_chunk(c, counts):
        n_finite, n_nonneg = counts
        for u in range(2):
            r0 = pl.multiple_of(c * tk + u * ta, ta)
            d = jnp.maximum(_nn(b_ref[pl.ds(r0, ta), 2 * DB:2 * DB + DI], qiw_ref[...]), 0.0)
            acc = w[0:1, :] * d[:, 0:tq]
            for h in range(1, HI):
                acc = acc + w[h:h + 1, :] * d[:, h * tq:(h + 1) * tq]
            sc = jnp.where(r0 + key <= qry, acc, -jnp.inf)
            sc_ref[pl.ds(r0, ta), :] = sc
            n_finite = n_finite + jnp.sum(jnp.where(sc > -jnp.inf, 1.0, 0.0), 0, keepdims=True)
            n_nonneg = n_nonneg + jnp.sum(jnp.where(sc >= 0.0, 1.0, 0.0), 0, keepdims=True)
        return n_finite, n_nonneg

    zero = jnp.zeros((1, tq), F32)
    n_finite, n_nonneg = lax.fori_loop(0, nch, score_chunk, (zero, zero))
    t, need, tied = _kth_largest_cols(sc_ref, nch, tk, ksel, n_finite, n_nonneg)

    def qk(c, _):
        return _nn(b_ref[pl.ds(pl.multiple_of(c * ta, ta), ta), 0:DB], qbw_ref[...])

    def v_t(c, _):
        return vbt_ref[:, pl.ds(pl.multiple_of(c * ta, ta), ta)]

    def to_bias(exact_ties):
        def body(c, eq_before):
            for u in range(2):
                rows = pl.ds(pl.multiple_of(c * tk + u * ta, ta), ta)
                x = sc_ref[rows, :]
                if exact_ties:
                    eq = jnp.where(x == t, 1.0, 0.0)
                    rank = eq_before + _nn(_tri(ta, True), eq.astype(BF))
                    sc_ref[rows, :] = jnp.where(
                        x > t, 0.0, jnp.where(x == t, jnp.where(rank <= need, 0.0, NEG), NEG))
                    eq_before = eq_before + jnp.sum(eq, 0, keepdims=True)
                else:
                    sc_ref[rows, :] = jnp.where(x >= t, 0.0, NEG)
            return eq_before
        return lax.fori_loop(0, nch, body, jnp.zeros((1, tq), F32))

    lax.cond(jnp.max(tied) > 0.0, lambda: to_bias(True), lambda: to_bias(False))

    def add_bias(c, _, s, last):
        return s + heads(sc_ref[pl.ds(pl.multiple_of(c * ta, ta), ta), :])

    _flash_key_major(nch, 1, 1, H_B * tq, qk, add_bias, v_t, s_ref, p_ref, al_ref, acc_ref)
    o = acc_ref[0, 0:DB, :] * (1.0 / acc_ref[0, DB:DB + 1, :])
    for h in range(H_B):
        o_ref[:, h * DB:(h + 1) * DB] = o[:, h * tq:(h + 1) * tq].T.astype(o_ref.dtype)


def _sparse_attn_prompt(qt, wit, b_bf, vbt, batch, seq, ksel, tq=256, tk=256):
    nq = seq // tq
    return pl.pallas_call(
        functools.partial(_sparse_kernel, tq=tq, tk=tk, ksel=ksel),
        grid=(batch, nq),
        in_specs=[pl.BlockSpec((512, tq), lambda b, i: (1, b * nq + i)),
                  pl.BlockSpec((512, tq), lambda b, i: (2, b * nq + i)),
                  pl.BlockSpec((16, tq), lambda b, i: (0, b * nq + i)),
                  pl.BlockSpec((seq, 320), lambda b, i: (b, 0)),
                  pl.BlockSpec((V_ROWS, seq), lambda b, i: (0, b))],
        out_specs=pl.BlockSpec((tq, 512), lambda b, i: (b * nq + i, 0)),
        out_shape=jax.ShapeDtypeStruct((batch * seq, 512), BF),
        scratch_shapes=[pltpu.VMEM((seq, tq), F32), pltpu.VMEM((DI, HI * tq), BF),
                        pltpu.VMEM((DB, H_B * tq), BF)] + _flash_scratch(1, V_ROWS, tk // 2, H_B * tq),
        compiler_params=_params(2),
        name="sparse_attn_prompt",
    )(qt, qt, wit, b_bf, vbt)


def _sample_even_kernel(pt_ref, lam_ref, g_ref, q_ref, wi_ref, *rest, npg, page, ksel, lam_init):
    del pt_ref
    nblk = npg + 1
    a_pages = rest[:nblk]
    b_pages = rest[nblk:2 * nblk]
    o_ref = rest[2 * nblk]
    sc_ref = rest[2 * nblk + 1]
    r = q_ref.shape[0]
    lam = _lambda(lam_ref, lam_init)
    q = q_ref[...]

    def a_rows(p, j):
        n_pos = a_pages[p].shape[0] // (2 * H_A)
        x = a_pages[p][pl.ds(j, n_pos, stride=2 * H_A), :]
        if n_pos < page:
            x = jnp.concatenate([x, jnp.zeros((page - n_pos, 128), F32)], axis=0)
        return x.astype(BF)

    def new_ok(rows):
        tok = lax.broadcasted_iota(jnp.int32, (rows, page), 0) % r
        return lax.broadcasted_iota(jnp.int32, (rows, page), 1) <= tok

    ok2 = new_ok(2 * r)
    for h in range(H_A):
        qq = _split_q12(q[:, h * 128:(h + 1) * 128])
        ss = [_nt(qq, a_rows(p, h)) for p in range(nblk)]
        ss[npg] = jnp.where(ok2, ss[npg], NEG)
        m = functools.reduce(jnp.maximum, ss)
        m = jnp.max(m, -1, keepdims=True)
        ps = [jnp.exp(s - m) for s in ss]
        l = jnp.sum(functools.reduce(lambda x, y: x + y, ps), -1, keepdims=True)
        acc = functools.reduce(lambda x, y: x + y, [
            _nn(ps[p].astype(BF), a_rows(p, H_A + h))
            for p in range(nblk)])
        o_ref[:, h * 128:(h + 1) * 128] = _subln(acc / l, lam, g_ref[...], lam_init)

    qi = jnp.concatenate([q[:, 1024 + h * DI:1024 + (h + 1) * DI] for h in range(HI)], axis=0).astype(BF)
    w = wi_ref[...]
    wcol = jnp.concatenate([w[:, h:h + 1] for h in range(HI)], axis=0)
    ok1 = new_ok(r)
    for p in range(nblk):
        d = jnp.maximum(_nn(qi, b_pages[p][2 * DB:2 * DB + DI, :].astype(BF)), 0.0) * wcol
        sc = d[0:r]
        for h in range(1, HI):
            sc = sc + d[h * r:(h + 1) * r]
        if p == npg:
            sc = jnp.where(ok1, sc, -jnp.inf)
        sc_ref[:, p * page:(p + 1) * page] = sc
    t, need, _ = _kth_largest_rows(sc_ref, nblk, page, ksel)

    tri = _tri(page, False)
    q4 = jnp.concatenate([q[:, 512 + h * DB:512 + (h + 1) * DB] for h in range(H_B)], axis=0).astype(BF)
    eq_before = jnp.zeros((r, 1), F32)
    ss, sels = [], []
    for p in range(nblk):
        sel, n_eq = _selected(sc_ref[:, p * page:(p + 1) * page], t, need, eq_before, tri)
        eq_before = eq_before + n_eq
        if p == npg:
            sel = jnp.where(ok1, sel, 0.0)
        sels.append(jnp.concatenate([sel] * H_B, axis=0) > 0.5)
        ss.append(_nn(q4, b_pages[p][0:DB, :].astype(BF)))
    m = functools.reduce(jnp.maximum, [jnp.where(sl, s, NEG) for sl, s in zip(sels, ss)])
    m = jnp.max(m, -1, keepdims=True)
    ps = [jnp.where(sl, jnp.exp(s - m), 0.0) for sl, s in zip(sels, ss)]
    l = jnp.sum(functools.reduce(lambda x, y: x + y, ps), -1, keepdims=True)
    acc = functools.reduce(lambda x, y: x + y, [
        _nt(ps[p].astype(BF), b_pages[p][DB:2 * DB, :].astype(BF)) for p in range(nblk)])
    o = acc / l
    for h in range(H_B):
        o_ref[:, 512 + h * DB:512 + (h + 1) * DB] = o[h * r:(h + 1) * r]


def _sample_even(page_table, lam_e, g, qs, wis, anew_pg, bnew_pg, cache_a_pg, cache_b_pg, ksel, lam_init):
    nreq, npg = page_table.shape
    page = cache_b_pg.shape[2]
    r = qs.shape[1]
    req = lambda a: pl.BlockSpec((None,) + a.shape[1:], lambda i, pt: (i, 0, 0))

    def page_spec(a, p):
        return pl.BlockSpec((None,) + a.shape[1:], lambda i, pt, p=p: (pt[i, p], 0, 0))

    in_specs = [_const_spec(lam_e.shape), _const_spec(g.shape), req(qs), req(wis)]
    in_specs += [page_spec(cache_a_pg, p) for p in range(npg)] + [req(anew_pg)]
    in_specs += [page_spec(cache_b_pg, p) for p in range(npg)] + [req(bnew_pg)]
    grid_spec = pltpu.PrefetchScalarGridSpec(
        num_scalar_prefetch=1, grid=(nreq,), in_specs=in_specs,
        out_specs=pl.BlockSpec((None, r, 1024), lambda i, pt: (i, 0, 0)),
        scratch_shapes=[pltpu.VMEM((r, (npg + 1) * page), F32)])
    return pl.pallas_call(
        functools.partial(_sample_even_kernel, npg=npg, page=page, ksel=ksel, lam_init=lam_init),
        grid_spec=grid_spec,
        out_shape=jax.ShapeDtypeStruct((nreq, r, 1024), F32),
        compiler_params=_params(1),
        name="sample_even",
    )(page_table, lam_e, g, qs, wis, *([cache_a_pg] * npg), anew_pg, *([cache_b_pg] * npg), bnew_pg)


def _outproj_kernel(*refs, n_lhs):
    lhs = refs[:n_lhs]
    w_ref, x_ref, g_ref, b_ref, o_ref = refs[n_lhs:]
    y = None
    k0 = 0
    for a_ref in lhs:
        kw = a_ref.shape[1]
        part = _nn(a_ref[...].astype(BF), w_ref[k0:k0 + kw, :])
        y = part if y is None else y + part
        k0 += kw
    o_ref[...] = _ln(ALPHA * x_ref[...] + y, g_ref[...], b_ref[...])


def _outproj_res_ln(lhs, w, x, g, b, tm=512):
    n = x.shape[0]
    tm = min(tm, n)
    row = lambda wd: pl.BlockSpec((tm, wd), lambda i: (i, 0))
    return pl.pallas_call(
        functools.partial(_outproj_kernel, n_lhs=len(lhs)),
        grid=(n // tm,),
        in_specs=[row(a.shape[1]) for a in lhs] + [_const_spec(w.shape), row(D_MODEL),
                                                   _const_spec(g.shape), _const_spec(b.shape)],
        out_specs=row(D_MODEL),
        out_shape=jax.ShapeDtypeStruct((n, D_MODEL), F32),
        compiler_params=_params(1),
        name="outproj_res_ln",
    )(*lhs, w, x, g, b)


def _mlp_kernel(x_ref, w1_ref, w2_ref, g_ref, b_ref, o_ref, acc_ref, *, ck):
    x = x_ref[...]
    xb = x.astype(BF)
    for c in range(D_FF // ck):
        h = jnp.maximum(_nn(xb, w1_ref[:, c * ck:(c + 1) * ck]), 0.0)
        part = _nn((h * h).astype(BF), w2_ref[c * ck:(c + 1) * ck, :])
        if c == 0:
            acc_ref[...] = part
        else:
            acc_ref[...] += part
    o_ref[...] = _ln(ALPHA * x + acc_ref[...], g_ref[...], b_ref[...])


def _mlp_res_ln(x, w1, w2, g, b, tm=512, ck=512):
    n = x.shape[0]
    tm = min(tm, n)
    row = pl.BlockSpec((tm, D_MODEL), lambda i: (i, 0))
    resident = lambda s: pl.BlockSpec(s, lambda i: (0, 0), pipeline_mode=pl.Buffered(1))
    return pl.pallas_call(
        functools.partial(_mlp_kernel, ck=ck),
        grid=(n // tm,),
        in_specs=[row, resident(w1.shape), resident(w2.shape), _const_spec(g.shape), _const_spec(b.shape)],
        out_specs=row,
        out_shape=jax.ShapeDtypeStruct((n, D_MODEL), F32),
        scratch_shapes=[pltpu.VMEM((tm, D_MODEL), F32)],
        compiler_params=_params(1),
        name="mlp_res_ln",
    )(x, w1, w2, g, b)


def _gelu(x):
    return 0.5 * x * (1.0 + jnp.tanh(math.sqrt(2.0 / math.pi) * (x + 0.044715 * (x * x * x))))


def _proj_odd_kernel(x_ref, w_ref, g_ref, b_ref, xc_ref, u_ref, vn_ref):
    h = _nn(x_ref[...].astype(BF), w_ref[...])
    xc_ref[...] = h[:, :MIX_C]
    u_ref[...] = _gelu(h[:, MIX_C:MIX_C + MIX_D])
    vn_ref[...] = _ln(_gelu(h[:, MIX_C + MIX_D:]), g_ref[...], b_ref[...])


def _proj_odd(x, w, g, b, tm=512):
    n = x.shape[0]
    tm = min(tm, n)
    row = lambda wd: pl.BlockSpec((tm, wd), lambda i: (i, 0))
    return pl.pallas_call(
        _proj_odd_kernel,
        grid=(n // tm,),
        in_specs=[row(D_MODEL), _const_spec(w.shape), _const_spec(g.shape), _const_spec(b.shape)],
        out_specs=[row(512), row(512), row(512)],
        out_shape=[jax.ShapeDtypeStruct((n, 512), F32)] * 3,
        compiler_params=_params(1),
        name="proj_odd",
    )(x, w, g, b)


def _pool_sgu_kernel(prev_ref, halo_ref, xc_ref, u_ref, vn_ref, wp_ref, sc_ref, ws_ref, bs_ref,
                     o_ref, ext_ref, *, start):
    t = pl.program_id(1)
    hal = prev_ref.shape[0]
    rows = xc_ref.shape[0]
    ext_ref[0:hal, :] = jnp.where(t == 0, prev_ref[...], halo_ref[...])
    ext_ref[hal:hal + rows, :] = xc_ref[...]
    pos = start + t * rows + lax.broadcasted_iota(jnp.int32, (rows, 1), 0)
    for g, w in enumerate(POOL_WINDOWS):
        gs = slice(g * C_GROUP, (g + 1) * C_GROUP)
        acc = ext_ref[hal:hal + rows, gs]
        for s in range(1, w):
            acc = acc + ext_ref[hal - s:hal - s + rows, gs]
        cnt = jnp.minimum(w, pos + 1).astype(F32)
        pooled = acc / cnt - xc_ref[:, gs]
        c = _nn(pooled.astype(BF), wp_ref[g]) * sc_ref[:, gs]
        o_ref[:, gs] = c.astype(o_ref.dtype)
    r = lax.broadcasted_iota(jnp.int32, (CHUNK, CHUNK), 0)
    cc = lax.broadcasted_iota(jnp.int32, (CHUNK, CHUNK), 1)
    for g in range(D_GROUPS):
        gs = slice(g * 128, (g + 1) * 128)
        ws = jnp.where(r >= cc, ws_ref[g], 0.0).astype(BF)
        for k in range(rows // CHUNK):
            ks = slice(k * CHUNK, (k + 1) * CHUNK)
            s = _nn(ws, vn_ref[ks, gs].astype(BF)) + bs_ref[:, g:g + 1]
            o_ref[ks, MIX_C + g * 128:MIX_C + (g + 1) * 128] = (u_ref[ks, gs] * s).astype(o_ref.dtype)


def _pool_sgu_prompt(prev16, xc, u, vn, wp, scale, ws, bs_t, batch, seq, start, rows=512):
    rows = min(rows, seq)
    nt = seq // rows
    hal = prev16.shape[1]
    per = rows // hal
    row = pl.BlockSpec((rows, 512), lambda b, t: (b * nt + t, 0))
    return pl.pallas_call(
        functools.partial(_pool_sgu_kernel, start=start),
        grid=(batch, nt),
        in_specs=[pl.BlockSpec((None, hal, 512), lambda b, t: (b, 0, 0)),
                  pl.BlockSpec((hal, 512), lambda b, t: (jnp.maximum((b * nt + t) * per - 1, 0), 0)),
                  row, row, row, _const_spec(wp.shape), _const_spec(scale.shape),
                  _const_spec(ws.shape), _const_spec(bs_t.shape)],
        out_specs=pl.BlockSpec((rows, 1024), lambda b, t: (b * nt + t, 0)),
        out_shape=jax.ShapeDtypeStruct((batch * seq, 1024), BF),
        scratch_shapes=[pltpu.VMEM((hal + rows, 512), F32)],
        compiler_params=_params(2),
        name="pool_sgu_prompt",
    )(prev16, xc, xc, u, vn, wp, scale, ws, bs_t)


def _pool_sgu_sample_kernel(ws_ref, bs_ref, prev_ref, xc_ref, u_ref, vn_ref, wp_ref, sc_ref, o_ref,
                            *, start):
    nprev = prev_ref.shape[0]
    ntok = xc_ref.shape[0]
    for t in range(ntok):
        for g, w in enumerate(POOL_WINDOWS):
            gs = slice(g * C_GROUP, (g + 1) * C_GROUP)
            acc = None
            for s in range(w):
                j = nprev + t - s
                slab = prev_ref[j, :, gs] if j < nprev else xc_ref[j - nprev, :, gs]
                acc = slab if acc is None else acc + slab
            cnt = float(min(w, start + t + 1))
            pooled = acc / cnt - xc_ref[t, :, gs]
            o_ref[t, :, gs] = _nn(pooled.astype(BF), wp_ref[g]) * sc_ref[:, gs]
        for g in range(D_GROUPS):
            gs = slice(g * 128, (g + 1) * 128)
            s = None
            for j in range(t + 1):
                term = ws_ref[(g * ntok + t) * ntok + j] * vn_ref[j, :, gs]
                s = term if s is None else s + term
            s = s + bs_ref[g * ntok + t]
            o_ref[t, :, MIX_C + g * 128:MIX_C + (g + 1) * 128] = u_ref[t, :, gs] * s


def _pool_sgu_sample(ws_small, bs_small, prev_t, xc_t, u_t, vn_t, wp, scale, start):
    ntok, nreq, _ = xc_t.shape
    smem = pl.BlockSpec(memory_space=pltpu.SMEM)
    return pl.pallas_call(
        functools.partial(_pool_sgu_sample_kernel, start=start),
        grid=(1,),
        in_specs=[smem, smem, _const_spec(prev_t.shape), _const_spec(xc_t.shape), _const_spec(u_t.shape),
                  _const_spec(vn_t.shape), _const_spec(wp.shape), _const_spec(scale.shape)],
        out_specs=_const_spec((ntok, nreq, 1024)),
        out_shape=jax.ShapeDtypeStruct((ntok, nreq, 1024), F32),
        compiler_params=_params(1),
        name="pool_sgu_sample",
    )(ws_small, bs_small, prev_t, xc_t, u_t, vn_t, wp, scale)


def _even_weights(w_in):
    q_a, k_a, v_a, q_b, k_b, v_b, q_i, k_i, w_i = jnp.split(
        w_in, [512, 1024, 1536, 2048, 2176, 2304, 2816, 2880], axis=1)
    wa = jnp.concatenate([k_a.reshape(D_MODEL, H_A, 2 * DA), v_a.reshape(D_MODEL, H_A, DV_A)],
                         -1).reshape(D_MODEL, H_A * (2 * DA + DV_A))
    wb = jnp.concatenate([k_b, v_b, k_i], 1)
    wq = jnp.concatenate([q_a, q_b, q_i], 1)
    ww = jnp.concatenate([w_i, jnp.zeros((D_MODEL, 128 - HI), w_in.dtype)], 1)
    qscale = jnp.concatenate([jnp.full((512,), DA ** -0.5, F32), jnp.full((512,), DB ** -0.5, F32),
                              jnp.full((512,), DI ** -0.5, F32)]).reshape(1, 1536)
    wwt = jnp.concatenate([w_i.T, jnp.zeros((16 - HI, D_MODEL), w_in.dtype)], 0)
    natural = (wa.astype(BF), wb.astype(BF), wq.astype(BF), ww.astype(BF), qscale)
    feature_major = (wq.T.astype(BF), v_a.T.astype(BF), v_b.T.astype(BF), wwt.astype(BF))
    return natural, feature_major


def _a_pages(x):
    n, page, _ = x.shape
    return x.reshape(n, page, H_A, 2, 128).transpose(0, 1, 3, 2, 4).reshape(n, page * 2 * H_A, 128)


def _pad_rows(x, rows):
    return jnp.pad(x, ((0, 0), (0, rows - x.shape[1]), (0, 0)))


def kernel(x_prompt, x_sample, cache_a, cache_b, state_pool, page_table, w_in_e, lam_e, subln_g, w_out_e,
           w_in_o, w_pool, pool_scale, sgu_g, sgu_b, w_s, b_s, w_out_o, w_mlp1, w_mlp2, ln_g, ln_b):
    batch, seq, _ = x_prompt.shape
    nreq, ntok, _ = x_sample.shape
    npg = page_table.shape[1]
    page = cache_a.shape[2]
    past = npg * page
    xp = x_prompt.reshape(batch * seq, D_MODEL)
    xs = x_sample.reshape(nreq * ntok, D_MODEL)
    outs = {k: [] for k in ("a_p", "b_p", "pool_p", "a_s", "b_s", "pool_s", "v_s")}
    rpad = 8

    for l in range(DEPTH):
        i = l // 2
        row2 = lambda v: v.reshape(1, -1)
        if l % 2 == 0:
            lam_init = 0.8 - 0.6 * math.exp(-0.3 * l)
            (wa, wb, wq, ww, qscale), (wqt, wvat, wvbt, wwt) = _even_weights(w_in_e[i])
            g = row2(subln_g[i])
            w_out = w_out_e[i].astype(BF)
            na, nb, abf, bbf, qt, vat, vbt, wit = _proj_even_t(xp, wa, wb, wqt, wvat, wvbt, wwt)
            o_a = _diff_attn_prompt(lam_e[i], subln_g[i].reshape(-1, 1), qt, abf, vat, batch, seq, lam_init)
            o_b = _sparse_attn_prompt(qt, wit, bbf, vbt, batch, seq, min(TOPK_MAX, seq // 4))
            mix_p = [o_a, o_b]
            outs["a_p"].append(na.reshape(batch, seq, 2, H_A, 128).transpose(0, 1, 3, 2, 4)
                               .reshape(batch, seq, H_A, 2 * DA + DV_A))
            outs["b_p"].append(nb.reshape(batch, seq, 2 * DB + DI))
            nas, nbs, _, _, qs, wis = _proj_even(xs, wa, wb, wq, ww, qscale)
            o_s = _sample_even(
                page_table, lam_e[i], g,
                _pad_rows(qs.astype(F32).reshape(nreq, ntok, 1536), rpad),
                _pad_rows(wis.reshape(nreq, ntok, 128), rpad),
                _a_pages(_pad_rows(nas.reshape(nreq, ntok, 1024), rpad)),
                _pad_rows(nbs.reshape(nreq, ntok, 320), page).transpose(0, 2, 1),
                _a_pages(cache_a[i].reshape(-1, page, 1024)), cache_b[i].transpose(0, 2, 1),
                min(TOPK_MAX, (past + ntok) // 4), lam_init)
            mix_s = [o_s[:, :ntok].reshape(nreq * ntok, 1024)]
            outs["a_s"].append(nas.reshape(nreq, ntok, H_A, 2 * DA + DV_A))
            outs["b_s"].append(nbs.reshape(nreq, ntok, 2 * DB + DI))
        else:
            w_in = w_in_o[i].astype(BF)
            w_out = w_out_o[i].astype(BF)
            wp = w_pool[i].astype(BF)
            scale = row2(pool_scale[i])
            sg, sb = row2(sgu_g[i]), row2(sgu_b[i])
            xc, u, vn = _proj_odd(xp, w_in, sg, sb)
            prev16 = jnp.zeros((batch, POOL_BUF + 1, MIX_C), F32)
            mix_p = [_pool_sgu_prompt(prev16, xc, u, vn, wp, scale, w_s[i], b_s[i].T, batch, seq, 0)]
            outs["pool_p"].append(xc.reshape(batch, seq, MIX_C)[:, seq - POOL_BUF:])
            xcs, us, vns = _proj_odd(xs, w_in, sg, sb)
            tmaj = lambda v: v.reshape(nreq, ntok, -1).transpose(1, 0, 2)
            m_t = _pool_sgu_sample(
                w_s[i][:, :ntok, :ntok].reshape(-1), b_s[i][:, :ntok].reshape(-1),
                state_pool[i].transpose(1, 0, 2), tmaj(xcs), tmaj(us), tmaj(vns), wp, scale, past)
            mix_s = [m_t.transpose(1, 0, 2).reshape(nreq * ntok, 1024)]
            ext = jnp.concatenate([state_pool[i], xcs.reshape(nreq, ntok, MIX_C)], 1)
            outs["pool_s"].append(ext[:, ext.shape[1] - POOL_BUF:])
            outs["v_s"].append(vns.reshape(nreq, ntok, MIX_D))
        g0, b0, g1, b1 = row2(ln_g[l, 0]), row2(ln_b[l, 0]), row2(ln_g[l, 1]), row2(ln_b[l, 1])
        w1, w2 = w_mlp1[l].astype(BF), w_mlp2[l].astype(BF)
        xp = _outproj_res_ln(mix_p, w_out, xp, g0, b0)
        xp = _mlp_res_ln(xp, w1, w2, g1, b1)
        xs = _outproj_res_ln(mix_s, w_out, xs, g0, b0)
        xs = _mlp_res_ln(xs, w1, w2, g1, b1)

    st = lambda k: jnp.stack(outs[k])
    return (xp.reshape(batch, seq, D_MODEL), xs.reshape(nreq, ntok, D_MODEL), st("a_p"), st("b_p"),
            st("pool_p"), st("a_s"), st("b_s"), st("pool_s"), st("v_s"))
```

```python
import functools
import math

import jax
import jax.numpy as jnp
from jax import lax
from jax.experimental import pallas as pl
from jax.experimental.pallas import tpu as pltpu

D_MODEL = 1024
H_A = 4
DA = 64
DV_A = 128
H_B = 4
DB = 128
HI = 8
DI = 64
TOPK_MAX = 256
MIX_C = 512
MIX_D = 512
POOL_WINDOWS = (2, 4, 8, 16)
C_GROUP = 128
POOL_BUF = 15
CHUNK = 128
D_GROUPS = 4
D_FF = 4096
DEPTH = 2
ALPHA = (2 * DEPTH) ** 0.25
EPS = 1e-5

BF = jnp.bfloat16
F32 = jnp.float32
NEG = -1e30
INT_MIN = -(2 ** 31)
FLT_LOWEST = -3.4028234663852886e38
MANY = 1e9
LOG2E = math.log2(math.e)
V_ROWS = 144
VMEM_LIMIT_BYTES = 56 * 1024 * 1024
ROW_TILE = 512
FF_CHUNK = 512
DIFF_QUERY_TILE = 512
DIFF_KEY_TILE = 256
SPARSE_QUERY_TILE = 256
SPARSE_KEY_TILE = 256


def _params(n_axes):
    return pltpu.CompilerParams(dimension_semantics=("arbitrary",) * n_axes,
                                vmem_limit_bytes=VMEM_LIMIT_BYTES)


def _nn(a, b):
    return jnp.dot(a, b, preferred_element_type=F32)


def _nt(a, b):
    return lax.dot_general(a, b, (((1,), (1,)), ((), ())), preferred_element_type=F32)


def _ln(z, g, b):
    mu = jnp.mean(z, -1, keepdims=True)
    d = z - mu
    var = jnp.mean(d * d, -1, keepdims=True)
    return d * lax.rsqrt(var + EPS) * g + b


def _const_spec(shape):
    nd = len(shape)
    return pl.BlockSpec(shape, lambda *_: (0,) * nd)


def _key_to_float(k):
    return lax.bitcast_convert_type(jnp.where(k < 0, k ^ jnp.int32(0x7FFFFFFF), k), F32)


def _kth_largest(count, shape, ksel, bits=1, n_finite=None, n_nonneg=None):
    if n_finite is None:
        n_finite, = count([jnp.full(shape, -jnp.inf, F32)], True)
    if n_nonneg is None:
        n_nonneg, = count([jnp.zeros(shape, F32)], False)
    nonneg = n_nonneg >= ksel
    start = (jnp.where(nonneg, jnp.int32(0), jnp.int32(INT_MIN)), jnp.where(nonneg, n_nonneg, MANY),
             jnp.where(nonneg, 0.0, n_nonneg))

    def refine(carry, shift, nbits):
        k, n_ge, n_above = carry
        cands = [k | lax.shift_left(jnp.int32(d), shift) for d in range(1, 2 ** nbits)]
        prev_ok = None
        for cand, n_cand in zip(cands, count([_key_to_float(c) for c in cands], False)):
            ok = n_cand >= ksel
            k, n_ge = jnp.where(ok, cand, k), jnp.where(ok, n_cand, n_ge)
            first_miss = jnp.logical_not(ok) if prev_ok is None else jnp.logical_and(prev_ok, jnp.logical_not(ok))
            n_above = jnp.where(first_miss, n_cand, n_above)
            prev_ok = ok
        return k, n_ge, n_above

    npass, rest = divmod(31, bits)
    carry = lax.fori_loop(0, npass, lambda it, cr: refine(cr, jnp.int32(31 - bits) - bits * it, bits), start)
    if rest:
        carry = refine(carry, jnp.int32(0), rest)
    k, n_ge, n_above = carry
    short = n_finite < ksel
    t = jnp.where(short, FLT_LOWEST, _key_to_float(k))
    need = jnp.where(short, MANY, ksel - n_above)
    tied = jnp.where(short, 0.0, jnp.where(n_ge > ksel, 1.0, 0.0))
    return t, need, tied


def _proj_even_kernel(x_ref, wa_ref, wb_ref, wq_ref, ww_ref, qs_ref,
                      na_ref, nb_ref, abf_ref, bbf_ref, q_ref, wi_ref):
    x = x_ref[...].astype(BF)
    a = _nn(x, wa_ref[...])
    na_ref[...] = a
    abf_ref[...] = a.astype(BF)
    b = _nn(x, wb_ref[...])
    nb_ref[...] = b
    bbf_ref[...] = b.astype(BF)
    q_ref[...] = (_nn(x, wq_ref[...]) * qs_ref[...]).astype(BF)
    wi_ref[...] = _nn(x, ww_ref[...]) * (HI ** -0.5)


def _proj_even(x, wa, wb, wq, ww, qscale, tm=ROW_TILE):
    n = x.shape[0]
    tm = min(tm, n)
    row = lambda w: pl.BlockSpec((tm, w), lambda i: (i, 0))
    return pl.pallas_call(
        _proj_even_kernel,
        grid=(n // tm,),
        in_specs=[row(D_MODEL), _const_spec(wa.shape), _const_spec(wb.shape),
                  _const_spec(wq.shape), _const_spec(ww.shape), _const_spec(qscale.shape)],
        out_specs=[row(1024), row(320), row(1024), row(320), row(1536), row(128)],
        out_shape=[jax.ShapeDtypeStruct((n, 1024), F32), jax.ShapeDtypeStruct((n, 320), F32),
                   jax.ShapeDtypeStruct((n, 1024), BF), jax.ShapeDtypeStruct((n, 320), BF),
                   jax.ShapeDtypeStruct((n, 1536), BF), jax.ShapeDtypeStruct((n, 128), F32)],
        compiler_params=_params(1),
        name="proj_even",
    )(x, wa, wb, wq, ww, qscale)


def _proj_even_t_kernel(x_ref, wa_ref, wb_ref, wqt_ref, wvat_ref, wvbt_ref, wwt_ref,
                        na_ref, nb_ref, abf_ref, bbf_ref, qt_ref, vat_ref, vbt_ref, wit_ref):
    x = x_ref[...]
    xb = x.astype(BF)
    xt = x.T.astype(BF)
    a = _nn(xb, wa_ref[...])
    tm = a.shape[0]
    for h in range(H_A):
        for part in range(2):
            na_ref[pl.ds(H_A * part + h, tm, stride=2 * H_A), :] = a[:, h * 256 + part * 128:h * 256 + (part + 1) * 128]
    abf_ref[...] = a.astype(BF)
    b = _nn(xb, wb_ref[...])
    nb_ref[...] = b
    bbf_ref[...] = b.astype(BF)
    qt = _nn(wqt_ref[...], xt)
    qt_ref[0:512, :] = (qt[0:512] * (DA ** -0.5 * LOG2E)).astype(BF)
    qt_ref[512:1024, :] = (qt[512:1024] * (DB ** -0.5 * LOG2E)).astype(BF)
    qt_ref[1024:1536, :] = (qt[1024:1536] * (DI ** -0.5)).astype(BF)
    pad = V_ROWS - 128
    ones_rows = jnp.where(lax.broadcasted_iota(jnp.int32, (pad, tm), 0) == 0, 1.0, 0.0).astype(BF)
    va = _nn(wvat_ref[...], xt).astype(BF)
    for h in range(H_A):
        vat_ref[h * V_ROWS:h * V_ROWS + 128, :] = va[h * 128:(h + 1) * 128]
        vat_ref[h * V_ROWS + 128:(h + 1) * V_ROWS, :] = ones_rows
    vbt_ref[0:128, :] = _nn(wvbt_ref[...], xt).astype(BF)
    vbt_ref[128:V_ROWS, :] = ones_rows
    wit_ref[...] = _nn(wwt_ref[...], xt) * (HI ** -0.5)


def _proj_even_t(x, wa, wb, wqt, wvat, wvbt, wwt, tm=ROW_TILE):
    n = x.shape[0]
    row = lambda w: pl.BlockSpec((tm, w), lambda i: (i, 0))
    col = lambda h: pl.BlockSpec((h, tm), lambda i: (0, i))
    return pl.pallas_call(
        _proj_even_t_kernel,
        grid=(n // tm,),
        in_specs=[row(D_MODEL)] + [_const_spec(w.shape) for w in (wa, wb, wqt, wvat, wvbt, wwt)],
        out_specs=[pl.BlockSpec((2 * H_A * tm, 128), lambda i: (i, 0)), row(320), row(1024), row(320),
                   col(1536), col(H_A * V_ROWS), col(V_ROWS), col(16)],
        out_shape=[jax.ShapeDtypeStruct((2 * H_A * n, 128), F32), jax.ShapeDtypeStruct((n, 320), F32),
                   jax.ShapeDtypeStruct((n, 1024), BF), jax.ShapeDtypeStruct((n, 320), BF),
                   jax.ShapeDtypeStruct((1536, n), BF), jax.ShapeDtypeStruct((H_A * V_ROWS, n), BF),
                   jax.ShapeDtypeStruct((V_ROWS, n), BF), jax.ShapeDtypeStruct((16, n), F32)],
        compiler_params=_params(1),
        name="proj_even_t",
    )(x, wa, wb, wqt, wvat, wvbt, wwt)


def _lambda(lam_ref, lam_init):
    lp = lam_ref[...]
    return (jnp.exp(jnp.sum(lp[0:1] * lp[1:2], axis=-1, keepdims=True))
            - jnp.exp(jnp.sum(lp[2:3] * lp[3:4], axis=-1, keepdims=True)) + lam_init)


def _split_q12(q):
    qf = q.astype(F32)
    lane = lax.broadcasted_iota(jnp.int32, qf.shape, 1)
    return jnp.concatenate([jnp.where(lane < DA, qf, 0.0), jnp.where(lane >= DA, qf, 0.0)],
                           axis=0).astype(BF)


def _subln(o, lam, g, lam_init):
    r = o.shape[0] // 2
    d = o[:r] - lam * o[r:]
    ms = jnp.mean(d * d, -1, keepdims=True)
    return d * lax.rsqrt(ms + EPS) * g * (1.0 - lam_init)


def _kth_largest_rows(sc_ref, nch, tk, ksel):
    rows = sc_ref.shape[0]

    def count(cands, strict):
        accs = [jnp.zeros((rows, tk), F32) for _ in cands]
        for c in range(nch):
            x = sc_ref[:, c * tk:(c + 1) * tk]
            accs = [a + jnp.where((x > cand) if strict else (x >= cand), 1.0, 0.0)
                    for a, cand in zip(accs, cands)]
        return [jnp.sum(a, -1, keepdims=True) for a in accs]

    n_finite, n_nonneg = count([jnp.full((rows, 1), FLT_LOWEST, F32), jnp.zeros((rows, 1), F32)], False)
    return _kth_largest(count, (rows, 1), ksel, 3, n_finite, n_nonneg)


def _kth_largest_cols(sc_ref, nch, tk, ksel, n_finite=None, n_nonneg=None):
    cols = sc_ref.shape[1]

    def count(cands, strict):
        cand, = cands

        def body(c, acc):
            x = sc_ref[pl.ds(pl.multiple_of(c * tk, tk), tk), :]
            hit = (x > cand) if strict else (x >= cand)
            return acc + jnp.sum(jnp.where(hit, 1.0, 0.0).reshape(tk // 32, 32, cols), axis=0)
        acc = lax.fori_loop(0, nch, body, jnp.zeros((32, cols), F32))
        return [jnp.sum(acc, 0, keepdims=True)]

    return _kth_largest(count, (1, cols), ksel, 1, n_finite, n_nonneg)


def _selected(x, t, need, eq_before, tri):
    eq = jnp.where(x == t, 1.0, 0.0)
    rank = eq_before + _nn(eq.astype(BF), tri)
    tie_taken = jnp.where(rank <= need, eq, 0.0)
    return jnp.where(x > t, 1.0, tie_taken), jnp.sum(eq, -1, keepdims=True)


def _tri(n, lower):
    r = lax.broadcasted_iota(jnp.int32, (n, n), 0)
    c = lax.broadcasted_iota(jnp.int32, (n, n), 1)
    return jnp.where((r >= c) if lower else (r <= c), 1.0, 0.0).astype(BF)


def _flash_key_major(npair, nlast, nchain, tq, qk, mask, v_t, s_ref, p_ref, al_ref, acc_ref):
    def scores(c, slot):
        for j in range(nchain):
            s_ref[slot, j] = qk(c, j)

    def softmax(c, slot, ms, last):
        ms_new = []
        for j in range(nchain):
            s = mask(c, j, s_ref[slot, j], last)
            m_new = jnp.maximum(ms[j], jnp.max(s, 0, keepdims=True))
            p_ref[slot, j] = jnp.exp2(s - m_new).astype(BF)
            al_ref[slot, j] = jnp.exp2(ms[j] - m_new)
            ms_new.append(m_new)
        return tuple(ms_new)

    def fold(c, slot):
        for j in range(nchain):
            acc_ref[j] = al_ref[slot, j] * acc_ref[j] + _nn(v_t(c, j), p_ref[slot, j])

    acc_ref[...] = jnp.zeros(acc_ref.shape, F32)
    p_ref[1] = jnp.zeros(p_ref.shape[1:], BF)
    al_ref[1] = jnp.ones(al_ref.shape[1:], F32)
    scores(0, 0)

    def pair(c, carry, last, final):
        fold(jnp.maximum(c - 1, 0), 1)
        carry = softmax(c, 0, carry, last)
        scores(c + 1, 1)
        fold(c, 0)
        carry = softmax(c + 1, 1, carry, last)
        if not final:
            scores(c + 2, 0)
        return carry

    n_plain = npair - nlast
    carry = lax.fori_loop(0, n_plain, lambda c, cr: pair(2 * c, cr, False, False),
                          tuple(jnp.full((1, tq), NEG, F32) for _ in range(nchain)))
    for d in range(nlast):
        carry = pair(2 * (n_plain + d), carry, True, d == nlast - 1)
    fold(2 * npair - 1, 1)


def _diff_kernel(lam_ref, g_ref, qt_ref, a_ref, vt_ref, o_ref, qh_ref, s_ref, p_ref, al_ref, acc_ref,
                 *, tq, tk, lam_init):
    i = pl.program_id(1)
    lam = _lambda(lam_ref, lam_init)
    sub = lax.broadcasted_iota(jnp.int32, (128, tq), 0)
    key = lax.broadcasted_iota(jnp.int32, (tk, 2 * tq), 0)
    qry = i * tq + lax.broadcasted_iota(jnp.int32, (tk, 2 * tq), 1) % tq
    for h in range(H_A):
        qt = qt_ref[h * 128:(h + 1) * 128, :].astype(F32)
        qh_ref[h, :, 0:tq] = jnp.where(sub < DA, qt, 0.0).astype(BF)
        qh_ref[h, :, tq:2 * tq] = jnp.where(sub >= DA, qt, 0.0).astype(BF)

    def qk(c, h):
        return _nn(a_ref[pl.ds(pl.multiple_of(c * tk, tk), tk), h * 256:h * 256 + 128], qh_ref[h])

    def mask(c, h, s, last):
        return jnp.where(c * tk + key <= qry, s, NEG) if last else s

    def v_t(c, h):
        return vt_ref[h * V_ROWS:(h + 1) * V_ROWS, pl.ds(pl.multiple_of(c * tk, tk), tk)]

    nlast = tq // (2 * tk)
    _flash_key_major((i + 1) * nlast, nlast, H_A, 2 * tq, qk, mask, v_t, s_ref, p_ref, al_ref, acc_ref)
    for h in range(H_A):
        o12 = acc_ref[h, 0:DV_A, :] * (1.0 / acc_ref[h, DV_A:DV_A + 1, :])
        d = o12[:, 0:tq] - lam * o12[:, tq:2 * tq]
        ms = jnp.mean(d * d, 0, keepdims=True)
        o = d * lax.rsqrt(ms + EPS) * g_ref[...] * (1.0 - lam_init)
        o_ref[:, h * 128:(h + 1) * 128] = o.T.astype(o_ref.dtype)


def _flash_scratch(nchain, feat, tk, tq):
    return [pltpu.VMEM((2, nchain, tk, tq), F32), pltpu.VMEM((2, nchain, tk, tq), BF),
            pltpu.VMEM((2, nchain, 1, tq), F32), pltpu.VMEM((nchain, feat, tq), F32)]


def _diff_attn_prompt(lam_e, g_col, qt, a_bf, vat, batch, seq, lam_init, tq=DIFF_QUERY_TILE, tk=DIFF_KEY_TILE):
    nq = seq // tq
    return pl.pallas_call(
        functools.partial(_diff_kernel, tq=tq, tk=tk, lam_init=lam_init),
        grid=(batch, nq),
        in_specs=[_const_spec(lam_e.shape), _const_spec(g_col.shape),
                  pl.BlockSpec((512, tq), lambda b, i: (0, b * nq + i)),
                  pl.BlockSpec((seq, 1024), lambda b, i: (b, 0)),
                  pl.BlockSpec((H_A * V_ROWS, seq), lambda b, i: (0, b))],
        out_specs=pl.BlockSpec((tq, 512), lambda b, i: (b * nq + i, 0)),
        out_shape=jax.ShapeDtypeStruct((batch * seq, 512), BF),
        scratch_shapes=[pltpu.VMEM((H_A, 128, 2 * tq), BF)] + _flash_scratch(H_A, V_ROWS, tk, 2 * tq),
        compiler_params=_params(2),
        name="diff_attn_prompt",
    )(lam_e, g_col, qt, a_bf, vat)


def _attn_chunk(tq, tk):
    return tk if (tq // tk) % 2 == 0 else tk // 2


def _sparse_kernel(qbt_ref, qit_ref, wit_ref, b_ref, vbt_ref, o_ref, sc_ref, qiw_ref, qbw_ref,
                   s_ref, p_ref, al_ref, acc_ref, *, tq, tk, ksel):
    i = pl.program_id(1)
    nch = (i + 1) * (tq // tk)
    ta = tk // 2
    key = lax.broadcasted_iota(jnp.int32, (ta, tq), 0)
    qry = i * tq + lax.broadcasted_iota(jnp.int32, (ta, tq), 1)
    w = wit_ref[...]
    for h in range(HI):
        qiw_ref[:, h * tq:(h + 1) * tq] = qit_ref[h * DI:(h + 1) * DI, :]
    for h in range(H_B):
        qbw_ref[:, h * tq:(h + 1) * tq] = qbt_ref[h * DB:(h + 1) * DB, :]
    heads = lambda x: jnp.concatenate([x] * H_B, axis=1)

    def score_chunk(c, counts):
        n_finite, n_nonneg = counts
        for u in range(2):
            r0 = pl.multiple_of(c * tk + u * ta, ta)
            d = jnp.maximum(_nn(b_ref[pl.ds(r0, ta), 2 * DB:2 * DB + DI], qiw_ref[...]), 0.0)
            acc = w[0:1, :] * d[:, 0:tq]
            for h in range(1, HI):
                acc = acc + w[h:h + 1, :] * d[:, h * tq:(h + 1) * tq]
            sc = jnp.where(r0 + key <= qry, acc, -jnp.inf)
            sc_ref[pl.ds(r0, ta), :] = sc
            n_finite = n_finite + jnp.sum(jnp.where(sc > -jnp.inf, 1.0, 0.0), 0, keepdims=True)
            n_nonneg = n_nonneg + jnp.sum(jnp.where(sc >= 0.0, 1.0, 0.0), 0, keepdims=True)
        return n_finite, n_nonneg

    zero = jnp.zeros((1, tq), F32)
    n_finite, n_nonneg = lax.fori_loop(0, nch, score_chunk, (zero, zero))
    t, need, tied = _kth_largest_cols(sc_ref, nch, tk, ksel, n_finite, n_nonneg)

    tv = _attn_chunk(tq, tk)

    def qk(c, _):
        return _nn(b_ref[pl.ds(pl.multiple_of(c * tv, tv), tv), 0:DB], qbw_ref[...])

    def v_t(c, _):
        return vbt_ref[:, pl.ds(pl.multiple_of(c * tv, tv), tv)]

    def to_bias(exact_ties):
        def body(c, eq_before):
            for u in range(2):
                rows = pl.ds(pl.multiple_of(c * tk + u * ta, ta), ta)
                x = sc_ref[rows, :]
                if exact_ties:
                    eq = jnp.where(x == t, 1.0, 0.0)
                    rank = eq_before + _nn(_tri(ta, True), eq.astype(BF))
                    sc_ref[rows, :] = jnp.where(
                        x > t, 0.0, jnp.where(x == t, jnp.where(rank <= need, 0.0, NEG), NEG))
                    eq_before = eq_before + jnp.sum(eq, 0, keepdims=True)
                else:
                    sc_ref[rows, :] = jnp.where(x >= t, 0.0, NEG)
            return eq_before
        return lax.fori_loop(0, nch, body, jnp.zeros((1, tq), F32))

    lax.cond(jnp.max(tied) > 0.0, lambda: to_bias(True), lambda: to_bias(False))

    def add_bias(c, _, s, last):
        return s + heads(sc_ref[pl.ds(pl.multiple_of(c * tv, tv), tv), :])

    _flash_key_major(nch * tk // (2 * tv), 1, 1, H_B * tq, qk, add_bias, v_t, s_ref, p_ref, al_ref, acc_ref)
    o = acc_ref[0, 0:DB, :] * (1.0 / acc_ref[0, DB:DB + 1, :])
    for h in range(H_B):
        o_ref[:, h * DB:(h + 1) * DB] = o[:, h * tq:(h + 1) * tq].T.astype(o_ref.dtype)


def _sparse_attn_prompt(qt, wit, b_bf, vbt, batch, seq, ksel, tq=SPARSE_QUERY_TILE, tk=SPARSE_KEY_TILE):
    nq = seq // tq
    return pl.pallas_call(
        functools.partial(_sparse_kernel, tq=tq, tk=tk, ksel=ksel),
        grid=(batch, nq),
        in_specs=[pl.BlockSpec((512, tq), lambda b, i: (1, b * nq + i)),
                  pl.BlockSpec((512, tq), lambda b, i: (2, b * nq + i)),
                  pl.BlockSpec((16, tq), lambda b, i: (0, b * nq + i)),
                  pl.BlockSpec((seq, 320), lambda b, i: (b, 0)),
                  pl.BlockSpec((V_ROWS, seq), lambda b, i: (0, b))],
        out_specs=pl.BlockSpec((tq, 512), lambda b, i: (b * nq + i, 0)),
        out_shape=jax.ShapeDtypeStruct((batch * seq, 512), BF),
        scratch_shapes=[pltpu.VMEM((seq, tq), F32), pltpu.VMEM((DI, HI * tq), BF),
                        pltpu.VMEM((DB, H_B * tq), BF)] + _flash_scratch(1, V_ROWS, _attn_chunk(tq, tk), H_B * tq),
        compiler_params=_params(2),
        name="sparse_attn_prompt",
    )(qt, qt, wit, b_bf, vbt)


def _sample_even_kernel(pt_ref, lam_ref, g_ref, q_ref, wi_ref, *rest, npg, page, ksel, lam_init):
    del pt_ref
    nblk = npg + 1
    a_pages = rest[:nblk]
    b_pages = rest[nblk:2 * nblk]
    o_ref = rest[2 * nblk]
    sc_ref = rest[2 * nblk + 1]
    r = q_ref.shape[0]
    lam = _lambda(lam_ref, lam_init)
    q = q_ref[...]

    def a_rows(p, j):
        n_pos = a_pages[p].shape[0] // (2 * H_A)
        x = a_pages[p][pl.ds(j, n_pos, stride=2 * H_A), :]
        if n_pos < page:
            x = jnp.concatenate([x, jnp.zeros((page - n_pos, 128), F32)], axis=0)
        return x.astype(BF)

    def new_ok(rows):
        tok = lax.broadcasted_iota(jnp.int32, (rows, page), 0) % r
        return lax.broadcasted_iota(jnp.int32, (rows, page), 1) <= tok

    ok2 = new_ok(2 * r)
    for h in range(H_A):
        qq = _split_q12(q[:, h * 128:(h + 1) * 128])
        ss = [_nt(qq, a_rows(p, h)) for p in range(nblk)]
        ss[npg] = jnp.where(ok2, ss[npg], NEG)
        m = functools.reduce(jnp.maximum, ss)
        m = jnp.max(m, -1, keepdims=True)
        ps = [jnp.exp(s - m) for s in ss]
        l = jnp.sum(functools.reduce(lambda x, y: x + y, ps), -1, keepdims=True)
        acc = functools.reduce(lambda x, y: x + y, [
            _nn(ps[p].astype(BF), a_rows(p, H_A + h))
            for p in range(nblk)])
        o_ref[:, h * 128:(h + 1) * 128] = _subln(acc / l, lam, g_ref[...], lam_init)

    qi = jnp.concatenate([q[:, 1024 + h * DI:1024 + (h + 1) * DI] for h in range(HI)], axis=0).astype(BF)
    w = wi_ref[...]
    wcol = jnp.concatenate([w[:, h:h + 1] for h in range(HI)], axis=0)
    ok1 = new_ok(r)
    for p in range(nblk):
        d = jnp.maximum(_nn(qi, b_pages[p][2 * DB:2 * DB + DI, :].astype(BF)), 0.0) * wcol
        sc = d[0:r]
        for h in range(1, HI):
            sc = sc + d[h * r:(h + 1) * r]
        if p == npg:
            sc = jnp.where(ok1, sc, -jnp.inf)
        sc_ref[:, p * page:(p + 1) * page] = sc
    t, need, _ = _kth_largest_rows(sc_ref, nblk, page, ksel)

    tri = _tri(page, False)
    q4 = jnp.concatenate([q[:, 512 + h * DB:512 + (h + 1) * DB] for h in range(H_B)], axis=0).astype(BF)
    eq_before = jnp.zeros((r, 1), F32)
    ss, sels = [], []
    for p in range(nblk):
        sel, n_eq = _selected(sc_ref[:, p * page:(p + 1) * page], t, need, eq_before, tri)
        eq_before = eq_before + n_eq
        if p == npg:
            sel = jnp.where(ok1, sel, 0.0)
        sels.append(jnp.concatenate([sel] * H_B, axis=0) > 0.5)
        ss.append(_nn(q4, b_pages[p][0:DB, :].astype(BF)))
    m = functools.reduce(jnp.maximum, [jnp.where(sl, s, NEG) for sl, s in zip(sels, ss)])
    m = jnp.max(m, -1, keepdims=True)
    ps = [jnp.where(sl, jnp.exp(s - m), 0.0) for sl, s in zip(sels, ss)]
    l = jnp.sum(functools.reduce(lambda x, y: x + y, ps), -1, keepdims=True)
    acc = functools.reduce(lambda x, y: x + y, [
        _nt(ps[p].astype(BF), b_pages[p][DB:2 * DB, :].astype(BF)) for p in range(nblk)])
    o = acc / l
    for h in range(H_B):
        o_ref[:, 512 + h * DB:512 + (h + 1) * DB] = o[h * r:(h + 1) * r]


def _sample_even(page_table, lam_e, g, qs, wis, anew_pg, bnew_pg, cache_a_pg, cache_b_pg, ksel, lam_init):
    nreq, npg = page_table.shape
    page = cache_b_pg.shape[2]
    r = qs.shape[1]
    req = lambda a: pl.BlockSpec((None,) + a.shape[1:], lambda i, pt: (i, 0, 0))

    def page_spec(a, p):
        return pl.BlockSpec((None,) + a.shape[1:], lambda i, pt, p=p: (pt[i, p], 0, 0))

    in_specs = [_const_spec(lam_e.shape), _const_spec(g.shape), req(qs), req(wis)]
    in_specs += [page_spec(cache_a_pg, p) for p in range(npg)] + [req(anew_pg)]
    in_specs += [page_spec(cache_b_pg, p) for p in range(npg)] + [req(bnew_pg)]
    grid_spec = pltpu.PrefetchScalarGridSpec(
        num_scalar_prefetch=1, grid=(nreq,), in_specs=in_specs,
        out_specs=pl.BlockSpec((None, r, 1024), lambda i, pt: (i, 0, 0)),
        scratch_shapes=[pltpu.VMEM((r, (npg + 1) * page), F32)])
    return pl.pallas_call(
        functools.partial(_sample_even_kernel, npg=npg, page=page, ksel=ksel, lam_init=lam_init),
        grid_spec=grid_spec,
        out_shape=jax.ShapeDtypeStruct((nreq, r, 1024), F32),
        compiler_params=_params(1),
        name="sample_even",
    )(page_table, lam_e, g, qs, wis, *([cache_a_pg] * npg), anew_pg, *([cache_b_pg] * npg), bnew_pg)


def _layer_tail_kernel(*refs, n_lhs, ck):
    lhs = refs[:n_lhs]
    wo_ref, x_ref, g0_ref, b0_ref, w1_ref, w2_ref, g1_ref, b1_ref, o_ref, acc_ref = refs[n_lhs:]
    y = None
    k0 = 0
    for a_ref in lhs:
        kw = a_ref.shape[1]
        part = _nn(a_ref[...].astype(BF), wo_ref[k0:k0 + kw, :])
        y = part if y is None else y + part
        k0 += kw
    x1 = _ln(ALPHA * x_ref[...] + y, g0_ref[...], b0_ref[...])
    xb = x1.astype(BF)
    for c in range(D_FF // ck):
        h = jnp.maximum(_nn(xb, w1_ref[:, c * ck:(c + 1) * ck]), 0.0)
        part = _nn((h * h).astype(BF), w2_ref[c * ck:(c + 1) * ck, :])
        if c == 0:
            acc_ref[...] = part
        else:
            acc_ref[...] += part
    o_ref[...] = _ln(ALPHA * x1 + acc_ref[...], g1_ref[...], b1_ref[...])


def _layer_tail(lhs, w_out, x, g0, b0, w1, w2, g1, b1, tm=ROW_TILE, ck=FF_CHUNK):
    n = x.shape[0]
    tm = min(tm, n)
    row = lambda wd: pl.BlockSpec((tm, wd), lambda i: (i, 0))
    resident = lambda s: pl.BlockSpec(s, lambda i: (0, 0), pipeline_mode=pl.Buffered(1))
    vec = _const_spec(g0.shape)
    return pl.pallas_call(
        functools.partial(_layer_tail_kernel, n_lhs=len(lhs), ck=ck),
        grid=(n // tm,),
        in_specs=[row(a.shape[1]) for a in lhs] + [resident(w_out.shape), row(D_MODEL), vec, vec,
                                                   resident(w1.shape), resident(w2.shape), vec, vec],
        out_specs=row(D_MODEL),
        out_shape=jax.ShapeDtypeStruct((n, D_MODEL), F32),
        scratch_shapes=[pltpu.VMEM((tm, D_MODEL), F32)],
        compiler_params=_params(1),
        name="layer_tail",
    )(*lhs, w_out, x, g0, b0, w1, w2, g1, b1)


def _gelu(x):
    return 0.5 * x * (1.0 + jnp.tanh(math.sqrt(2.0 / math.pi) * (x + 0.044715 * (x * x * x))))


def _proj_odd_kernel(x_ref, w_ref, g_ref, b_ref, xc_ref, u_ref, vn_ref):
    h = _nn(x_ref[...].astype(BF), w_ref[...])
    xc_ref[...] = h[:, :MIX_C]
    u_ref[...] = _gelu(h[:, MIX_C:MIX_C + MIX_D])
    vn_ref[...] = _ln(_gelu(h[:, MIX_C + MIX_D:]), g_ref[...], b_ref[...])


def _proj_odd(x, w, g, b, tm=ROW_TILE):
    n = x.shape[0]
    tm = min(tm, n)
    row = lambda wd: pl.BlockSpec((tm, wd), lambda i: (i, 0))
    return pl.pallas_call(
        _proj_odd_kernel,
        grid=(n // tm,),
        in_specs=[row(D_MODEL), _const_spec(w.shape), _const_spec(g.shape), _const_spec(b.shape)],
        out_specs=[row(512), row(512), row(512)],
        out_shape=[jax.ShapeDtypeStruct((n, 512), F32)] * 3,
        compiler_params=_params(1),
        name="proj_odd",
    )(x, w, g, b)


def _odd_mixer_kernel(prev_ref, xh_ref, x_ref, w_ref, g_ref, b_ref, wp_ref, sc_ref, ws_ref, bs_ref,
                      o_ref, xc_ref, ext_ref, *, start):
    t = pl.program_id(1)
    hal = prev_ref.shape[0]
    rows = x_ref.shape[0]
    h = _nn(x_ref[...].astype(BF), w_ref[...])
    xc = h[:, :MIX_C]
    u = _gelu(h[:, MIX_C:MIX_C + MIX_D])
    vn = _ln(_gelu(h[:, MIX_C + MIX_D:]), g_ref[...], b_ref[...])
    xc_ref[...] = xc
    halo = _nn(xh_ref[...].astype(BF), w_ref[:, :MIX_C])
    ext_ref[0:hal, :] = jnp.where(t == 0, prev_ref[...], halo)
    ext_ref[hal:hal + rows, :] = xc
    pos = start + t * rows + lax.broadcasted_iota(jnp.int32, (rows, 1), 0)
    for g, w in enumerate(POOL_WINDOWS):
        gs = slice(g * C_GROUP, (g + 1) * C_GROUP)
        acc = ext_ref[hal:hal + rows, gs]
        for s in range(1, w):
            acc = acc + ext_ref[hal - s:hal - s + rows, gs]
        cnt = jnp.minimum(w, pos + 1).astype(F32)
        pooled = acc / cnt - ext_ref[hal:hal + rows, gs]
        c = _nn(pooled.astype(BF), wp_ref[g]) * sc_ref[:, gs]
        o_ref[:, gs] = c.astype(o_ref.dtype)
    r = lax.broadcasted_iota(jnp.int32, (CHUNK, CHUNK), 0)
    cc = lax.broadcasted_iota(jnp.int32, (CHUNK, CHUNK), 1)
    for g in range(D_GROUPS):
        gs = slice(g * 128, (g + 1) * 128)
        ws = jnp.where(r >= cc, ws_ref[g], 0.0).astype(BF)
        for k in range(rows // CHUNK):
            ks = slice(k * CHUNK, (k + 1) * CHUNK)
            s = _nn(ws, vn[ks, gs].astype(BF)) + bs_ref[:, g:g + 1]
            o_ref[ks, MIX_C + g * 128:MIX_C + (g + 1) * 128] = (u[ks, gs] * s).astype(o_ref.dtype)


def _odd_mixer_prompt(prev16, x, w_in, sg, sb, wp, scale, ws, bs_t, batch, seq, start, rows=ROW_TILE):
    rows = min(rows, seq)
    nt = seq // rows
    hal = prev16.shape[1]
    per = rows // hal
    row = lambda wd: pl.BlockSpec((rows, wd), lambda b, t: (b * nt + t, 0))
    return pl.pallas_call(
        functools.partial(_odd_mixer_kernel, start=start),
        grid=(batch, nt),
        in_specs=[pl.BlockSpec((None, hal, MIX_C), lambda b, t: (b, 0, 0)),
                  pl.BlockSpec((hal, D_MODEL), lambda b, t: (jnp.maximum((b * nt + t) * per - 1, 0), 0)),
                  row(D_MODEL), _const_spec(w_in.shape), _const_spec(sg.shape), _const_spec(sb.shape),
                  _const_spec(wp.shape), _const_spec(scale.shape), _const_spec(ws.shape),
                  _const_spec(bs_t.shape)],
        out_specs=[row(1024), row(MIX_C)],
        out_shape=[jax.ShapeDtypeStruct((batch * seq, 1024), BF),
                   jax.ShapeDtypeStruct((batch * seq, MIX_C), F32)],
        scratch_shapes=[pltpu.VMEM((hal + rows, MIX_C), F32)],
        compiler_params=_params(2),
        name="odd_mixer_prompt",
    )(prev16, x, x, w_in, sg, sb, wp, scale, ws, bs_t)


def _pool_sgu_sample_kernel(ws_ref, bs_ref, prev_ref, xc_ref, u_ref, vn_ref, wp_ref, sc_ref, o_ref,
                            *, start):
    nprev = prev_ref.shape[0]
    ntok = xc_ref.shape[0]
    for t in range(ntok):
        for g, w in enumerate(POOL_WINDOWS):
            gs = slice(g * C_GROUP, (g + 1) * C_GROUP)
            acc = None
            for s in range(w):
                j = nprev + t - s
                slab = prev_ref[j, :, gs] if j < nprev else xc_ref[j - nprev, :, gs]
                acc = slab if acc is None else acc + slab
            cnt = float(min(w, start + t + 1))
            pooled = acc / cnt - xc_ref[t, :, gs]
            o_ref[t, :, gs] = _nn(pooled.astype(BF), wp_ref[g]) * sc_ref[:, gs]
        for g in range(D_GROUPS):
            gs = slice(g * 128, (g + 1) * 128)
            s = None
            for j in range(t + 1):
                term = ws_ref[(g * ntok + t) * ntok + j] * vn_ref[j, :, gs]
                s = term if s is None else s + term
            s = s + bs_ref[g * ntok + t]
            o_ref[t, :, MIX_C + g * 128:MIX_C + (g + 1) * 128] = u_ref[t, :, gs] * s


def _pool_sgu_sample(ws_small, bs_small, prev_t, xc_t, u_t, vn_t, wp, scale, start):
    ntok, nreq, _ = xc_t.shape
    smem = pl.BlockSpec(memory_space=pltpu.SMEM)
    return pl.pallas_call(
        functools.partial(_pool_sgu_sample_kernel, start=start),
        grid=(1,),
        in_specs=[smem, smem, _const_spec(prev_t.shape), _const_spec(xc_t.shape), _const_spec(u_t.shape),
                  _const_spec(vn_t.shape), _const_spec(wp.shape), _const_spec(scale.shape)],
        out_specs=_const_spec((ntok, nreq, 1024)),
        out_shape=jax.ShapeDtypeStruct((ntok, nreq, 1024), F32),
        compiler_params=_params(1),
        name="pool_sgu_sample",
    )(ws_small, bs_small, prev_t, xc_t, u_t, vn_t, wp, scale)


def _even_weights(w_in):
    q_a, k_a, v_a, q_b, k_b, v_b, q_i, k_i, w_i = jnp.split(
        w_in, [512, 1024, 1536, 2048, 2176, 2304, 2816, 2880], axis=1)
    wa = jnp.concatenate([k_a.reshape(D_MODEL, H_A, 2 * DA), v_a.reshape(D_MODEL, H_A, DV_A)],
                         -1).reshape(D_MODEL, H_A * (2 * DA + DV_A))
    wb = jnp.concatenate([k_b, v_b, k_i], 1)
    wq = jnp.concatenate([q_a, q_b, q_i], 1)
    ww = jnp.concatenate([w_i, jnp.zeros((D_MODEL, 128 - HI), w_in.dtype)], 1)
    qscale = jnp.concatenate([jnp.full((512,), DA ** -0.5, F32), jnp.full((512,), DB ** -0.5, F32),
                              jnp.full((512,), DI ** -0.5, F32)]).reshape(1, 1536)
    wwt = jnp.concatenate([w_i.T, jnp.zeros((16 - HI, D_MODEL), w_in.dtype)], 0)
    natural = (wa.astype(BF), wb.astype(BF), wq.astype(BF), ww.astype(BF), qscale)
    feature_major = (wq.T.astype(BF), v_a.T.astype(BF), v_b.T.astype(BF), wwt.astype(BF))
    return natural, feature_major


def _a_pages(x):
    n, page, _ = x.shape
    return x.reshape(n, page, H_A, 2, 128).transpose(0, 1, 3, 2, 4).reshape(n, page * 2 * H_A, 128)


def _pad_rows(x, rows):
    return jnp.pad(x, ((0, 0), (0, rows - x.shape[1]), (0, 0)))


def kernel(x_prompt, x_sample, cache_a, cache_b, state_pool, page_table, w_in_e, lam_e, subln_g, w_out_e,
           w_in_o, w_pool, pool_scale, sgu_g, sgu_b, w_s, b_s, w_out_o, w_mlp1, w_mlp2, ln_g, ln_b):
    batch, seq, _ = x_prompt.shape
    nreq, ntok, _ = x_sample.shape
    npg = page_table.shape[1]
    page = cache_a.shape[2]
    past = npg * page
    xp = x_prompt.reshape(batch * seq, D_MODEL)
    xs = x_sample.reshape(nreq * ntok, D_MODEL)
    outs = {k: [] for k in ("a_p", "b_p", "pool_p", "a_s", "b_s", "pool_s", "v_s")}
    rpad = 8

    for l in range(DEPTH):
        i = l // 2
        row2 = lambda v: v.reshape(1, -1)
        if l % 2 == 0:
            lam_init = 0.8 - 0.6 * math.exp(-0.3 * l)
            (wa, wb, wq, ww, qscale), (wqt, wvat, wvbt, wwt) = _even_weights(w_in_e[i])
            g = row2(subln_g[i])
            w_out = w_out_e[i].astype(BF)
            na, nb, abf, bbf, qt, vat, vbt, wit = _proj_even_t(xp, wa, wb, wqt, wvat, wvbt, wwt)
            o_a = _diff_attn_prompt(lam_e[i], subln_g[i].reshape(-1, 1), qt, abf, vat, batch, seq, lam_init)
            o_b = _sparse_attn_prompt(qt, wit, bbf, vbt, batch, seq, min(TOPK_MAX, seq // 4))
            mix_p = [o_a, o_b]
            outs["a_p"].append(na.reshape(batch, seq, 2, H_A, 128).transpose(0, 1, 3, 2, 4)
                               .reshape(batch, seq, H_A, 2 * DA + DV_A))
            outs["b_p"].append(nb.reshape(batch, seq, 2 * DB + DI))
            nas, nbs, _, _, qs, wis = _proj_even(xs, wa, wb, wq, ww, qscale)
            o_s = _sample_even(
                page_table, lam_e[i], g,
                _pad_rows(qs.astype(F32).reshape(nreq, ntok, 1536), rpad),
                _pad_rows(wis.reshape(nreq, ntok, 128), rpad),
                _a_pages(_pad_rows(nas.reshape(nreq, ntok, 1024), rpad)),
                _pad_rows(nbs.reshape(nreq, ntok, 320), page).transpose(0, 2, 1),
                _a_pages(cache_a[i].reshape(-1, page, 1024)), cache_b[i].transpose(0, 2, 1),
                min(TOPK_MAX, (past + ntok) // 4), lam_init)
            mix_s = [o_s[:, :ntok].reshape(nreq * ntok, 1024)]
            outs["a_s"].append(nas.reshape(nreq, ntok, H_A, 2 * DA + DV_A))
            outs["b_s"].append(nbs.reshape(nreq, ntok, 2 * DB + DI))
        else:
            w_in = w_in_o[i].astype(BF)
            w_out = w_out_o[i].astype(BF)
            wp = w_pool[i].astype(BF)
            scale = row2(pool_scale[i])
            sg, sb = row2(sgu_g[i]), row2(sgu_b[i])
            prev16 = jnp.zeros((batch, POOL_BUF + 1, MIX_C), F32)
            m_p, xc = _odd_mixer_prompt(prev16, xp, w_in, sg, sb, wp, scale, w_s[i], b_s[i].T, batch, seq, 0)
            mix_p = [m_p]
            outs["pool_p"].append(xc.reshape(batch, seq, MIX_C)[:, seq - POOL_BUF:])
            xcs, us, vns = _proj_odd(xs, w_in, sg, sb)
            tmaj = lambda v: v.reshape(nreq, ntok, -1).transpose(1, 0, 2)
            m_t = _pool_sgu_sample(
                w_s[i][:, :ntok, :ntok].reshape(-1), b_s[i][:, :ntok].reshape(-1),
                state_pool[i].transpose(1, 0, 2), tmaj(xcs), tmaj(us), tmaj(vns), wp, scale, past)
            mix_s = [m_t.transpose(1, 0, 2).reshape(nreq * ntok, 1024)]
            ext = jnp.concatenate([state_pool[i], xcs.reshape(nreq, ntok, MIX_C)], 1)
            outs["pool_s"].append(ext[:, ext.shape[1] - POOL_BUF:])
            outs["v_s"].append(vns.reshape(nreq, ntok, MIX_D))
        g0, b0, g1, b1 = row2(ln_g[l, 0]), row2(ln_b[l, 0]), row2(ln_g[l, 1]), row2(ln_b[l, 1])
        w1, w2 = w_mlp1[l].astype(BF), w_mlp2[l].astype(BF)
        xp = _layer_tail(mix_p, w_out, xp, g0, b0, w1, w2, g1, b1)
        xs = _layer_tail(mix_s, w_out, xs, g0, b0, w1, w2, g1, b1)

    st = lambda k: jnp.stack(outs[k])
    return (xp.reshape(batch, seq, D_MODEL), xs.reshape(nreq, ntok, D_MODEL), st("a_p"), st("b_p"),
            st("pool_p"), st("a_s"), st("b_s"), st("pool_s"), st("v_s"))
```

```python
import functools
import math

import jax
import jax.numpy as jnp
from jax import lax
from jax.experimental import pallas as pl
from jax.experimental.pallas import tpu as pltpu

D_MODEL = 1024
H_A = 4
DA = 64
DV_A = 128
H_B = 4
DB = 128
HI = 8
DI = 64
TOPK_MAX = 256
MIX_C = 512
MIX_D = 512
POOL_WINDOWS = (2, 4, 8, 16)
C_GROUP = 128
POOL_BUF = 15
CHUNK = 128
D_GROUPS = 4
D_FF = 4096
DEPTH = 2
ALPHA = (2 * DEPTH) ** 0.25
EPS = 1e-5

BF = jnp.bfloat16
F32 = jnp.float32
NEG = -1e30
INT_MIN = -(2 ** 31)
FLT_LOWEST = -3.4028234663852886e38
MANY = 1e9
LOG2E = math.log2(math.e)
V_ROWS = 144
VMEM_LIMIT_BYTES = 56 * 1024 * 1024
ROW_TILE = 512
FF_CHUNK = 512
DIFF_QUERY_TILE = 512
DIFF_KEY_TILE = 256
SPARSE_QUERY_TILE = 256
SPARSE_KEY_TILE = 256


def _params(n_axes):
    return pltpu.CompilerParams(dimension_semantics=("arbitrary",) * n_axes,
                                vmem_limit_bytes=VMEM_LIMIT_BYTES)


def _nn(a, b):
    return jnp.dot(a, b, preferred_element_type=F32)


def _nt(a, b):
    return lax.dot_general(a, b, (((1,), (1,)), ((), ())), preferred_element_type=F32)


def _ln(z, g, b):
    mu = jnp.mean(z, -1, keepdims=True)
    d = z - mu
    var = jnp.mean(d * d, -1, keepdims=True)
    return d * lax.rsqrt(var + EPS) * g + b


def _const_spec(shape):
    nd = len(shape)
    return pl.BlockSpec(shape, lambda *_: (0,) * nd)


def _key_to_float(k):
    return lax.bitcast_convert_type(jnp.where(k < 0, k ^ jnp.int32(0x7FFFFFFF), k), F32)


def _kth_largest(count, shape, ksel, bits=1, n_finite=None, n_nonneg=None, count_hi=None):
    if n_finite is None:
        n_finite, = count([jnp.full(shape, -jnp.inf, F32)], True)
    if n_nonneg is None:
        n_nonneg, = count([jnp.zeros(shape, F32)], False)
    nonneg = n_nonneg >= ksel
    start = (jnp.where(nonneg, jnp.int32(0), jnp.int32(INT_MIN)), jnp.where(nonneg, n_nonneg, MANY),
             jnp.where(nonneg, 0.0, n_nonneg))

    def refine(carry, shift, nbits):
        k, n_ge, n_above = carry
        cands = [k | lax.shift_left(jnp.int32(d), shift) for d in range(1, 2 ** nbits)]
        prev_ok = None
        for cand, n_cand in zip(cands, count([_key_to_float(c) for c in cands], False)):
            ok = n_cand >= ksel
            k, n_ge = jnp.where(ok, cand, k), jnp.where(ok, n_cand, n_ge)
            first_miss = jnp.logical_not(ok) if prev_ok is None else jnp.logical_and(prev_ok, jnp.logical_not(ok))
            n_above = jnp.where(first_miss, n_cand, n_above)
            prev_ok = ok
        return k, n_ge, n_above

    if count_hi is None:
        npass, rest = divmod(31, bits)
        carry = lax.fori_loop(0, npass, lambda it, cr: refine(cr, jnp.int32(31 - bits) - bits * it, bits), start)
        if rest:
            carry = refine(carry, jnp.int32(0), rest)
    else:
        def guess(it, k):
            cand = k | lax.shift_left(jnp.int32(1), jnp.int32(30) - it)
            return jnp.where(count_hi(cand) >= ksel, cand, k)

        k_hi = lax.fori_loop(0, 15, guess, start[0])
        top = k_hi + jnp.int32(1 << 16)
        n_lo, = count([_key_to_float(k_hi)], False)
        n_up, = count([_key_to_float(top)], False)
        n_lo = jnp.where(k_hi == start[0], start[1], n_lo)
        n_up = jnp.where(top < k_hi, 0.0, n_up)
        good = jnp.logical_or(n_finite < ksel, jnp.logical_and(n_lo >= ksel, n_up < ksel))
        carry = lax.cond(
            jnp.min(jnp.where(good, 1.0, 0.0)) > 0.0,
            lambda: (k_hi, n_lo, n_up),
            lambda: lax.fori_loop(0, 15, lambda it, cr: refine(cr, jnp.int32(30) - it, 1), start))
        carry = lax.fori_loop(15, 31, lambda it, cr: refine(cr, jnp.int32(30) - it, 1), carry)
    k, n_ge, n_above = carry
    short = n_finite < ksel
    t = jnp.where(short, FLT_LOWEST, _key_to_float(k))
    need = jnp.where(short, MANY, ksel - n_above)
    tied = jnp.where(short, 0.0, jnp.where(n_ge > ksel, 1.0, 0.0))
    return t, need, tied


def _proj_even_kernel(x_ref, wa_ref, wb_ref, wq_ref, ww_ref, qs_ref,
                      na_ref, nb_ref, abf_ref, bbf_ref, q_ref, wi_ref):
    x = x_ref[...].astype(BF)
    a = _nn(x, wa_ref[...])
    na_ref[...] = a
    abf_ref[...] = a.astype(BF)
    b = _nn(x, wb_ref[...])
    nb_ref[...] = b
    bbf_ref[...] = b.astype(BF)
    q_ref[...] = (_nn(x, wq_ref[...]) * qs_ref[...]).astype(BF)
    wi_ref[...] = _nn(x, ww_ref[...]) * (HI ** -0.5)


def _proj_even(x, wa, wb, wq, ww, qscale, tm=ROW_TILE):
    n = x.shape[0]
    tm = min(tm, n)
    row = lambda w: pl.BlockSpec((tm, w), lambda i: (i, 0))
    return pl.pallas_call(
        _proj_even_kernel,
        grid=(n // tm,),
        in_specs=[row(D_MODEL), _const_spec(wa.shape), _const_spec(wb.shape),
                  _const_spec(wq.shape), _const_spec(ww.shape), _const_spec(qscale.shape)],
        out_specs=[row(1024), row(320), row(1024), row(320), row(1536), row(128)],
        out_shape=[jax.ShapeDtypeStruct((n, 1024), F32), jax.ShapeDtypeStruct((n, 320), F32),
                   jax.ShapeDtypeStruct((n, 1024), BF), jax.ShapeDtypeStruct((n, 320), BF),
                   jax.ShapeDtypeStruct((n, 1536), BF), jax.ShapeDtypeStruct((n, 128), F32)],
        compiler_params=_params(1),
        name="proj_even",
    )(x, wa, wb, wq, ww, qscale)


def _proj_even_t_kernel(x_ref, wa_ref, wb_ref, wqt_ref, wvat_ref, wvbt_ref, wwt_ref,
                        na_ref, nb_ref, abf_ref, bbf_ref, qt_ref, vat_ref, vbt_ref, wit_ref):
    x = x_ref[...]
    xb = x.astype(BF)
    xt = x.T.astype(BF)
    a = _nn(xb, wa_ref[...])
    tm = a.shape[0]
    for h in range(H_A):
        for part in range(2):
            na_ref[pl.ds(H_A * part + h, tm, stride=2 * H_A), :] = a[:, h * 256 + part * 128:h * 256 + (part + 1) * 128]
    abf_ref[...] = a.astype(BF)
    b = _nn(xb, wb_ref[...])
    nb_ref[...] = b
    bbf_ref[...] = b.astype(BF)
    qt = _nn(wqt_ref[...], xt)
    qt_ref[0:512, :] = (qt[0:512] * (DA ** -0.5 * LOG2E)).astype(BF)
    qt_ref[512:1024, :] = (qt[512:1024] * (DB ** -0.5 * LOG2E)).astype(BF)
    qt_ref[1024:1536, :] = (qt[1024:1536] * (DI ** -0.5)).astype(BF)
    pad = V_ROWS - 128
    ones_rows = jnp.where(lax.broadcasted_iota(jnp.int32, (pad, tm), 0) == 0, 1.0, 0.0).astype(BF)
    va = _nn(wvat_ref[...], xt).astype(BF)
    for h in range(H_A):
        vat_ref[h * V_ROWS:h * V_ROWS + 128, :] = va[h * 128:(h + 1) * 128]
        vat_ref[h * V_ROWS + 128:(h + 1) * V_ROWS, :] = ones_rows
    vbt_ref[0:128, :] = _nn(wvbt_ref[...], xt).astype(BF)
    vbt_ref[128:V_ROWS, :] = ones_rows
    wit_ref[...] = _nn(wwt_ref[...], xt) * (HI ** -0.5)


def _proj_even_t(x, wa, wb, wqt, wvat, wvbt, wwt, tm=ROW_TILE):
    n = x.shape[0]
    row = lambda w: pl.BlockSpec((tm, w), lambda i: (i, 0))
    col = lambda h: pl.BlockSpec((h, tm), lambda i: (0, i))
    return pl.pallas_call(
        _proj_even_t_kernel,
        grid=(n // tm,),
        in_specs=[row(D_MODEL)] + [_const_spec(w.shape) for w in (wa, wb, wqt, wvat, wvbt, wwt)],
        out_specs=[pl.BlockSpec((2 * H_A * tm, 128), lambda i: (i, 0)), row(320), row(1024), row(320),
                   col(1536), col(H_A * V_ROWS), col(V_ROWS), col(16)],
        out_shape=[jax.ShapeDtypeStruct((2 * H_A * n, 128), F32), jax.ShapeDtypeStruct((n, 320), F32),
                   jax.ShapeDtypeStruct((n, 1024), BF), jax.ShapeDtypeStruct((n, 320), BF),
                   jax.ShapeDtypeStruct((1536, n), BF), jax.ShapeDtypeStruct((H_A * V_ROWS, n), BF),
                   jax.ShapeDtypeStruct((V_ROWS, n), BF), jax.ShapeDtypeStruct((16, n), F32)],
        compiler_params=_params(1),
        name="proj_even_t",
    )(x, wa, wb, wqt, wvat, wvbt, wwt)


def _lambda(lam_ref, lam_init):
    lp = lam_ref[...]
    return (jnp.exp(jnp.sum(lp[0:1] * lp[1:2], axis=-1, keepdims=True))
            - jnp.exp(jnp.sum(lp[2:3] * lp[3:4], axis=-1, keepdims=True)) + lam_init)


def _split_q12(q):
    qf = q.astype(F32)
    lane = lax.broadcasted_iota(jnp.int32, qf.shape, 1)
    return jnp.concatenate([jnp.where(lane < DA, qf, 0.0), jnp.where(lane >= DA, qf, 0.0)],
                           axis=0).astype(BF)


def _subln(o, lam, g, lam_init):
    r = o.shape[0] // 2
    d = o[:r] - lam * o[r:]
    ms = jnp.mean(d * d, -1, keepdims=True)
    return d * lax.rsqrt(ms + EPS) * g * (1.0 - lam_init)


def _kth_largest_rows(sc_ref, nch, tk, ksel):
    rows = sc_ref.shape[0]

    def count(cands, strict):
        accs = [jnp.zeros((rows, tk), F32) for _ in cands]
        for c in range(nch):
            x = sc_ref[:, c * tk:(c + 1) * tk]
            accs = [a + jnp.where((x > cand) if strict else (x >= cand), 1.0, 0.0)
                    for a, cand in zip(accs, cands)]
        return [jnp.sum(a, -1, keepdims=True) for a in accs]

    n_finite, n_nonneg = count([jnp.full((rows, 1), FLT_LOWEST, F32), jnp.zeros((rows, 1), F32)], False)
    return _kth_largest(count, (rows, 1), ksel, 3, n_finite, n_nonneg)


def _top16(x):
    return lax.bitcast_convert_type(lax.bitcast_convert_type(x, jnp.int32) & jnp.int32(-65536), F32)


def _kth_largest_cols(sc_ref, sc16_ref, nch, tk, ksel, n_finite=None, n_nonneg=None):
    cols = sc_ref.shape[1]

    def count_hi(cand_key):
        cand = _top16(_key_to_float(cand_key)).astype(BF)

        def body(c, acc):
            x = sc16_ref[pl.ds(pl.multiple_of(c * tk, tk), tk), :]
            hit = jnp.where(x >= cand, jnp.ones((), BF), jnp.zeros((), BF))
            parts = [hit[r * 32:(r + 1) * 32] for r in range(tk // 32)]
            while len(parts) > 1:
                parts = [a + b for a, b in zip(parts[0::2], parts[1::2])]
            return acc + parts[0]
        acc = lax.fori_loop(0, nch, body, jnp.zeros((32, cols), BF))
        return jnp.sum(acc.astype(F32), 0, keepdims=True)

    def count(cands, strict):
        cand, = cands

        def body(c, acc):
            x = sc_ref[pl.ds(pl.multiple_of(c * tk, tk), tk), :]
            hit = (x > cand) if strict else (x >= cand)
            return acc + jnp.sum(jnp.where(hit, 1.0, 0.0).reshape(tk // 32, 32, cols), axis=0)
        acc = lax.fori_loop(0, nch, body, jnp.zeros((32, cols), F32))
        return [jnp.sum(acc, 0, keepdims=True)]

    return _kth_largest(count, (1, cols), ksel, 1, n_finite, n_nonneg, count_hi)


def _selected(x, t, need, eq_before, tri):
    eq = jnp.where(x == t, 1.0, 0.0)
    rank = eq_before + _nn(eq.astype(BF), tri)
    tie_taken = jnp.where(rank <= need, eq, 0.0)
    return jnp.where(x > t, 1.0, tie_taken), jnp.sum(eq, -1, keepdims=True)


def _tri(n, lower):
    r = lax.broadcasted_iota(jnp.int32, (n, n), 0)
    c = lax.broadcasted_iota(jnp.int32, (n, n), 1)
    return jnp.where((r >= c) if lower else (r <= c), 1.0, 0.0).astype(BF)


def _flash_key_major(npair, nlast, nchain, tq, qk, mask, v_t, s_ref, p_ref, al_ref, acc_ref):
    def scores(c, slot):
        for j in range(nchain):
            s_ref[slot, j] = qk(c, j)

    def softmax(c, slot, ms, last):
        ms_new = []
        for j in range(nchain):
            s = mask(c, j, s_ref[slot, j], last)
            m_new = jnp.maximum(ms[j], jnp.max(s, 0, keepdims=True))
            p_ref[slot, j] = jnp.exp2(s - m_new).astype(BF)
            al_ref[slot, j] = jnp.exp2(ms[j] - m_new)
            ms_new.append(m_new)
        return tuple(ms_new)

    def fold(c, slot):
        for j in range(nchain):
            acc_ref[j] = al_ref[slot, j] * acc_ref[j] + _nn(v_t(c, j), p_ref[slot, j])

    acc_ref[...] = jnp.zeros(acc_ref.shape, F32)
    p_ref[1] = jnp.zeros(p_ref.shape[1:], BF)
    al_ref[1] = jnp.ones(al_ref.shape[1:], F32)
    scores(0, 0)

    def pair(c, carry, last, final):
        fold(jnp.maximum(c - 1, 0), 1)
        carry = softmax(c, 0, carry, last)
        scores(c + 1, 1)
        fold(c, 0)
        carry = softmax(c + 1, 1, carry, last)
        if not final:
            scores(c + 2, 0)
        return carry

    n_plain = npair - nlast
    carry = lax.fori_loop(0, n_plain, lambda c, cr: pair(2 * c, cr, False, False),
                          tuple(jnp.full((1, tq), NEG, F32) for _ in range(nchain)))
    for d in range(nlast):
        carry = pair(2 * (n_plain + d), carry, True, d == nlast - 1)
    fold(2 * npair - 1, 1)


def _diff_kernel(lam_ref, g_ref, qt_ref, a_ref, vt_ref, o_ref, qh_ref, s_ref, p_ref, al_ref, acc_ref,
                 *, tq, tk, lam_init):
    i = pl.program_id(1)
    lam = _lambda(lam_ref, lam_init)
    sub = lax.broadcasted_iota(jnp.int32, (128, tq), 0)
    key = lax.broadcasted_iota(jnp.int32, (tk, 2 * tq), 0)
    qry = i * tq + lax.broadcasted_iota(jnp.int32, (tk, 2 * tq), 1) % tq
    for h in range(H_A):
        qt = qt_ref[h * 128:(h + 1) * 128, :].astype(F32)
        qh_ref[h, :, 0:tq] = jnp.where(sub < DA, qt, 0.0).astype(BF)
        qh_ref[h, :, tq:2 * tq] = jnp.where(sub >= DA, qt, 0.0).astype(BF)

    def qk(c, h):
        return _nn(a_ref[pl.ds(pl.multiple_of(c * tk, tk), tk), h * 256:h * 256 + 128], qh_ref[h])

    def mask(c, h, s, last):
        return jnp.where(c * tk + key <= qry, s, NEG) if last else s

    def v_t(c, h):
        return vt_ref[h * V_ROWS:(h + 1) * V_ROWS, pl.ds(pl.multiple_of(c * tk, tk), tk)]

    nlast = tq // (2 * tk)
    _flash_key_major((i + 1) * nlast, nlast, H_A, 2 * tq, qk, mask, v_t, s_ref, p_ref, al_ref, acc_ref)
    for h in range(H_A):
        o12 = acc_ref[h, 0:DV_A, :] * (1.0 / acc_ref[h, DV_A:DV_A + 1, :])
        d = o12[:, 0:tq] - lam * o12[:, tq:2 * tq]
        ms = jnp.mean(d * d, 0, keepdims=True)
        o = d * lax.rsqrt(ms + EPS) * g_ref[...] * (1.0 - lam_init)
        o_ref[:, h * 128:(h + 1) * 128] = o.T.astype(o_ref.dtype)


def _flash_scratch(nchain, feat, tk, tq):
    return [pltpu.VMEM((2, nchain, tk, tq), F32), pltpu.VMEM((2, nchain, tk, tq), BF),
            pltpu.VMEM((2, nchain, 1, tq), F32), pltpu.VMEM((nchain, feat, tq), F32)]


def _diff_attn_prompt(lam_e, g_col, qt, a_bf, vat, batch, seq, lam_init, tq=DIFF_QUERY_TILE, tk=DIFF_KEY_TILE):
    nq = seq // tq
    return pl.pallas_call(
        functools.partial(_diff_kernel, tq=tq, tk=tk, lam_init=lam_init),
        grid=(batch, nq),
        in_specs=[_const_spec(lam_e.shape), _const_spec(g_col.shape),
                  pl.BlockSpec((512, tq), lambda b, i: (0, b * nq + i)),
                  pl.BlockSpec((seq, 1024), lambda b, i: (b, 0)),
                  pl.BlockSpec((H_A * V_ROWS, seq), lambda b, i: (0, b))],
        out_specs=pl.BlockSpec((tq, 512), lambda b, i: (b * nq + i, 0)),
        out_shape=jax.ShapeDtypeStruct((batch * seq, 512), BF),
        scratch_shapes=[pltpu.VMEM((H_A, 128, 2 * tq), BF)] + _flash_scratch(H_A, V_ROWS, tk, 2 * tq),
        compiler_params=_params(2),
        name="diff_attn_prompt",
    )(lam_e, g_col, qt, a_bf, vat)


def _attn_chunk(tq, tk):
    return tk if (tq // tk) % 2 == 0 else tk // 2


def _sparse_kernel(qbt_ref, qit_ref, wit_ref, b_ref, vbt_ref, o_ref, sc_ref, sc16_ref, qiw_ref, qbw_ref,
                   s_ref, p_ref, al_ref, acc_ref, *, tq, tk, ksel):
    i = pl.program_id(1)
    nch = (i + 1) * (tq // tk)
    ta = tk // 2
    key = lax.broadcasted_iota(jnp.int32, (ta, tq), 0)
    qry = i * tq + lax.broadcasted_iota(jnp.int32, (ta, tq), 1)
    w = wit_ref[...]
    for h in range(HI):
        qiw_ref[:, h * tq:(h + 1) * tq] = qit_ref[h * DI:(h + 1) * DI, :]
    for h in range(H_B):
        qbw_ref[:, h * tq:(h + 1) * tq] = qbt_ref[h * DB:(h + 1) * DB, :]
    heads = lambda x: jnp.concatenate([x] * H_B, axis=1)

    def score_chunk(c, counts):
        n_finite, n_nonneg = counts
        for u in range(2):
            r0 = pl.multiple_of(c * tk + u * ta, ta)
            d = jnp.maximum(_nn(b_ref[pl.ds(r0, ta), 2 * DB:2 * DB + DI], qiw_ref[...]), 0.0)
            acc = w[0:1, :] * d[:, 0:tq]
            for h in range(1, HI):
                acc = acc + w[h:h + 1, :] * d[:, h * tq:(h + 1) * tq]
            sc = jnp.where(r0 + key <= qry, acc, -jnp.inf)
            sc_ref[pl.ds(r0, ta), :] = sc
            sc16_ref[pl.ds(r0, ta), :] = _top16(sc).astype(BF)
            n_finite = n_finite + jnp.sum(jnp.where(sc > -jnp.inf, 1.0, 0.0), 0, keepdims=True)
            n_nonneg = n_nonneg + jnp.sum(jnp.where(sc >= 0.0, 1.0, 0.0), 0, keepdims=True)
        return n_finite, n_nonneg

    zero = jnp.zeros((1, tq), F32)
    n_finite, n_nonneg = lax.fori_loop(0, nch, score_chunk, (zero, zero))
    t, need, tied = _kth_largest_cols(sc_ref, sc16_ref, nch, tk, ksel, n_finite, n_nonneg)

    tv = _attn_chunk(tq, tk)

    def qk(c, _):
        return _nn(b_ref[pl.ds(pl.multiple_of(c * tv, tv), tv), 0:DB], qbw_ref[...])

    def v_t(c, _):
        return vbt_ref[:, pl.ds(pl.multiple_of(c * tv, tv), tv)]

    def to_bias(exact_ties):
        def body(c, eq_before):
            for u in range(2):
                rows = pl.ds(pl.multiple_of(c * tk + u * ta, ta), ta)
                x = sc_ref[rows, :]
                if exact_ties:
                    eq = jnp.where(x == t, 1.0, 0.0)
                    rank = eq_before + _nn(_tri(ta, True), eq.astype(BF))
                    sc_ref[rows, :] = jnp.where(
                        x > t, 0.0, jnp.where(x == t, jnp.where(rank <= need, 0.0, NEG), NEG))
                    eq_before = eq_before + jnp.sum(eq, 0, keepdims=True)
                else:
                    sc_ref[rows, :] = jnp.where(x >= t, 0.0, NEG)
            return eq_before
        return lax.fori_loop(0, nch, body, jnp.zeros((1, tq), F32))

    lax.cond(jnp.max(tied) > 0.0, lambda: to_bias(True), lambda: to_bias(False))

    def add_bias(c, _, s, last):
        return s + heads(sc_ref[pl.ds(pl.multiple_of(c * tv, tv), tv), :])

    _flash_key_major(nch * tk // (2 * tv), 1, 1, H_B * tq, qk, add_bias, v_t, s_ref, p_ref, al_ref, acc_ref)
    o = acc_ref[0, 0:DB, :] * (1.0 / acc_ref[0, DB:DB + 1, :])
    for h in range(H_B):
        o_ref[:, h * DB:(h + 1) * DB] = o[:, h * tq:(h + 1) * tq].T.astype(o_ref.dtype)


def _sparse_attn_prompt(qt, wit, b_bf, vbt, batch, seq, ksel, tq=SPARSE_QUERY_TILE, tk=SPARSE_KEY_TILE):
    nq = seq // tq
    return pl.pallas_call(
        functools.partial(_sparse_kernel, tq=tq, tk=tk, ksel=ksel),
        grid=(batch, nq),
        in_specs=[pl.BlockSpec((512, tq), lambda b, i: (1, b * nq + i)),
                  pl.BlockSpec((512, tq), lambda b, i: (2, b * nq + i)),
                  pl.BlockSpec((16, tq), lambda b, i: (0, b * nq + i)),
                  pl.BlockSpec((seq, 320), lambda b, i: (b, 0)),
                  pl.BlockSpec((V_ROWS, seq), lambda b, i: (0, b))],
        out_specs=pl.BlockSpec((tq, 512), lambda b, i: (b * nq + i, 0)),
        out_shape=jax.ShapeDtypeStruct((batch * seq, 512), BF),
        scratch_shapes=[pltpu.VMEM((seq, tq), F32), pltpu.VMEM((seq, tq), BF), pltpu.VMEM((DI, HI * tq), BF),
                        pltpu.VMEM((DB, H_B * tq), BF)] + _flash_scratch(1, V_ROWS, _attn_chunk(tq, tk), H_B * tq),
        compiler_params=_params(2),
        name="sparse_attn_prompt",
    )(qt, qt, wit, b_bf, vbt)


def _sample_even_kernel(pt_ref, lam_ref, g_ref, q_ref, wi_ref, *rest, npg, page, ksel, lam_init):
    del pt_ref
    nblk = npg + 1
    a_pages = rest[:nblk]
    b_pages = rest[nblk:2 * nblk]
    o_ref = rest[2 * nblk]
    sc_ref = rest[2 * nblk + 1]
    r = q_ref.shape[0]
    lam = _lambda(lam_ref, lam_init)
    q = q_ref[...]

    def a_rows(p, j):
        n_pos = a_pages[p].shape[0] // (2 * H_A)
        x = a_pages[p][pl.ds(j, n_pos, stride=2 * H_A), :]
        if n_pos < page:
            x = jnp.concatenate([x, jnp.zeros((page - n_pos, 128), F32)], axis=0)
        return x.astype(BF)

    def new_ok(rows):
        tok = lax.broadcasted_iota(jnp.int32, (rows, page), 0) % r
        return lax.broadcasted_iota(jnp.int32, (rows, page), 1) <= tok

    ok2 = new_ok(2 * r)
    for h in range(H_A):
        qq = _split_q12(q[:, h * 128:(h + 1) * 128])
        ss = [_nt(qq, a_rows(p, h)) for p in range(nblk)]
        ss[npg] = jnp.where(ok2, ss[npg], NEG)
        m = functools.reduce(jnp.maximum, ss)
        m = jnp.max(m, -1, keepdims=True)
        ps = [jnp.exp(s - m) for s in ss]
        l = jnp.sum(functools.reduce(lambda x, y: x + y, ps), -1, keepdims=True)
        acc = functools.reduce(lambda x, y: x + y, [
            _nn(ps[p].astype(BF), a_rows(p, H_A + h))
            for p in range(nblk)])
        o_ref[:, h * 128:(h + 1) * 128] = _subln(acc / l, lam, g_ref[...], lam_init)

    qi = jnp.concatenate([q[:, 1024 + h * DI:1024 + (h + 1) * DI] for h in range(HI)], axis=0).astype(BF)
    w = wi_ref[...]
    wcol = jnp.concatenate([w[:, h:h + 1] for h in range(HI)], axis=0)
    ok1 = new_ok(r)
    for p in range(nblk):
        d = jnp.maximum(_nn(qi, b_pages[p][2 * DB:2 * DB + DI, :].astype(BF)), 0.0) * wcol
        sc = d[0:r]
        for h in range(1, HI):
            sc = sc + d[h * r:(h + 1) * r]
        if p == npg:
            sc = jnp.where(ok1, sc, -jnp.inf)
        sc_ref[:, p * page:(p + 1) * page] = sc
    t, need, _ = _kth_largest_rows(sc_ref, nblk, page, ksel)

    tri = _tri(page, False)
    q4 = jnp.concatenate([q[:, 512 + h * DB:512 + (h + 1) * DB] for h in range(H_B)], axis=0).astype(BF)
    eq_before = jnp.zeros((r, 1), F32)
    ss, sels = [], []
    for p in range(nblk):
        sel, n_eq = _selected(sc_ref[:, p * page:(p + 1) * page], t, need, eq_before, tri)
        eq_before = eq_before + n_eq
        if p == npg:
            sel = jnp.where(ok1, sel, 0.0)
        sels.append(jnp.concatenate([sel] * H_B, axis=0) > 0.5)
        ss.append(_nn(q4, b_pages[p][0:DB, :].astype(BF)))
    m = functools.reduce(jnp.maximum, [jnp.where(sl, s, NEG) for sl, s in zip(sels, ss)])
    m = jnp.max(m, -1, keepdims=True)
    ps = [jnp.where(sl, jnp.exp(s - m), 0.0) for sl, s in zip(sels, ss)]
    l = jnp.sum(functools.reduce(lambda x, y: x + y, ps), -1, keepdims=True)
    acc = functools.reduce(lambda x, y: x + y, [
        _nt(ps[p].astype(BF), b_pages[p][DB:2 * DB, :].astype(BF)) for p in range(nblk)])
    o = acc / l
    for h in range(H_B):
        o_ref[:, 512 + h * DB:512 + (h + 1) * DB] = o[h * r:(h + 1) * r]


def _sample_even(page_table, lam_e, g, qs, wis, anew_pg, bnew_pg, cache_a_pg, cache_b_pg, ksel, lam_init):
    nreq, npg = page_table.shape
    page = cache_b_pg.shape[2]
    r = qs.shape[1]
    req = lambda a: pl.BlockSpec((None,) + a.shape[1:], lambda i, pt: (i, 0, 0))

    def page_spec(a, p):
        return pl.BlockSpec((None,) + a.shape[1:], lambda i, pt, p=p: (pt[i, p], 0, 0))

    in_specs = [_const_spec(lam_e.shape), _const_spec(g.shape), req(qs), req(wis)]
    in_specs += [page_spec(cache_a_pg, p) for p in range(npg)] + [req(anew_pg)]
    in_specs += [page_spec(cache_b_pg, p) for p in range(npg)] + [req(bnew_pg)]
    grid_spec = pltpu.PrefetchScalarGridSpec(
        num_scalar_prefetch=1, grid=(nreq,), in_specs=in_specs,
        out_specs=pl.BlockSpec((None, r, 1024), lambda i, pt: (i, 0, 0)),
        scratch_shapes=[pltpu.VMEM((r, (npg + 1) * page), F32)])
    return pl.pallas_call(
        functools.partial(_sample_even_kernel, npg=npg, page=page, ksel=ksel, lam_init=lam_init),
        grid_spec=grid_spec,
        out_shape=jax.ShapeDtypeStruct((nreq, r, 1024), F32),
        compiler_params=_params(1),
        name="sample_even",
    )(page_table, lam_e, g, qs, wis, *([cache_a_pg] * npg), anew_pg, *([cache_b_pg] * npg), bnew_pg)


def _layer_tail_kernel(*refs, n_lhs, ck):
    lhs = refs[:n_lhs]
    wo_ref, x_ref, g0_ref, b0_ref, w1_ref, w2_ref, g1_ref, b1_ref, o_ref, acc_ref = refs[n_lhs:]
    y = None
    k0 = 0
    for a_ref in lhs:
        kw = a_ref.shape[1]
        part = _nn(a_ref[...].astype(BF), wo_ref[k0:k0 + kw, :])
        y = part if y is None else y + part
        k0 += kw
    x1 = _ln(ALPHA * x_ref[...] + y, g0_ref[...], b0_ref[...])
    xb = x1.astype(BF)
    for c in range(D_FF // ck):
        h = jnp.maximum(_nn(xb, w1_ref[:, c * ck:(c + 1) * ck]), 0.0)
        part = _nn((h * h).astype(BF), w2_ref[c * ck:(c + 1) * ck, :])
        if c == 0:
            acc_ref[...] = part
        else:
            acc_ref[...] += part
    o_ref[...] = _ln(ALPHA * x1 + acc_ref[...], g1_ref[...], b1_ref[...])


def _layer_tail(lhs, w_out, x, g0, b0, w1, w2, g1, b1, tm=ROW_TILE, ck=FF_CHUNK):
    n = x.shape[0]
    tm = min(tm, n)
    row = lambda wd: pl.BlockSpec((tm, wd), lambda i: (i, 0))
    resident = lambda s: pl.BlockSpec(s, lambda i: (0, 0), pipeline_mode=pl.Buffered(1))
    vec = _const_spec(g0.shape)
    return pl.pallas_call(
        functools.partial(_layer_tail_kernel, n_lhs=len(lhs), ck=ck),
        grid=(n // tm,),
        in_specs=[row(a.shape[1]) for a in lhs] + [resident(w_out.shape), row(D_MODEL), vec, vec,
                                                   resident(w1.shape), resident(w2.shape), vec, vec],
        out_specs=row(D_MODEL),
        out_shape=jax.ShapeDtypeStruct((n, D_MODEL), F32),
        scratch_shapes=[pltpu.VMEM((tm, D_MODEL), F32)],
        compiler_params=_params(1),
        name="layer_tail",
    )(*lhs, w_out, x, g0, b0, w1, w2, g1, b1)


def _gelu(x):
    return 0.5 * x * (1.0 + jnp.tanh(math.sqrt(2.0 / math.pi) * (x + 0.044715 * (x * x * x))))


def _proj_odd_kernel(x_ref, w_ref, g_ref, b_ref, xc_ref, u_ref, vn_ref):
    h = _nn(x_ref[...].astype(BF), w_ref[...])
    xc_ref[...] = h[:, :MIX_C]
    u_ref[...] = _gelu(h[:, MIX_C:MIX_C + MIX_D])
    vn_ref[...] = _ln(_gelu(h[:, MIX_C + MIX_D:]), g_ref[...], b_ref[...])


def _proj_odd(x, w, g, b, tm=ROW_TILE):
    n = x.shape[0]
    tm = min(tm, n)
    row = lambda wd: pl.BlockSpec((tm, wd), lambda i: (i, 0))
    return pl.pallas_call(
        _proj_odd_kernel,
        grid=(n // tm,),
        in_specs=[row(D_MODEL), _const_spec(w.shape), _const_spec(g.shape), _const_spec(b.shape)],
        out_specs=[row(512), row(512), row(512)],
        out_shape=[jax.ShapeDtypeStruct((n, 512), F32)] * 3,
        compiler_params=_params(1),
        name="proj_odd",
    )(x, w, g, b)


def _odd_mixer_kernel(prev_ref, xh_ref, x_ref, w_ref, g_ref, b_ref, wp_ref, sc_ref, ws_ref, bs_ref,
                      o_ref, xc_ref, ext_ref, *, start):
    t = pl.program_id(1)
    hal = prev_ref.shape[0]
    rows = x_ref.shape[0]
    h = _nn(x_ref[...].astype(BF), w_ref[...])
    xc = h[:, :MIX_C]
    u = _gelu(h[:, MIX_C:MIX_C + MIX_D])
    vn = _ln(_gelu(h[:, MIX_C + MIX_D:]), g_ref[...], b_ref[...])
    xc_ref[...] = xc
    halo = _nn(xh_ref[...].astype(BF), w_ref[:, :MIX_C])
    ext_ref[0:hal, :] = jnp.where(t == 0, prev_ref[...], halo)
    ext_ref[hal:hal + rows, :] = xc
    pos = start + t * rows + lax.broadcasted_iota(jnp.int32, (rows, 1), 0)
    for g, w in enumerate(POOL_WINDOWS):
        gs = slice(g * C_GROUP, (g + 1) * C_GROUP)
        acc = ext_ref[hal:hal + rows, gs]
        for s in range(1, w):
            acc = acc + ext_ref[hal - s:hal - s + rows, gs]
        cnt = jnp.minimum(w, pos + 1).astype(F32)
        pooled = acc / cnt - ext_ref[hal:hal + rows, gs]
        c = _nn(pooled.astype(BF), wp_ref[g]) * sc_ref[:, gs]
        o_ref[:, gs] = c.astype(o_ref.dtype)
    r = lax.broadcasted_iota(jnp.int32, (CHUNK, CHUNK), 0)
    cc = lax.broadcasted_iota(jnp.int32, (CHUNK, CHUNK), 1)
    for g in range(D_GROUPS):
        gs = slice(g * 128, (g + 1) * 128)
        ws = jnp.where(r >= cc, ws_ref[g], 0.0).astype(BF)
        for k in range(rows // CHUNK):
            ks = slice(k * CHUNK, (k + 1) * CHUNK)
            s = _nn(ws, vn[ks, gs].astype(BF)) + bs_ref[:, g:g + 1]
            o_ref[ks, MIX_C + g * 128:MIX_C + (g + 1) * 128] = (u[ks, gs] * s).astype(o_ref.dtype)


def _odd_mixer_prompt(prev16, x, w_in, sg, sb, wp, scale, ws, bs_t, batch, seq, start, rows=ROW_TILE):
    rows = min(rows, seq)
    nt = seq // rows
    hal = prev16.shape[1]
    per = rows // hal
    row = lambda wd: pl.BlockSpec((rows, wd), lambda b, t: (b * nt + t, 0))
    return pl.pallas_call(
        functools.partial(_odd_mixer_kernel, start=start),
        grid=(batch, nt),
        in_specs=[pl.BlockSpec((None, hal, MIX_C), lambda b, t: (b, 0, 0)),
                  pl.BlockSpec((hal, D_MODEL), lambda b, t: (jnp.maximum((b * nt + t) * per - 1, 0), 0)),
                  row(D_MODEL), _const_spec(w_in.shape), _const_spec(sg.shape), _const_spec(sb.shape),
                  _const_spec(wp.shape), _const_spec(scale.shape), _const_spec(ws.shape),
                  _const_spec(bs_t.shape)],
        out_specs=[row(1024), row(MIX_C)],
        out_shape=[jax.ShapeDtypeStruct((batch * seq, 1024), BF),
                   jax.ShapeDtypeStruct((batch * seq, MIX_C), F32)],
        scratch_shapes=[pltpu.VMEM((hal + rows, MIX_C), F32)],
        compiler_params=_params(2),
        name="odd_mixer_prompt",
    )(prev16, x, x, w_in, sg, sb, wp, scale, ws, bs_t)


def _pool_sgu_sample_kernel(ws_ref, bs_ref, prev_ref, xc_ref, u_ref, vn_ref, wp_ref, sc_ref, o_ref,
                            *, start):
    nprev = prev_ref.shape[0]
    ntok = xc_ref.shape[0]
    for t in range(ntok):
        for g, w in enumerate(POOL_WINDOWS):
            gs = slice(g * C_GROUP, (g + 1) * C_GROUP)
            acc = None
            for s in range(w):
                j = nprev + t - s
                slab = prev_ref[j, :, gs] if j < nprev else xc_ref[j - nprev, :, gs]
                acc = slab if acc is None else acc + slab
            cnt = float(min(w, start + t + 1))
            pooled = acc / cnt - xc_ref[t, :, gs]
            o_ref[t, :, gs] = _nn(pooled.astype(BF), wp_ref[g]) * sc_ref[:, gs]
        for g in range(D_GROUPS):
            gs = slice(g * 128, (g + 1) * 128)
            s = None
            for j in range(t + 1):
                term = ws_ref[(g * ntok + t) * ntok + j] * vn_ref[j, :, gs]
                s = term if s is None else s + term
            s = s + bs_ref[g * ntok + t]
            o_ref[t, :, MIX_C + g * 128:MIX_C + (g + 1) * 128] = u_ref[t, :, gs] * s


def _pool_sgu_sample(ws_small, bs_small, prev_t, xc_t, u_t, vn_t, wp, scale, start):
    ntok, nreq, _ = xc_t.shape
    smem = pl.BlockSpec(memory_space=pltpu.SMEM)
    return pl.pallas_call(
        functools.partial(_pool_sgu_sample_kernel, start=start),
        grid=(1,),
        in_specs=[smem, smem, _const_spec(prev_t.shape), _const_spec(xc_t.shape), _const_spec(u_t.shape),
                  _const_spec(vn_t.shape), _const_spec(wp.shape), _const_spec(scale.shape)],
        out_specs=_const_spec((ntok, nreq, 1024)),
        out_shape=jax.ShapeDtypeStruct((ntok, nreq, 1024), F32),
        compiler_params=_params(1),
        name="pool_sgu_sample",
    )(ws_small, bs_small, prev_t, xc_t, u_t, vn_t, wp, scale)


def _even_weights(w_in):
    q_a, k_a, v_a, q_b, k_b, v_b, q_i, k_i, w_i = jnp.split(
        w_in, [512, 1024, 1536, 2048, 2176, 2304, 2816, 2880], axis=1)
    wa = jnp.concatenate([k_a.reshape(D_MODEL, H_A, 2 * DA), v_a.reshape(D_MODEL, H_A, DV_A)],
                         -1).reshape(D_MODEL, H_A * (2 * DA + DV_A))
    wb = jnp.concatenate([k_b, v_b, k_i], 1)
    wq = jnp.concatenate([q_a, q_b, q_i], 1)
    ww = jnp.concatenate([w_i, jnp.zeros((D_MODEL, 128 - HI), w_in.dtype)], 1)
    qscale = jnp.concatenate([jnp.full((512,), DA ** -0.5, F32), jnp.full((512,), DB ** -0.5, F32),
                              jnp.full((512,), DI ** -0.5, F32)]).reshape(1, 1536)
    wwt = jnp.concatenate([w_i.T, jnp.zeros((16 - HI, D_MODEL), w_in.dtype)], 0)
    natural = (wa.astype(BF), wb.astype(BF), wq.astype(BF), ww.astype(BF), qscale)
    feature_major = (wq.T.astype(BF), v_a.T.astype(BF), v_b.T.astype(BF), wwt.astype(BF))
    return natural, feature_major


def _a_pages(x):
    n, page, _ = x.shape
    return x.reshape(n, page, H_A, 2, 128).transpose(0, 1, 3, 2, 4).reshape(n, page * 2 * H_A, 128)


def _pad_rows(x, rows):
    return jnp.pad(x, ((0, 0), (0, rows - x.shape[1]), (0, 0)))


def kernel(x_prompt, x_sample, cache_a, cache_b, state_pool, page_table, w_in_e, lam_e, subln_g, w_out_e,
           w_in_o, w_pool, pool_scale, sgu_g, sgu_b, w_s, b_s, w_out_o, w_mlp1, w_mlp2, ln_g, ln_b):
    batch, seq, _ = x_prompt.shape
    nreq, ntok, _ = x_sample.shape
    npg = page_table.shape[1]
    page = cache_a.shape[2]
    past = npg * page
    xp = x_prompt.reshape(batch * seq, D_MODEL)
    xs = x_sample.reshape(nreq * ntok, D_MODEL)
    outs = {k: [] for k in ("a_p", "b_p", "pool_p", "a_s", "b_s", "pool_s", "v_s")}
    rpad = 8

    for l in range(DEPTH):
        i = l // 2
        row2 = lambda v: v.reshape(1, -1)
        if l % 2 == 0:
            lam_init = 0.8 - 0.6 * math.exp(-0.3 * l)
            (wa, wb, wq, ww, qscale), (wqt, wvat, wvbt, wwt) = _even_weights(w_in_e[i])
            g = row2(subln_g[i])
            w_out = w_out_e[i].astype(BF)
            na, nb, abf, bbf, qt, vat, vbt, wit = _proj_even_t(xp, wa, wb, wqt, wvat, wvbt, wwt)
            o_a = _diff_attn_prompt(lam_e[i], subln_g[i].reshape(-1, 1), qt, abf, vat, batch, seq, lam_init)
            o_b = _sparse_attn_prompt(qt, wit, bbf, vbt, batch, seq, min(TOPK_MAX, seq // 4))
            mix_p = [o_a, o_b]
            outs["a_p"].append(na.reshape(batch, seq, 2, H_A, 128).transpose(0, 1, 3, 2, 4)
                               .reshape(batch, seq, H_A, 2 * DA + DV_A))
            outs["b_p"].append(nb.reshape(batch, seq, 2 * DB + DI))
            nas, nbs, _, _, qs, wis = _proj_even(xs, wa, wb, wq, ww, qscale)
            o_s = _sample_even(
                page_table, lam_e[i], g,
                _pad_rows(qs.astype(F32).reshape(nreq, ntok, 1536), rpad),
                _pad_rows(wis.reshape(nreq, ntok, 128), rpad),
                _a_pages(_pad_rows(nas.reshape(nreq, ntok, 1024), rpad)),
                _pad_rows(nbs.reshape(nreq, ntok, 320), page).transpose(0, 2, 1),
                _a_pages(cache_a[i].reshape(-1, page, 1024)), cache_b[i].transpose(0, 2, 1),
                min(TOPK_MAX, (past + ntok) // 4), lam_init)
            mix_s = [o_s[:, :ntok].reshape(nreq * ntok, 1024)]
            outs["a_s"].append(nas.reshape(nreq, ntok, H_A, 2 * DA + DV_A))
            outs["b_s"].append(nbs.reshape(nreq, ntok, 2 * DB + DI))
        else:
            w_in = w_in_o[i].astype(BF)
            w_out = w_out_o[i].astype(BF)
            wp = w_pool[i].astype(BF)
            scale = row2(pool_scale[i])
            sg, sb = row2(sgu_g[i]), row2(sgu_b[i])
            prev16 = jnp.zeros((batch, POOL_BUF + 1, MIX_C), F32)
            m_p, xc = _odd_mixer_prompt(prev16, xp, w_in, sg, sb, wp, scale, w_s[i], b_s[i].T, batch, seq, 0)
            mix_p = [m_p]
            outs["pool_p"].append(xc.reshape(batch, seq, MIX_C)[:, seq - POOL_BUF:])
            xcs, us, vns = _proj_odd(xs, w_in, sg, sb)
            tmaj = lambda v: v.reshape(nreq, ntok, -1).transpose(1, 0, 2)
            m_t = _pool_sgu_sample(
                w_s[i][:, :ntok, :ntok].reshape(-1), b_s[i][:, :ntok].reshape(-1),
                state_pool[i].transpose(1, 0, 2), tmaj(xcs), tmaj(us), tmaj(vns), wp, scale, past)
            mix_s = [m_t.transpose(1, 0, 2).reshape(nreq * ntok, 1024)]
            ext = jnp.concatenate([state_pool[i], xcs.reshape(nreq, ntok, MIX_C)], 1)
            outs["pool_s"].append(ext[:, ext.shape[1] - POOL_BUF:])
            outs["v_s"].append(vns.reshape(nreq, ntok, MIX_D))
        g0, b0, g1, b1 = row2(ln_g[l, 0]), row2(ln_b[l, 0]), row2(ln_g[l, 1]), row2(ln_b[l, 1])
        w1, w2 = w_mlp1[l].astype(BF), w_mlp2[l].astype(BF)
        xp = _layer_tail(mix_p, w_out, xp, g0, b0, w1, w2, g1, b1)
        xs = _layer_tail(mix_s, w_out, xs, g0, b0, w1, w2, g1, b1)

    st = lambda k: jnp.stack(outs[k])
    return (xp.reshape(batch, seq, D_MODEL), xs.reshape(nreq, ntok, D_MODEL), st("a_p"), st("b_p"),
            st("pool_p"), st("a_s"), st("b_s"), st("pool_s"), st("v_s"))
```

```python
import functools
import math

import jax
import jax.numpy as jnp
from jax import lax
from jax.experimental import pallas as pl
from jax.experimental.pallas import tpu as pltpu

D_MODEL = 1024
H_A = 4
DA = 64
DV_A = 128
H_B = 4
DB = 128
HI = 8
DI = 64
TOPK_MAX = 256
MIX_C = 512
MIX_D = 512
POOL_WINDOWS = (2, 4, 8, 16)
C_GROUP = 128
POOL_BUF = 15
CHUNK = 128
D_GROUPS = 4
D_FF = 4096
DEPTH = 2
ALPHA = (2 * DEPTH) ** 0.25
EPS = 1e-5

BF = jnp.bfloat16
F32 = jnp.float32
NEG = -1e30
INT_MIN = -(2 ** 31)
FLT_LOWEST = -3.4028234663852886e38
MANY = 1e9
LOG2E = math.log2(math.e)
V_ROWS = 144
VMEM_LIMIT_BYTES = 56 * 1024 * 1024
ROW_TILE = 512
FF_CHUNK = 512
DIFF_QUERY_TILE = 512
DIFF_KEY_TILE = 256
SPARSE_QUERY_TILE = 256
SPARSE_KEY_TILE = 256


def _params(n_axes, vmem_mib):
    assert vmem_mib * 1024 * 1024 <= VMEM_LIMIT_BYTES
    return pltpu.CompilerParams(dimension_semantics=("arbitrary",) * n_axes,
                                vmem_limit_bytes=vmem_mib * 1024 * 1024)


def _nn(a, b):
    return jnp.dot(a, b, preferred_element_type=F32)


def _nt(a, b):
    return lax.dot_general(a, b, (((1,), (1,)), ((), ())), preferred_element_type=F32)


def _ln(z, g, b):
    mu = jnp.mean(z, -1, keepdims=True)
    d = z - mu
    var = jnp.mean(d * d, -1, keepdims=True)
    return d * lax.rsqrt(var + EPS) * g + b


def _const_spec(shape):
    nd = len(shape)
    return pl.BlockSpec(shape, lambda *_: (0,) * nd)


def _key_to_float(k):
    return lax.bitcast_convert_type(jnp.where(k < 0, k ^ jnp.int32(0x7FFFFFFF), k), F32)


def _kth_largest(count, shape, ksel, bits=1, n_finite=None, n_nonneg=None, count_hi=None):
    if n_finite is None:
        n_finite, = count([jnp.full(shape, -jnp.inf, F32)], True)
    if n_nonneg is None:
        n_nonneg, = count([jnp.zeros(shape, F32)], False)
    nonneg = n_nonneg >= ksel
    start = (jnp.where(nonneg, jnp.int32(0), jnp.int32(INT_MIN)), jnp.where(nonneg, n_nonneg, MANY),
             jnp.where(nonneg, 0.0, n_nonneg))

    def refine(carry, shift, nbits):
        k, n_ge, n_above = carry
        cands = [k | lax.shift_left(jnp.int32(d), shift) for d in range(1, 2 ** nbits)]
        prev_ok = None
        for cand, n_cand in zip(cands, count([_key_to_float(c) for c in cands], False)):
            ok = n_cand >= ksel
            k, n_ge = jnp.where(ok, cand, k), jnp.where(ok, n_cand, n_ge)
            first_miss = jnp.logical_not(ok) if prev_ok is None else jnp.logical_and(prev_ok, jnp.logical_not(ok))
            n_above = jnp.where(first_miss, n_cand, n_above)
            prev_ok = ok
        return k, n_ge, n_above

    if count_hi is None:
        npass, rest = divmod(31, bits)
        carry = lax.fori_loop(0, npass, lambda it, cr: refine(cr, jnp.int32(31 - bits) - bits * it, bits), start)
        if rest:
            carry = refine(carry, jnp.int32(0), rest)
    else:
        def guess(it, k):
            cand = k | lax.shift_left(jnp.int32(1), jnp.int32(30) - it)
            return jnp.where(count_hi(cand) >= ksel, cand, k)

        k_hi = lax.fori_loop(0, 15, guess, start[0])
        top = k_hi + jnp.int32(1 << 16)
        n_lo, = count([_key_to_float(k_hi)], False)
        n_up, = count([_key_to_float(top)], False)
        n_lo = jnp.where(k_hi == start[0], start[1], n_lo)
        n_up = jnp.where(top < k_hi, 0.0, n_up)
        good = jnp.logical_or(n_finite < ksel, jnp.logical_and(n_lo >= ksel, n_up < ksel))
        carry = lax.cond(
            jnp.min(jnp.where(good, 1.0, 0.0)) > 0.0,
            lambda: (k_hi, n_lo, n_up),
            lambda: lax.fori_loop(0, 15, lambda it, cr: refine(cr, jnp.int32(30) - it, 1), start))
        carry = lax.fori_loop(15, 31, lambda it, cr: refine(cr, jnp.int32(30) - it, 1), carry)
    k, n_ge, n_above = carry
    short = n_finite < ksel
    t = jnp.where(short, FLT_LOWEST, _key_to_float(k))
    need = jnp.where(short, MANY, ksel - n_above)
    tied = jnp.where(short, 0.0, jnp.where(n_ge > ksel, 1.0, 0.0))
    return t, need, tied


def _proj_even_kernel(x_ref, wa_ref, wb_ref, wq_ref, ww_ref, qs_ref,
                      na_ref, nb_ref, abf_ref, bbf_ref, q_ref, wi_ref):
    x = x_ref[...].astype(BF)
    a = _nn(x, wa_ref[...])
    na_ref[...] = a
    abf_ref[...] = a.astype(BF)
    b = _nn(x, wb_ref[...])
    nb_ref[...] = b
    bbf_ref[...] = b.astype(BF)
    q_ref[...] = (_nn(x, wq_ref[...]) * qs_ref[...]).astype(BF)
    wi_ref[...] = _nn(x, ww_ref[...]) * (HI ** -0.5)


def _proj_even(x, wa, wb, wq, ww, qscale, tm=ROW_TILE):
    n = x.shape[0]
    tm = min(tm, n)
    row = lambda w: pl.BlockSpec((tm, w), lambda i: (i, 0))
    return pl.pallas_call(
        _proj_even_kernel,
        grid=(n // tm,),
        in_specs=[row(D_MODEL), _const_spec(wa.shape), _const_spec(wb.shape),
                  _const_spec(wq.shape), _const_spec(ww.shape), _const_spec(qscale.shape)],
        out_specs=[row(1024), row(320), row(1024), row(320), row(1536), row(128)],
        out_shape=[jax.ShapeDtypeStruct((n, 1024), F32), jax.ShapeDtypeStruct((n, 320), F32),
                   jax.ShapeDtypeStruct((n, 1024), BF), jax.ShapeDtypeStruct((n, 320), BF),
                   jax.ShapeDtypeStruct((n, 1536), BF), jax.ShapeDtypeStruct((n, 128), F32)],
        compiler_params=_params(1, 32),
        name="proj_even",
    )(x, wa, wb, wq, ww, qscale)


def _proj_even_t_kernel(x_ref, wa_ref, wb_ref, wqt_ref, wvat_ref, wvbt_ref, wwt_ref,
                        na_ref, nb_ref, abf_ref, bbf_ref, qt_ref, vat_ref, vbt_ref, wit_ref):
    x = x_ref[...]
    xb = x.astype(BF)
    xt = x.T.astype(BF)
    a = _nn(xb, wa_ref[...])
    tm = a.shape[0]
    for h in range(H_A):
        for part in range(2):
            na_ref[pl.ds(H_A * part + h, tm, stride=2 * H_A), :] = a[:, h * 256 + part * 128:h * 256 + (part + 1) * 128]
    abf_ref[...] = a.astype(BF)
    b = _nn(xb, wb_ref[...])
    nb_ref[...] = b
    bbf_ref[...] = b.astype(BF)
    qt = _nn(wqt_ref[...], xt)
    qt_ref[0:512, :] = (qt[0:512] * (DA ** -0.5 * LOG2E)).astype(BF)
    qt_ref[512:1024, :] = (qt[512:1024] * (DB ** -0.5 * LOG2E)).astype(BF)
    qt_ref[1024:1536, :] = (qt[1024:1536] * (DI ** -0.5)).astype(BF)
    pad = V_ROWS - 128
    ones_rows = jnp.where(lax.broadcasted_iota(jnp.int32, (pad, tm), 0) == 0, 1.0, 0.0).astype(BF)
    va = _nn(wvat_ref[...], xt).astype(BF)
    for h in range(H_A):
        vat_ref[h * V_ROWS:h * V_ROWS + 128, :] = va[h * 128:(h + 1) * 128]
        vat_ref[h * V_ROWS + 128:(h + 1) * V_ROWS, :] = ones_rows
    vbt_ref[0:128, :] = _nn(wvbt_ref[...], xt).astype(BF)
    vbt_ref[128:V_ROWS, :] = ones_rows
    wit_ref[...] = _nn(wwt_ref[...], xt) * (HI ** -0.5)


def _proj_even_t(x, wa, wb, wqt, wvat, wvbt, wwt, tm=ROW_TILE):
    n = x.shape[0]
    row = lambda w: pl.BlockSpec((tm, w), lambda i: (i, 0))
    col = lambda h: pl.BlockSpec((h, tm), lambda i: (0, i))
    return pl.pallas_call(
        _proj_even_t_kernel,
        grid=(n // tm,),
        in_specs=[row(D_MODEL)] + [_const_spec(w.shape) for w in (wa, wb, wqt, wvat, wvbt, wwt)],
        out_specs=[pl.BlockSpec((2 * H_A * tm, 128), lambda i: (i, 0)), row(320), row(1024), row(320),
                   col(1536), col(H_A * V_ROWS), col(V_ROWS), col(16)],
        out_shape=[jax.ShapeDtypeStruct((2 * H_A * n, 128), F32), jax.ShapeDtypeStruct((n, 320), F32),
                   jax.ShapeDtypeStruct((n, 1024), BF), jax.ShapeDtypeStruct((n, 320), BF),
                   jax.ShapeDtypeStruct((1536, n), BF), jax.ShapeDtypeStruct((H_A * V_ROWS, n), BF),
                   jax.ShapeDtypeStruct((V_ROWS, n), BF), jax.ShapeDtypeStruct((16, n), F32)],
        compiler_params=_params(1, 40),
        name="proj_even_t",
    )(x, wa, wb, wqt, wvat, wvbt, wwt)


def _lambda(lam_ref, lam_init):
    lp = lam_ref[...]
    return (jnp.exp(jnp.sum(lp[0:1] * lp[1:2], axis=-1, keepdims=True))
            - jnp.exp(jnp.sum(lp[2:3] * lp[3:4], axis=-1, keepdims=True)) + lam_init)


def _split_q12(q):
    qf = q.astype(F32)
    lane = lax.broadcasted_iota(jnp.int32, qf.shape, 1)
    return jnp.concatenate([jnp.where(lane < DA, qf, 0.0), jnp.where(lane >= DA, qf, 0.0)],
                           axis=0).astype(BF)


def _subln(o, lam, g, lam_init):
    r = o.shape[0] // 2
    d = o[:r] - lam * o[r:]
    ms = jnp.mean(d * d, -1, keepdims=True)
    return d * lax.rsqrt(ms + EPS) * g * (1.0 - lam_init)


def _kth_largest_rows(sc_ref, nch, tk, ksel):
    rows = sc_ref.shape[0]

    def count(cands, strict):
        accs = [jnp.zeros((rows, tk), F32) for _ in cands]
        for c in range(nch):
            x = sc_ref[:, c * tk:(c + 1) * tk]
            accs = [a + jnp.where((x > cand) if strict else (x >= cand), 1.0, 0.0)
                    for a, cand in zip(accs, cands)]
        return [jnp.sum(a, -1, keepdims=True) for a in accs]

    n_finite, n_nonneg = count([jnp.full((rows, 1), FLT_LOWEST, F32), jnp.zeros((rows, 1), F32)], False)
    return _kth_largest(count, (rows, 1), ksel, 3, n_finite, n_nonneg)


def _top16(x):
    return lax.bitcast_convert_type(lax.bitcast_convert_type(x, jnp.int32) & jnp.int32(-65536), F32)


def _kth_largest_cols(sc_ref, sc16_ref, nch, tk, ksel, n_finite=None, n_nonneg=None):
    cols = sc_ref.shape[1]

    def count_hi(cand_key):
        cand = _top16(_key_to_float(cand_key)).astype(BF)

        def body(c, acc):
            x = sc16_ref[pl.ds(pl.multiple_of(c * tk, tk), tk), :]
            hit = jnp.where(x >= cand, jnp.ones((), BF), jnp.zeros((), BF))
            parts = [hit[r * 32:(r + 1) * 32] for r in range(tk // 32)]
            while len(parts) > 1:
                parts = [a + b for a, b in zip(parts[0::2], parts[1::2])]
            return acc + parts[0]
        acc = lax.fori_loop(0, nch, body, jnp.zeros((32, cols), BF))
        return jnp.sum(acc.astype(F32), 0, keepdims=True)

    def count(cands, strict):
        cand, = cands

        def body(c, acc):
            x = sc_ref[pl.ds(pl.multiple_of(c * tk, tk), tk), :]
            hit = (x > cand) if strict else (x >= cand)
            return acc + jnp.sum(jnp.where(hit, 1.0, 0.0).reshape(tk // 32, 32, cols), axis=0)
        acc = lax.fori_loop(0, nch, body, jnp.zeros((32, cols), F32))
        return [jnp.sum(acc, 0, keepdims=True)]

    return _kth_largest(count, (1, cols), ksel, 1, n_finite, n_nonneg, count_hi)


def _selected(x, t, need, eq_before, tri):
    eq = jnp.where(x == t, 1.0, 0.0)
    rank = eq_before + _nn(eq.astype(BF), tri)
    tie_taken = jnp.where(rank <= need, eq, 0.0)
    return jnp.where(x > t, 1.0, tie_taken), jnp.sum(eq, -1, keepdims=True)


def _tri(n, lower):
    r = lax.broadcasted_iota(jnp.int32, (n, n), 0)
    c = lax.broadcasted_iota(jnp.int32, (n, n), 1)
    return jnp.where((r >= c) if lower else (r <= c), 1.0, 0.0).astype(BF)


def _flash_key_major(npair, nlast, nchain, tq, qk, mask, v_t, s_ref, p_ref, al_ref, acc_ref):
    def scores(c, slot):
        for j in range(nchain):
            s_ref[slot, j] = qk(c, j)

    def softmax(c, slot, ms, last):
        ms_new = []
        for j in range(nchain):
            s = mask(c, j, s_ref[slot, j], last)
            m_new = jnp.maximum(ms[j], jnp.max(s, 0, keepdims=True))
            p_ref[slot, j] = jnp.exp2(s - m_new).astype(BF)
            al_ref[slot, j] = jnp.exp2(ms[j] - m_new)
            ms_new.append(m_new)
        return tuple(ms_new)

    def fold(c, slot):
        for j in range(nchain):
            acc_ref[j] = al_ref[slot, j] * acc_ref[j] + _nn(v_t(c, j), p_ref[slot, j])

    acc_ref[...] = jnp.zeros(acc_ref.shape, F32)
    p_ref[1] = jnp.zeros(p_ref.shape[1:], BF)
    al_ref[1] = jnp.ones(al_ref.shape[1:], F32)
    scores(0, 0)

    def pair(c, carry, last, final):
        fold(jnp.maximum(c - 1, 0), 1)
        carry = softmax(c, 0, carry, last)
        scores(c + 1, 1)
        fold(c, 0)
        carry = softmax(c + 1, 1, carry, last)
        if not final:
            scores(c + 2, 0)
        return carry

    n_plain = npair - nlast
    carry = lax.fori_loop(0, n_plain, lambda c, cr: pair(2 * c, cr, False, False),
                          tuple(jnp.full((1, tq), NEG, F32) for _ in range(nchain)))
    for d in range(nlast):
        carry = pair(2 * (n_plain + d), carry, True, d == nlast - 1)
    fold(2 * npair - 1, 1)


def _diff_kernel(lam_ref, g_ref, qt_ref, a_ref, vt_ref, o_ref, qh_ref, s_ref, p_ref, al_ref, acc_ref,
                 *, tq, tk, lam_init):
    i = pl.program_id(1)
    lam = _lambda(lam_ref, lam_init)
    sub = lax.broadcasted_iota(jnp.int32, (128, tq), 0)
    key = lax.broadcasted_iota(jnp.int32, (tk, 2 * tq), 0)
    qry = i * tq + lax.broadcasted_iota(jnp.int32, (tk, 2 * tq), 1) % tq
    for h in range(H_A):
        qt = qt_ref[h * 128:(h + 1) * 128, :].astype(F32)
        qh_ref[h, :, 0:tq] = jnp.where(sub < DA, qt, 0.0).astype(BF)
        qh_ref[h, :, tq:2 * tq] = jnp.where(sub >= DA, qt, 0.0).astype(BF)

    def qk(c, h):
        return _nn(a_ref[pl.ds(pl.multiple_of(c * tk, tk), tk), h * 256:h * 256 + 128], qh_ref[h])

    def mask(c, h, s, last):
        return jnp.where(c * tk + key <= qry, s, NEG) if last else s

    def v_t(c, h):
        return vt_ref[h * V_ROWS:(h + 1) * V_ROWS, pl.ds(pl.multiple_of(c * tk, tk), tk)]

    nlast = tq // (2 * tk)
    _flash_key_major((i + 1) * nlast, nlast, H_A, 2 * tq, qk, mask, v_t, s_ref, p_ref, al_ref, acc_ref)
    for h in range(H_A):
        o12 = acc_ref[h, 0:DV_A, :] * (1.0 / acc_ref[h, DV_A:DV_A + 1, :])
        d = o12[:, 0:tq] - lam * o12[:, tq:2 * tq]
        ms = jnp.mean(d * d, 0, keepdims=True)
        o = d * lax.rsqrt(ms + EPS) * g_ref[...] * (1.0 - lam_init)
        o_ref[:, h * 128:(h + 1) * 128] = o.T.astype(o_ref.dtype)


def _flash_scratch(nchain, feat, tk, tq):
    return [pltpu.VMEM((2, nchain, tk, tq), F32), pltpu.VMEM((2, nchain, tk, tq), BF),
            pltpu.VMEM((2, nchain, 1, tq), F32), pltpu.VMEM((nchain, feat, tq), F32)]


def _diff_attn_prompt(lam_e, g_col, qt, a_bf, vat, batch, seq, lam_init, tq=DIFF_QUERY_TILE, tk=DIFF_KEY_TILE):
    nq = seq // tq
    return pl.pallas_call(
        functools.partial(_diff_kernel, tq=tq, tk=tk, lam_init=lam_init),
        grid=(batch, nq),
        in_specs=[_const_spec(lam_e.shape), _const_spec(g_col.shape),
                  pl.BlockSpec((512, tq), lambda b, i: (0, b * nq + i)),
                  pl.BlockSpec((seq, 1024), lambda b, i: (b, 0)),
                  pl.BlockSpec((H_A * V_ROWS, seq), lambda b, i: (0, b))],
        out_specs=pl.BlockSpec((tq, 512), lambda b, i: (b * nq + i, 0)),
        out_shape=jax.ShapeDtypeStruct((batch * seq, 512), BF),
        scratch_shapes=[pltpu.VMEM((H_A, 128, 2 * tq), BF)] + _flash_scratch(H_A, V_ROWS, tk, 2 * tq),
        compiler_params=_params(2, 52),
        name="diff_attn_prompt",
    )(lam_e, g_col, qt, a_bf, vat)


def _attn_chunk(tq, tk):
    return tk if (tq // tk) % 2 == 0 else tk // 2


def _sparse_kernel(qbt_ref, qit_ref, wit_ref, b_ref, vbt_ref, o_ref, sc_ref, sc16_ref, qiw_ref, qbw_ref,
                   s_ref, p_ref, al_ref, acc_ref, *, tq, tk, ksel):
    i = pl.program_id(1)
    nch = (i + 1) * (tq // tk)
    ta = tk // 2
    key = lax.broadcasted_iota(jnp.int32, (ta, tq), 0)
    qry = i * tq + lax.broadcasted_iota(jnp.int32, (ta, tq), 1)
    w = wit_ref[...]
    for h in range(HI):
        qiw_ref[:, h * tq:(h + 1) * tq] = qit_ref[h * DI:(h + 1) * DI, :]
    for h in range(H_B):
        qbw_ref[:, h * tq:(h + 1) * tq] = qbt_ref[h * DB:(h + 1) * DB, :]
    heads = lambda x: jnp.concatenate([x] * H_B, axis=1)

    def score_chunk(c, counts):
        n_finite, n_nonneg = counts
        for u in range(2):
            r0 = pl.multiple_of(c * tk + u * ta, ta)
            d = jnp.maximum(_nn(b_ref[pl.ds(r0, ta), 2 * DB:2 * DB + DI], qiw_ref[...]), 0.0)
            acc = w[0:1, :] * d[:, 0:tq]
            for h in range(1, HI):
                acc = acc + w[h:h + 1, :] * d[:, h * tq:(h + 1) * tq]
            sc = jnp.where(r0 + key <= qry, acc, -jnp.inf)
            sc_ref[pl.ds(r0, ta), :] = sc
            sc16_ref[pl.ds(r0, ta), :] = _top16(sc).astype(BF)
            n_finite = n_finite + jnp.sum(jnp.where(sc > -jnp.inf, 1.0, 0.0), 0, keepdims=True)
            n_nonneg = n_nonneg + jnp.sum(jnp.where(sc >= 0.0, 1.0, 0.0), 0, keepdims=True)
        return n_finite, n_nonneg

    zero = jnp.zeros((1, tq), F32)
    n_finite, n_nonneg = lax.fori_loop(0, nch, score_chunk, (zero, zero))
    t, need, tied = _kth_largest_cols(sc_ref, sc16_ref, nch, tk, ksel, n_finite, n_nonneg)

    tv = _attn_chunk(tq, tk)

    def qk(c, _):
        return _nn(b_ref[pl.ds(pl.multiple_of(c * tv, tv), tv), 0:DB], qbw_ref[...])

    def v_t(c, _):
        return vbt_ref[:, pl.ds(pl.multiple_of(c * tv, tv), tv)]

    def to_bias(exact_ties):
        def body(c, eq_before):
            for u in range(2):
                rows = pl.ds(pl.multiple_of(c * tk + u * ta, ta), ta)
                x = sc_ref[rows, :]
                if exact_ties:
                    eq = jnp.where(x == t, 1.0, 0.0)
                    rank = eq_before + _nn(_tri(ta, True), eq.astype(BF))
                    sc_ref[rows, :] = jnp.where(
                        x > t, 0.0, jnp.where(x == t, jnp.where(rank <= need, 0.0, NEG), NEG))
                    eq_before = eq_before + jnp.sum(eq, 0, keepdims=True)
                else:
                    sc_ref[rows, :] = jnp.where(x >= t, 0.0, NEG)
            return eq_before
        return lax.fori_loop(0, nch, body, jnp.zeros((1, tq), F32))

    lax.cond(jnp.max(tied) > 0.0, lambda: to_bias(True), lambda: to_bias(False))

    def add_bias(c, _, s, last):
        return s + heads(sc_ref[pl.ds(pl.multiple_of(c * tv, tv), tv), :])

    _flash_key_major(nch * tk // (2 * tv), 1, 1, H_B * tq, qk, add_bias, v_t, s_ref, p_ref, al_ref, acc_ref)
    o = acc_ref[0, 0:DB, :] * (1.0 / acc_ref[0, DB:DB + 1, :])
    for h in range(H_B):
        o_ref[:, h * DB:(h + 1) * DB] = o[:, h * tq:(h + 1) * tq].T.astype(o_ref.dtype)


def _sparse_attn_prompt(qt, wit, b_bf, vbt, batch, seq, ksel, tq=SPARSE_QUERY_TILE, tk=SPARSE_KEY_TILE):
    assert seq % tq == 0 and tq % tk == 0
    assert seq // 32 <= 256, "bfloat16 hit counters of the selection hold integers up to 256 only"
    nq = seq // tq
    return pl.pallas_call(
        functools.partial(_sparse_kernel, tq=tq, tk=tk, ksel=ksel),
        grid=(batch, nq),
        in_specs=[pl.BlockSpec((512, tq), lambda b, i: (1, b * nq + i)),
                  pl.BlockSpec((512, tq), lambda b, i: (2, b * nq + i)),
                  pl.BlockSpec((16, tq), lambda b, i: (0, b * nq + i)),
                  pl.BlockSpec((seq, 320), lambda b, i: (b, 0)),
                  pl.BlockSpec((V_ROWS, seq), lambda b, i: (0, b))],
        out_specs=pl.BlockSpec((tq, 512), lambda b, i: (b * nq + i, 0)),
        out_shape=jax.ShapeDtypeStruct((batch * seq, 512), BF),
        scratch_shapes=[pltpu.VMEM((seq, tq), F32), pltpu.VMEM((seq, tq), BF), pltpu.VMEM((DI, HI * tq), BF),
                        pltpu.VMEM((DB, H_B * tq), BF)] + _flash_scratch(1, V_ROWS, _attn_chunk(tq, tk), H_B * tq),
        compiler_params=_params(2, 32),
        name="sparse_attn_prompt",
    )(qt, qt, wit, b_bf, vbt)


def _sample_even_kernel(pt_ref, lam_ref, g_ref, q_ref, wi_ref, *rest, npg, page, ksel, lam_init):
    del pt_ref
    nblk = npg + 1
    a_pages = rest[:nblk]
    b_pages = rest[nblk:2 * nblk]
    o_ref = rest[2 * nblk]
    sc_ref = rest[2 * nblk + 1]
    r = q_ref.shape[0]
    lam = _lambda(lam_ref, lam_init)
    q = q_ref[...]

    def a_rows(p, j):
        n_pos = a_pages[p].shape[0] // (2 * H_A)
        x = a_pages[p][pl.ds(j, n_pos, stride=2 * H_A), :]
        if n_pos < page:
            x = jnp.concatenate([x, jnp.zeros((page - n_pos, 128), F32)], axis=0)
        return x.astype(BF)

    def new_ok(rows):
        tok = lax.broadcasted_iota(jnp.int32, (rows, page), 0) % r
        return lax.broadcasted_iota(jnp.int32, (rows, page), 1) <= tok

    ok2 = new_ok(2 * r)
    for h in range(H_A):
        qq = _split_q12(q[:, h * 128:(h + 1) * 128])
        ss = [_nt(qq, a_rows(p, h)) for p in range(nblk)]
        ss[npg] = jnp.where(ok2, ss[npg], NEG)
        m = functools.reduce(jnp.maximum, ss)
        m = jnp.max(m, -1, keepdims=True)
        ps = [jnp.exp(s - m) for s in ss]
        l = jnp.sum(functools.reduce(lambda x, y: x + y, ps), -1, keepdims=True)
        acc = functools.reduce(lambda x, y: x + y, [
            _nn(ps[p].astype(BF), a_rows(p, H_A + h))
            for p in range(nblk)])
        o_ref[:, h * 128:(h + 1) * 128] = _subln(acc / l, lam, g_ref[...], lam_init)

    qi = jnp.concatenate([q[:, 1024 + h * DI:1024 + (h + 1) * DI] for h in range(HI)], axis=0).astype(BF)
    w = wi_ref[...]
    wcol = jnp.concatenate([w[:, h:h + 1] for h in range(HI)], axis=0)
    ok1 = new_ok(r)
    for p in range(nblk):
        d = jnp.maximum(_nn(qi, b_pages[p][2 * DB:2 * DB + DI, :].astype(BF)), 0.0) * wcol
        sc = d[0:r]
        for h in range(1, HI):
            sc = sc + d[h * r:(h + 1) * r]
        if p == npg:
            sc = jnp.where(ok1, sc, -jnp.inf)
        sc_ref[:, p * page:(p + 1) * page] = sc
    t, need, _ = _kth_largest_rows(sc_ref, nblk, page, ksel)

    tri = _tri(page, False)
    q4 = jnp.concatenate([q[:, 512 + h * DB:512 + (h + 1) * DB] for h in range(H_B)], axis=0).astype(BF)
    eq_before = jnp.zeros((r, 1), F32)
    ss, sels = [], []
    for p in range(nblk):
        sel, n_eq = _selected(sc_ref[:, p * page:(p + 1) * page], t, need, eq_before, tri)
        eq_before = eq_before + n_eq
        if p == npg:
            sel = jnp.where(ok1, sel, 0.0)
        sels.append(jnp.concatenate([sel] * H_B, axis=0) > 0.5)
        ss.append(_nn(q4, b_pages[p][0:DB, :].astype(BF)))
    m = functools.reduce(jnp.maximum, [jnp.where(sl, s, NEG) for sl, s in zip(sels, ss)])
    m = jnp.max(m, -1, keepdims=True)
    ps = [jnp.where(sl, jnp.exp(s - m), 0.0) for sl, s in zip(sels, ss)]
    l = jnp.sum(functools.reduce(lambda x, y: x + y, ps), -1, keepdims=True)
    acc = functools.reduce(lambda x, y: x + y, [
        _nt(ps[p].astype(BF), b_pages[p][DB:2 * DB, :].astype(BF)) for p in range(nblk)])
    o = acc / l
    for h in range(H_B):
        o_ref[:, 512 + h * DB:512 + (h + 1) * DB] = o[h * r:(h + 1) * r]


def _sample_even(page_table, lam_e, g, qs, wis, anew_pg, bnew_pg, cache_a_pg, cache_b_pg, ksel, lam_init):
    nreq, npg = page_table.shape
    page = cache_b_pg.shape[2]
    r = qs.shape[1]
    req = lambda a: pl.BlockSpec((None,) + a.shape[1:], lambda i, pt: (i, 0, 0))

    def page_spec(a, p):
        return pl.BlockSpec((None,) + a.shape[1:], lambda i, pt, p=p: (pt[i, p], 0, 0))

    in_specs = [_const_spec(lam_e.shape), _const_spec(g.shape), req(qs), req(wis)]
    in_specs += [page_spec(cache_a_pg, p) for p in range(npg)] + [req(anew_pg)]
    in_specs += [page_spec(cache_b_pg, p) for p in range(npg)] + [req(bnew_pg)]
    grid_spec = pltpu.PrefetchScalarGridSpec(
        num_scalar_prefetch=1, grid=(nreq,), in_specs=in_specs,
        out_specs=pl.BlockSpec((None, r, 1024), lambda i, pt: (i, 0, 0)),
        scratch_shapes=[pltpu.VMEM((r, (npg + 1) * page), F32)])
    return pl.pallas_call(
        functools.partial(_sample_even_kernel, npg=npg, page=page, ksel=ksel, lam_init=lam_init),
        grid_spec=grid_spec,
        out_shape=jax.ShapeDtypeStruct((nreq, r, 1024), F32),
        compiler_params=_params(1, 40),
        name="sample_even",
    )(page_table, lam_e, g, qs, wis, *([cache_a_pg] * npg), anew_pg, *([cache_b_pg] * npg), bnew_pg)


def _layer_tail_kernel(*refs, n_lhs, ck):
    lhs = refs[:n_lhs]
    wo_ref, x_ref, g0_ref, b0_ref, w1_ref, w2_ref, g1_ref, b1_ref, o_ref, acc_ref = refs[n_lhs:]
    y = None
    k0 = 0
    for a_ref in lhs:
        kw = a_ref.shape[1]
        part = _nn(a_ref[...].astype(BF), wo_ref[k0:k0 + kw, :])
        y = part if y is None else y + part
        k0 += kw
    x1 = _ln(ALPHA * x_ref[...] + y, g0_ref[...], b0_ref[...])
    xb = x1.astype(BF)
    for c in range(D_FF // ck):
        h = jnp.maximum(_nn(xb, w1_ref[:, c * ck:(c + 1) * ck]), 0.0)
        part = _nn((h * h).astype(BF), w2_ref[c * ck:(c + 1) * ck, :])
        if c == 0:
            acc_ref[...] = part
        else:
            acc_ref[...] += part
    o_ref[...] = _ln(ALPHA * x1 + acc_ref[...], g1_ref[...], b1_ref[...])


def _layer_tail(lhs, w_out, x, g0, b0, w1, w2, g1, b1, tm=ROW_TILE, ck=FF_CHUNK):
    n = x.shape[0]
    tm = min(tm, n)
    row = lambda wd: pl.BlockSpec((tm, wd), lambda i: (i, 0))
    resident = lambda s: pl.BlockSpec(s, lambda i: (0, 0), pipeline_mode=pl.Buffered(1))
    vec = _const_spec(g0.shape)
    return pl.pallas_call(
        functools.partial(_layer_tail_kernel, n_lhs=len(lhs), ck=ck),
        grid=(n // tm,),
        in_specs=[row(a.shape[1]) for a in lhs] + [resident(w_out.shape), row(D_MODEL), vec, vec,
                                                   resident(w1.shape), resident(w2.shape), vec, vec],
        out_specs=row(D_MODEL),
        out_shape=jax.ShapeDtypeStruct((n, D_MODEL), F32),
        scratch_shapes=[pltpu.VMEM((tm, D_MODEL), F32)],
        compiler_params=_params(1, 48),
        name="layer_tail",
    )(*lhs, w_out, x, g0, b0, w1, w2, g1, b1)


def _gelu(x):
    return 0.5 * x * (1.0 + jnp.tanh(math.sqrt(2.0 / math.pi) * (x + 0.044715 * (x * x * x))))


def _proj_odd_kernel(x_ref, w_ref, g_ref, b_ref, xc_ref, u_ref, vn_ref):
    h = _nn(x_ref[...].astype(BF), w_ref[...])
    xc_ref[...] = h[:, :MIX_C]
    u_ref[...] = _gelu(h[:, MIX_C:MIX_C + MIX_D])
    vn_ref[...] = _ln(_gelu(h[:, MIX_C + MIX_D:]), g_ref[...], b_ref[...])


def _proj_odd(x, w, g, b, tm=ROW_TILE):
    n = x.shape[0]
    tm = min(tm, n)
    row = lambda wd: pl.BlockSpec((tm, wd), lambda i: (i, 0))
    return pl.pallas_call(
        _proj_odd_kernel,
        grid=(n // tm,),
        in_specs=[row(D_MODEL), _const_spec(w.shape), _const_spec(g.shape), _const_spec(b.shape)],
        out_specs=[row(512), row(512), row(512)],
        out_shape=[jax.ShapeDtypeStruct((n, 512), F32)] * 3,
        compiler_params=_params(1, 32),
        name="proj_odd",
    )(x, w, g, b)


def _odd_mixer_kernel(prev_ref, xh_ref, x_ref, w_ref, g_ref, b_ref, wp_ref, sc_ref, ws_ref, bs_ref,
                      o_ref, xc_ref, ext_ref, *, start):
    t = pl.program_id(1)
    hal = prev_ref.shape[0]
    rows = x_ref.shape[0]
    h = _nn(x_ref[...].astype(BF), w_ref[...])
    xc = h[:, :MIX_C]
    u = _gelu(h[:, MIX_C:MIX_C + MIX_D])
    vn = _ln(_gelu(h[:, MIX_C + MIX_D:]), g_ref[...], b_ref[...])
    xc_ref[...] = xc
    halo = _nn(xh_ref[...].astype(BF), w_ref[:, :MIX_C])
    ext_ref[0:hal, :] = jnp.where(t == 0, prev_ref[...], halo)
    ext_ref[hal:hal + rows, :] = xc
    pos = start + t * rows + lax.broadcasted_iota(jnp.int32, (rows, 1), 0)
    for g, w in enumerate(POOL_WINDOWS):
        gs = slice(g * C_GROUP, (g + 1) * C_GROUP)
        acc = ext_ref[hal:hal + rows, gs]
        for s in range(1, w):
            acc = acc + ext_ref[hal - s:hal - s + rows, gs]
        cnt = jnp.minimum(w, pos + 1).astype(F32)
        pooled = acc / cnt - ext_ref[hal:hal + rows, gs]
        c = _nn(pooled.astype(BF), wp_ref[g]) * sc_ref[:, gs]
        o_ref[:, gs] = c.astype(o_ref.dtype)
    r = lax.broadcasted_iota(jnp.int32, (CHUNK, CHUNK), 0)
    cc = lax.broadcasted_iota(jnp.int32, (CHUNK, CHUNK), 1)
    for g in range(D_GROUPS):
        gs = slice(g * 128, (g + 1) * 128)
        ws = jnp.where(r >= cc, ws_ref[g], 0.0).astype(BF)
        for k in range(rows // CHUNK):
            ks = slice(k * CHUNK, (k + 1) * CHUNK)
            s = _nn(ws, vn[ks, gs].astype(BF)) + bs_ref[:, g:g + 1]
            o_ref[ks, MIX_C + g * 128:MIX_C + (g + 1) * 128] = (u[ks, gs] * s).astype(o_ref.dtype)


def _odd_mixer_prompt(prev16, x, w_in, sg, sb, wp, scale, ws, bs_t, batch, seq, start, rows=ROW_TILE):
    rows = min(rows, seq)
    nt = seq // rows
    hal = prev16.shape[1]
    per = rows // hal
    row = lambda wd: pl.BlockSpec((rows, wd), lambda b, t: (b * nt + t, 0))
    return pl.pallas_call(
        functools.partial(_odd_mixer_kernel, start=start),
        grid=(batch, nt),
        in_specs=[pl.BlockSpec((None, hal, MIX_C), lambda b, t: (b, 0, 0)),
                  pl.BlockSpec((hal, D_MODEL), lambda b, t: (jnp.maximum((b * nt + t) * per - 1, 0), 0)),
                  row(D_MODEL), _const_spec(w_in.shape), _const_spec(sg.shape), _const_spec(sb.shape),
                  _const_spec(wp.shape), _const_spec(scale.shape), _const_spec(ws.shape),
                  _const_spec(bs_t.shape)],
        out_specs=[row(1024), row(MIX_C)],
        out_shape=[jax.ShapeDtypeStruct((batch * seq, 1024), BF),
                   jax.ShapeDtypeStruct((batch * seq, MIX_C), F32)],
        scratch_shapes=[pltpu.VMEM((hal + rows, MIX_C), F32)],
        compiler_params=_params(2, 32),
        name="odd_mixer_prompt",
    )(prev16, x, x, w_in, sg, sb, wp, scale, ws, bs_t)


def _pool_sgu_sample_kernel(ws_ref, bs_ref, prev_ref, xc_ref, u_ref, vn_ref, wp_ref, sc_ref, o_ref,
                            *, start):
    nprev = prev_ref.shape[0]
    ntok = xc_ref.shape[0]
    for t in range(ntok):
        for g, w in enumerate(POOL_WINDOWS):
            gs = slice(g * C_GROUP, (g + 1) * C_GROUP)
            acc = None
            for s in range(w):
                j = nprev + t - s
                slab = prev_ref[j, :, gs] if j < nprev else xc_ref[j - nprev, :, gs]
                acc = slab if acc is None else acc + slab
            cnt = float(min(w, start + t + 1))
            pooled = acc / cnt - xc_ref[t, :, gs]
            o_ref[t, :, gs] = _nn(pooled.astype(BF), wp_ref[g]) * sc_ref[:, gs]
        for g in range(D_GROUPS):
            gs = slice(g * 128, (g + 1) * 128)
            s = None
            for j in range(t + 1):
                term = ws_ref[(g * ntok + t) * ntok + j] * vn_ref[j, :, gs]
                s = term if s is None else s + term
            s = s + bs_ref[g * ntok + t]
            o_ref[t, :, MIX_C + g * 128:MIX_C + (g + 1) * 128] = u_ref[t, :, gs] * s


def _pool_sgu_sample(ws_small, bs_small, prev_t, xc_t, u_t, vn_t, wp, scale, start):
    ntok, nreq, _ = xc_t.shape
    smem = pl.BlockSpec(memory_space=pltpu.SMEM)
    return pl.pallas_call(
        functools.partial(_pool_sgu_sample_kernel, start=start),
        grid=(1,),
        in_specs=[smem, smem, _const_spec(prev_t.shape), _const_spec(xc_t.shape), _const_spec(u_t.shape),
                  _const_spec(vn_t.shape), _const_spec(wp.shape), _const_spec(scale.shape)],
        out_specs=_const_spec((ntok, nreq, 1024)),
        out_shape=jax.ShapeDtypeStruct((ntok, nreq, 1024), F32),
        compiler_params=_params(1, 32),
        name="pool_sgu_sample",
    )(ws_small, bs_small, prev_t, xc_t, u_t, vn_t, wp, scale)


def _even_weights(w_in):
    q_a, k_a, v_a, q_b, k_b, v_b, q_i, k_i, w_i = jnp.split(
        w_in, [512, 1024, 1536, 2048, 2176, 2304, 2816, 2880], axis=1)
    wa = jnp.concatenate([k_a.reshape(D_MODEL, H_A, 2 * DA), v_a.reshape(D_MODEL, H_A, DV_A)],
                         -1).reshape(D_MODEL, H_A * (2 * DA + DV_A))
    wb = jnp.concatenate([k_b, v_b, k_i], 1)
    wq = jnp.concatenate([q_a, q_b, q_i], 1)
    ww = jnp.concatenate([w_i, jnp.zeros((D_MODEL, 128 - HI), w_in.dtype)], 1)
    qscale = jnp.concatenate([jnp.full((512,), DA ** -0.5, F32), jnp.full((512,), DB ** -0.5, F32),
                              jnp.full((512,), DI ** -0.5, F32)]).reshape(1, 1536)
    wwt = jnp.concatenate([w_i.T, jnp.zeros((16 - HI, D_MODEL), w_in.dtype)], 0)
    natural = (wa.astype(BF), wb.astype(BF), wq.astype(BF), ww.astype(BF), qscale)
    feature_major = (wq.T.astype(BF), v_a.T.astype(BF), v_b.T.astype(BF), wwt.astype(BF))
    return natural, feature_major


def _a_pages(x):
    n, page, _ = x.shape
    return x.reshape(n, page, H_A, 2, 128).transpose(0, 1, 3, 2, 4).reshape(n, page * 2 * H_A, 128)


def _pad_rows(x, rows):
    return jnp.pad(x, ((0, 0), (0, rows - x.shape[1]), (0, 0)))


def kernel(x_prompt, x_sample, cache_a, cache_b, state_pool, page_table, w_in_e, lam_e, subln_g, w_out_e,
           w_in_o, w_pool, pool_scale, sgu_g, sgu_b, w_s, b_s, w_out_o, w_mlp1, w_mlp2, ln_g, ln_b):
    batch, seq, _ = x_prompt.shape
    nreq, ntok, _ = x_sample.shape
    npg = page_table.shape[1]
    page = cache_a.shape[2]
    past = npg * page
    xp = x_prompt.reshape(batch * seq, D_MODEL)
    xs = x_sample.reshape(nreq * ntok, D_MODEL)
    outs = {k: [] for k in ("a_p", "b_p", "pool_p", "a_s", "b_s", "pool_s", "v_s")}
    rpad = 8

    for l in range(DEPTH):
        i = l // 2
        row2 = lambda v: v.reshape(1, -1)
        if l % 2 == 0:
            lam_init = 0.8 - 0.6 * math.exp(-0.3 * l)
            (wa, wb, wq, ww, qscale), (wqt, wvat, wvbt, wwt) = _even_weights(w_in_e[i])
            g = row2(subln_g[i])
            w_out = w_out_e[i].astype(BF)
            na, nb, abf, bbf, qt, vat, vbt, wit = _proj_even_t(xp, wa, wb, wqt, wvat, wvbt, wwt)
            o_a = _diff_attn_prompt(lam_e[i], subln_g[i].reshape(-1, 1), qt, abf, vat, batch, seq, lam_init)
            o_b = _sparse_attn_prompt(qt, wit, bbf, vbt, batch, seq, min(TOPK_MAX, seq // 4))
            mix_p = [o_a, o_b]
            outs["a_p"].append(na.reshape(batch, seq, 2, H_A, 128).transpose(0, 1, 3, 2, 4)
                               .reshape(batch, seq, H_A, 2 * DA + DV_A))
            outs["b_p"].append(nb.reshape(batch, seq, 2 * DB + DI))
            nas, nbs, _, _, qs, wis = _proj_even(xs, wa, wb, wq, ww, qscale)
            o_s = _sample_even(
                page_table, lam_e[i], g,
                _pad_rows(qs.astype(F32).reshape(nreq, ntok, 1536), rpad),
                _pad_rows(wis.reshape(nreq, ntok, 128), rpad),
                _a_pages(_pad_rows(nas.reshape(nreq, ntok, 1024), rpad)),
                _pad_rows(nbs.reshape(nreq, ntok, 320), page).transpose(0, 2, 1),
                _a_pages(cache_a[i].reshape(-1, page, 1024)), cache_b[i].transpose(0, 2, 1),
                min(TOPK_MAX, (past + ntok) // 4), lam_init)
            mix_s = [o_s[:, :ntok].reshape(nreq * ntok, 1024)]
            outs["a_s"].append(nas.reshape(nreq, ntok, H_A, 2 * DA + DV_A))
            outs["b_s"].append(nbs.reshape(nreq, ntok, 2 * DB + DI))
        else:
            w_in = w_in_o[i].astype(BF)
            w_out = w_out_o[i].astype(BF)
            wp = w_pool[i].astype(BF)
            scale = row2(pool_scale[i])
            sg, sb = row2(sgu_g[i]), row2(sgu_b[i])
            prev16 = jnp.zeros((batch, POOL_BUF + 1, MIX_C), F32)
            m_p, xc = _odd_mixer_prompt(prev16, xp, w_in, sg, sb, wp, scale, w_s[i], b_s[i].T, batch, seq, 0)
            mix_p = [m_p]
            outs["pool_p"].append(xc.reshape(batch, seq, MIX_C)[:, seq - POOL_BUF:])
            xcs, us, vns = _proj_odd(xs, w_in, sg, sb)
            tmaj = lambda v: v.reshape(nreq, ntok, -1).transpose(1, 0, 2)
            m_t = _pool_sgu_sample(
                w_s[i][:, :ntok, :ntok].reshape(-1), b_s[i][:, :ntok].reshape(-1),
                state_pool[i].transpose(1, 0, 2), tmaj(xcs), tmaj(us), tmaj(vns), wp, scale, past)
            mix_s = [m_t.transpose(1, 0, 2).reshape(nreq * ntok, 1024)]
            ext = jnp.concatenate([state_pool[i], xcs.reshape(nreq, ntok, MIX_C)], 1)
            outs["pool_s"].append(ext[:, ext.shape[1] - POOL_BUF:])
            outs["v_s"].append(vns.reshape(nreq, ntok, MIX_D))
        g0, b0, g1, b1 = row2(ln_g[l, 0]), row2(ln_b[l, 0]), row2(ln_g[l, 1]), row2(ln_b[l, 1])
        w1, w2 = w_mlp1[l].astype(BF), w_mlp2[l].astype(BF)
        xp = _layer_tail(mix_p, w_out, xp, g0, b0, w1, w2, g1, b1)
        xs = _layer_tail(mix_s, w_out, xs, g0, b0, w1, w2, g1, b1)

    st = lambda k: jnp.stack(outs[k])
    return (xp.reshape(batch, seq, D_MODEL), xs.reshape(nreq, ntok, D_MODEL), st("a_p"), st("b_p"),
            st("pool_p"), st("a_s"), st("b_s"), st("pool_s"), st("v_s"))
```

```python
import functools
import math

import jax
import jax.numpy as jnp
from jax import lax
from jax.experimental import pallas as pl
from jax.experimental.pallas import tpu as pltpu

D_MODEL = 1024
H_A = 4
DA = 64
DV_A = 128
H_B = 4
DB = 128
HI = 8
DI = 64
TOPK_MAX = 256
MIX_C = 512
MIX_D = 512
POOL_WINDOWS = (2, 4, 8, 16)
C_GROUP = 128
POOL_BUF = 15
CHUNK = 128
D_GROUPS = 4
D_FF = 4096
DEPTH = 2
ALPHA = (2 * DEPTH) ** 0.25
EPS = 1e-5

BF = jnp.bfloat16
F32 = jnp.float32
NEG = -1e30
INT_MIN = -(2 ** 31)
FLT_LOWEST = -3.4028234663852886e38
MANY = 1e9
LOG2E = math.log2(math.e)
V_ROWS = 144
VMEM_LIMIT_BYTES = 56 * 1024 * 1024
ROW_TILE = 512
FF_CHUNK = 512
DIFF_QUERY_TILE = 512
DIFF_KEY_TILE = 256
SPARSE_QUERY_TILE = 256
SPARSE_KEY_TILE = 256


def _params(n_axes, vmem_mib):
    assert vmem_mib * 1024 * 1024 <= VMEM_LIMIT_BYTES
    return pltpu.CompilerParams(dimension_semantics=("arbitrary",) * n_axes,
                                vmem_limit_bytes=vmem_mib * 1024 * 1024)


def _nn(a, b):
    return jnp.dot(a, b, preferred_element_type=F32)


def _nt(a, b):
    return lax.dot_general(a, b, (((1,), (1,)), ((), ())), preferred_element_type=F32)


def _ln(z, g, b):
    mu = jnp.mean(z, -1, keepdims=True)
    d = z - mu
    var = jnp.mean(d * d, -1, keepdims=True)
    return d * lax.rsqrt(var + EPS) * g + b


def _const_spec(shape):
    nd = len(shape)
    return pl.BlockSpec(shape, lambda *_: (0,) * nd)


def _key_to_float(k):
    return lax.bitcast_convert_type(jnp.where(k < 0, k ^ jnp.int32(0x7FFFFFFF), k), F32)


def _kth_largest(count, shape, ksel, bits=1, n_finite=None, n_nonneg=None, count_hi=None, unrolled=False):
    if n_finite is None:
        n_finite, = count([jnp.full(shape, -jnp.inf, F32)], True)
    if n_nonneg is None:
        n_nonneg, = count([jnp.zeros(shape, F32)], False)
    nonneg = n_nonneg >= ksel
    start = (jnp.where(nonneg, jnp.int32(0), jnp.int32(INT_MIN)), jnp.where(nonneg, n_nonneg, MANY),
             jnp.where(nonneg, 0.0, n_nonneg))

    def refine(carry, shift, nbits):
        k, n_ge, n_above = carry
        cands = [k | lax.shift_left(jnp.int32(d), shift) for d in range(1, 2 ** nbits)]
        prev_ok = None
        for cand, n_cand in zip(cands, count([_key_to_float(c) for c in cands], False)):
            ok = n_cand >= ksel
            k, n_ge = jnp.where(ok, cand, k), jnp.where(ok, n_cand, n_ge)
            first_miss = jnp.logical_not(ok) if prev_ok is None else jnp.logical_and(prev_ok, jnp.logical_not(ok))
            n_above = jnp.where(first_miss, n_cand, n_above)
            prev_ok = ok
        return k, n_ge, n_above

    if count_hi is None:
        npass, rest = divmod(31, bits)
        if unrolled:
            carry = start
            for it in range(npass):
                carry = refine(carry, jnp.int32(31 - bits - bits * it), bits)
        else:
            carry = lax.fori_loop(0, npass, lambda it, cr: refine(cr, jnp.int32(31 - bits) - bits * it, bits),
                                  start)
        if rest:
            carry = refine(carry, jnp.int32(0), rest)
    else:
        def guess(it, k):
            cand = k | lax.shift_left(jnp.int32(1), jnp.int32(30) - it)
            return jnp.where(count_hi(cand) >= ksel, cand, k)

        k_hi = lax.fori_loop(0, 15, guess, start[0])
        top = k_hi + jnp.int32(1 << 16)
        n_lo, = count([_key_to_float(k_hi)], False)
        n_up, = count([_key_to_float(top)], False)
        n_lo = jnp.where(k_hi == start[0], start[1], n_lo)
        n_up = jnp.where(top < k_hi, 0.0, n_up)
        good = jnp.logical_or(n_finite < ksel, jnp.logical_and(n_lo >= ksel, n_up < ksel))
        carry = lax.cond(
            jnp.min(jnp.where(good, 1.0, 0.0)) > 0.0,
            lambda: (k_hi, n_lo, n_up),
            lambda: lax.fori_loop(0, 15, lambda it, cr: refine(cr, jnp.int32(30) - it, 1), start))
        carry = lax.fori_loop(15, 31, lambda it, cr: refine(cr, jnp.int32(30) - it, 1), carry)
    k, n_ge, n_above = carry
    short = n_finite < ksel
    t = jnp.where(short, FLT_LOWEST, _key_to_float(k))
    need = jnp.where(short, MANY, ksel - n_above)
    tied = jnp.where(short, 0.0, jnp.where(n_ge > ksel, 1.0, 0.0))
    return t, need, tied


def _proj_even_kernel(x_ref, wa_ref, wb_ref, wq_ref, ww_ref, qs_ref,
                      na_ref, nb_ref, abf_ref, bbf_ref, q_ref, wi_ref):
    x = x_ref[...].astype(BF)
    a = _nn(x, wa_ref[...])
    na_ref[...] = a
    abf_ref[...] = a.astype(BF)
    b = _nn(x, wb_ref[...])
    nb_ref[...] = b
    bbf_ref[...] = b.astype(BF)
    q_ref[...] = (_nn(x, wq_ref[...]) * qs_ref[...]).astype(BF)
    wi_ref[...] = _nn(x, ww_ref[...]) * (HI ** -0.5)


def _proj_even(x, wa, wb, wq, ww, qscale, tm=ROW_TILE):
    n = x.shape[0]
    tm = min(tm, n)
    row = lambda w: pl.BlockSpec((tm, w), lambda i: (i, 0))
    return pl.pallas_call(
        _proj_even_kernel,
        grid=(n // tm,),
        in_specs=[row(D_MODEL), _const_spec(wa.shape), _const_spec(wb.shape),
                  _const_spec(wq.shape), _const_spec(ww.shape), _const_spec(qscale.shape)],
        out_specs=[row(1024), row(320), row(1024), row(320), row(1536), row(128)],
        out_shape=[jax.ShapeDtypeStruct((n, 1024), F32), jax.ShapeDtypeStruct((n, 320), F32),
                   jax.ShapeDtypeStruct((n, 1024), BF), jax.ShapeDtypeStruct((n, 320), BF),
                   jax.ShapeDtypeStruct((n, 1536), BF), jax.ShapeDtypeStruct((n, 128), F32)],
        compiler_params=_params(1, 32),
        name="proj_even",
    )(x, wa, wb, wq, ww, qscale)


def _proj_even_t_kernel(x_ref, wa_ref, wb_ref, wqt_ref, wvat_ref, wvbt_ref, wwt_ref,
                        na_ref, nb_ref, abf_ref, bbf_ref, qt_ref, vat_ref, vbt_ref, wit_ref):
    x = x_ref[...]
    xb = x.astype(BF)
    xt = x.T.astype(BF)
    a = _nn(xb, wa_ref[...])
    tm = a.shape[0]
    for h in range(H_A):
        for part in range(2):
            na_ref[pl.ds(H_A * part + h, tm, stride=2 * H_A), :] = a[:, h * 256 + part * 128:h * 256 + (part + 1) * 128]
    abf_ref[...] = a.astype(BF)
    b = _nn(xb, wb_ref[...])
    nb_ref[...] = b
    bbf_ref[...] = b.astype(BF)
    qt = _nn(wqt_ref[...], xt)
    qt_ref[0:512, :] = (qt[0:512] * (DA ** -0.5 * LOG2E)).astype(BF)
    qt_ref[512:1024, :] = (qt[512:1024] * (DB ** -0.5 * LOG2E)).astype(BF)
    qt_ref[1024:1536, :] = (qt[1024:1536] * (DI ** -0.5)).astype(BF)
    pad = V_ROWS - 128
    ones_rows = jnp.where(lax.broadcasted_iota(jnp.int32, (pad, tm), 0) == 0, 1.0, 0.0).astype(BF)
    va = _nn(wvat_ref[...], xt).astype(BF)
    for h in range(H_A):
        vat_ref[h * V_ROWS:h * V_ROWS + 128, :] = va[h * 128:(h + 1) * 128]
        vat_ref[h * V_ROWS + 128:(h + 1) * V_ROWS, :] = ones_rows
    vbt_ref[0:128, :] = _nn(wvbt_ref[...], xt).astype(BF)
    vbt_ref[128:V_ROWS, :] = ones_rows
    wit_ref[...] = _nn(wwt_ref[...], xt) * (HI ** -0.5)


def _proj_even_t(x, wa, wb, wqt, wvat, wvbt, wwt, tm=ROW_TILE):
    n = x.shape[0]
    row = lambda w: pl.BlockSpec((tm, w), lambda i: (i, 0))
    col = lambda h: pl.BlockSpec((h, tm), lambda i: (0, i))
    return pl.pallas_call(
        _proj_even_t_kernel,
        grid=(n // tm,),
        in_specs=[row(D_MODEL)] + [_const_spec(w.shape) for w in (wa, wb, wqt, wvat, wvbt, wwt)],
        out_specs=[pl.BlockSpec((2 * H_A * tm, 128), lambda i: (i, 0)), row(320), row(1024), row(320),
                   col(1536), col(H_A * V_ROWS), col(V_ROWS), col(16)],
        out_shape=[jax.ShapeDtypeStruct((2 * H_A * n, 128), F32), jax.ShapeDtypeStruct((n, 320), F32),
                   jax.ShapeDtypeStruct((n, 1024), BF), jax.ShapeDtypeStruct((n, 320), BF),
                   jax.ShapeDtypeStruct((1536, n), BF), jax.ShapeDtypeStruct((H_A * V_ROWS, n), BF),
                   jax.ShapeDtypeStruct((V_ROWS, n), BF), jax.ShapeDtypeStruct((16, n), F32)],
        compiler_params=_params(1, 40),
        name="proj_even_t",
    )(x, wa, wb, wqt, wvat, wvbt, wwt)


def _lambda(lam_ref, lam_init):
    lp = lam_ref[...]
    return (jnp.exp(jnp.sum(lp[0:1] * lp[1:2], axis=-1, keepdims=True))
            - jnp.exp(jnp.sum(lp[2:3] * lp[3:4], axis=-1, keepdims=True)) + lam_init)


def _split_q12(q):
    qf = q.astype(F32)
    lane = lax.broadcasted_iota(jnp.int32, qf.shape, 1)
    return jnp.concatenate([jnp.where(lane < DA, qf, 0.0), jnp.where(lane >= DA, qf, 0.0)],
                           axis=0).astype(BF)


def _subln(o, lam, g, lam_init):
    r = o.shape[0] // 2
    d = o[:r] - lam * o[r:]
    ms = jnp.mean(d * d, -1, keepdims=True)
    return d * lax.rsqrt(ms + EPS) * g * (1.0 - lam_init)


def _kth_largest_rows(sc_ref, nch, tk, ksel):
    rows = sc_ref.shape[0]

    def count(cands, strict):
        accs = [jnp.zeros((rows, tk), F32) for _ in cands]
        for c in range(nch):
            x = sc_ref[:, c * tk:(c + 1) * tk]
            accs = [a + jnp.where((x > cand) if strict else (x >= cand), 1.0, 0.0)
                    for a, cand in zip(accs, cands)]
        return [jnp.sum(a, -1, keepdims=True) for a in accs]

    n_finite, n_nonneg = count([jnp.full((rows, 1), FLT_LOWEST, F32), jnp.zeros((rows, 1), F32)], False)
    return _kth_largest(count, (rows, 1), ksel, 3, n_finite, n_nonneg, unrolled=True)


def _top16(x):
    return lax.bitcast_convert_type(lax.bitcast_convert_type(x, jnp.int32) & jnp.int32(-65536), F32)


def _kth_largest_cols(sc_ref, sc16_ref, nch, tk, ksel, n_finite=None, n_nonneg=None):
    cols = sc_ref.shape[1]

    def count_hi(cand_key):
        cand = _top16(_key_to_float(cand_key)).astype(BF)

        def body(c, acc):
            x = sc16_ref[pl.ds(pl.multiple_of(c * tk, tk), tk), :]
            hit = jnp.where(x >= cand, jnp.ones((), BF), jnp.zeros((), BF))
            parts = [hit[r * 32:(r + 1) * 32] for r in range(tk // 32)]
            while len(parts) > 1:
                parts = [a + b for a, b in zip(parts[0::2], parts[1::2])]
            return acc + parts[0]
        acc = lax.fori_loop(0, nch, body, jnp.zeros((32, cols), BF))
        return jnp.sum(acc.astype(F32), 0, keepdims=True)

    def count(cands, strict):
        cand, = cands

        def body(c, acc):
            x = sc_ref[pl.ds(pl.multiple_of(c * tk, tk), tk), :]
            hit = (x > cand) if strict else (x >= cand)
            return acc + jnp.sum(jnp.where(hit, 1.0, 0.0).reshape(tk // 32, 32, cols), axis=0)
        acc = lax.fori_loop(0, nch, body, jnp.zeros((32, cols), F32))
        return [jnp.sum(acc, 0, keepdims=True)]

    return _kth_largest(count, (1, cols), ksel, 1, n_finite, n_nonneg, count_hi)


def _selected(x, t, need, eq_before, tri):
    eq = jnp.where(x == t, 1.0, 0.0)
    rank = eq_before + _nn(eq.astype(BF), tri)
    tie_taken = jnp.where(rank <= need, eq, 0.0)
    return jnp.where(x > t, 1.0, tie_taken), jnp.sum(eq, -1, keepdims=True)


def _tri(n, lower):
    r = lax.broadcasted_iota(jnp.int32, (n, n), 0)
    c = lax.broadcasted_iota(jnp.int32, (n, n), 1)
    return jnp.where((r >= c) if lower else (r <= c), 1.0, 0.0).astype(BF)


def _flash_key_major(npair, nlast, nchain, tq, qk, mask, v_t, s_ref, p_ref, al_ref, acc_ref):
    def scores(c, slot):
        for j in range(nchain):
            s_ref[slot, j] = qk(c, j)

    def softmax(c, slot, ms, last):
        ms_new = []
        for j in range(nchain):
            s = mask(c, j, s_ref[slot, j], last)
            m_new = jnp.maximum(ms[j], jnp.max(s, 0, keepdims=True))
            p_ref[slot, j] = jnp.exp2(s - m_new).astype(BF)
            al_ref[slot, j] = jnp.exp2(ms[j] - m_new)
            ms_new.append(m_new)
        return tuple(ms_new)

    def fold(c, slot):
        for j in range(nchain):
            acc_ref[j] = al_ref[slot, j] * acc_ref[j] + _nn(v_t(c, j), p_ref[slot, j])

    acc_ref[...] = jnp.zeros(acc_ref.shape, F32)
    p_ref[1] = jnp.zeros(p_ref.shape[1:], BF)
    al_ref[1] = jnp.ones(al_ref.shape[1:], F32)
    scores(0, 0)

    def pair(c, carry, last, final):
        fold(jnp.maximum(c - 1, 0), 1)
        carry = softmax(c, 0, carry, last)
        scores(c + 1, 1)
        fold(c, 0)
        carry = softmax(c + 1, 1, carry, last)
        if not final:
            scores(c + 2, 0)
        return carry

    n_plain = npair - nlast
    carry = lax.fori_loop(0, n_plain, lambda c, cr: pair(2 * c, cr, False, False),
                          tuple(jnp.full((1, tq), NEG, F32) for _ in range(nchain)))
    for d in range(nlast):
        carry = pair(2 * (n_plain + d), carry, True, d == nlast - 1)
    fold(2 * npair - 1, 1)


def _diff_kernel(lam_ref, g_ref, qt_ref, a_ref, vt_ref, o_ref, qh_ref, s_ref, p_ref, al_ref, acc_ref,
                 *, tq, tk, lam_init):
    i = pl.program_id(1)
    lam = _lambda(lam_ref, lam_init)
    sub = lax.broadcasted_iota(jnp.int32, (128, tq), 0)
    key = lax.broadcasted_iota(jnp.int32, (tk, 2 * tq), 0)
    qry = i * tq + lax.broadcasted_iota(jnp.int32, (tk, 2 * tq), 1) % tq
    for h in range(H_A):
        qt = qt_ref[h * 128:(h + 1) * 128, :].astype(F32)
        qh_ref[h, :, 0:tq] = jnp.where(sub < DA, qt, 0.0).astype(BF)
        qh_ref[h, :, tq:2 * tq] = jnp.where(sub >= DA, qt, 0.0).astype(BF)

    def qk(c, h):
        return _nn(a_ref[pl.ds(pl.multiple_of(c * tk, tk), tk), h * 256:h * 256 + 128], qh_ref[h])

    def mask(c, h, s, last):
        return jnp.where(c * tk + key <= qry, s, NEG) if last else s

    def v_t(c, h):
        return vt_ref[h * V_ROWS:(h + 1) * V_ROWS, pl.ds(pl.multiple_of(c * tk, tk), tk)]

    nlast = tq // (2 * tk)
    _flash_key_major((i + 1) * nlast, nlast, H_A, 2 * tq, qk, mask, v_t, s_ref, p_ref, al_ref, acc_ref)
    for h in range(H_A):
        o12 = acc_ref[h, 0:DV_A, :] * (1.0 / acc_ref[h, DV_A:DV_A + 1, :])
        d = o12[:, 0:tq] - lam * o12[:, tq:2 * tq]
        ms = jnp.mean(d * d, 0, keepdims=True)
        o = d * lax.rsqrt(ms + EPS) * g_ref[...] * (1.0 - lam_init)
        o_ref[:, h * 128:(h + 1) * 128] = o.T.astype(o_ref.dtype)


def _flash_scratch(nchain, feat, tk, tq):
    return [pltpu.VMEM((2, nchain, tk, tq), F32), pltpu.VMEM((2, nchain, tk, tq), BF),
            pltpu.VMEM((2, nchain, 1, tq), F32), pltpu.VMEM((nchain, feat, tq), F32)]


def _diff_attn_prompt(lam_e, g_col, qt, a_bf, vat, batch, seq, lam_init, tq=DIFF_QUERY_TILE, tk=DIFF_KEY_TILE):
    nq = seq // tq
    return pl.pallas_call(
        functools.partial(_diff_kernel, tq=tq, tk=tk, lam_init=lam_init),
        grid=(batch, nq),
        in_specs=[_const_spec(lam_e.shape), _const_spec(g_col.shape),
                  pl.BlockSpec((512, tq), lambda b, i: (0, b * nq + i)),
                  pl.BlockSpec((seq, 1024), lambda b, i: (b, 0)),
                  pl.BlockSpec((H_A * V_ROWS, seq), lambda b, i: (0, b))],
        out_specs=pl.BlockSpec((tq, 512), lambda b, i: (b * nq + i, 0)),
        out_shape=jax.ShapeDtypeStruct((batch * seq, 512), BF),
        scratch_shapes=[pltpu.VMEM((H_A, 128, 2 * tq), BF)] + _flash_scratch(H_A, V_ROWS, tk, 2 * tq),
        compiler_params=_params(2, 52),
        name="diff_attn_prompt",
    )(lam_e, g_col, qt, a_bf, vat)


def _attn_chunk(tq, tk):
    return tk if (tq // tk) % 2 == 0 else tk // 2


def _sparse_kernel(qbt_ref, qit_ref, wit_ref, b_ref, vbt_ref, o_ref, sc_ref, sc16_ref, qiw_ref, qbw_ref,
                   s_ref, p_ref, al_ref, acc_ref, *, tq, tk, ksel):
    i = pl.program_id(1)
    nch = (i + 1) * (tq // tk)
    ta = tk // 2
    key = lax.broadcasted_iota(jnp.int32, (ta, tq), 0)
    qry = i * tq + lax.broadcasted_iota(jnp.int32, (ta, tq), 1)
    w = wit_ref[...]
    for h in range(HI):
        qiw_ref[:, h * tq:(h + 1) * tq] = qit_ref[h * DI:(h + 1) * DI, :]
    for h in range(H_B):
        qbw_ref[:, h * tq:(h + 1) * tq] = qbt_ref[h * DB:(h + 1) * DB, :]
    heads = lambda x: jnp.concatenate([x] * H_B, axis=1)

    def score_chunk(c, counts):
        n_finite, n_nonneg = counts
        for u in range(2):
            r0 = pl.multiple_of(c * tk + u * ta, ta)
            d = jnp.maximum(_nn(b_ref[pl.ds(r0, ta), 2 * DB:2 * DB + DI], qiw_ref[...]), 0.0)
            acc = w[0:1, :] * d[:, 0:tq]
            for h in range(1, HI):
                acc = acc + w[h:h + 1, :] * d[:, h * tq:(h + 1) * tq]
            sc = jnp.where(r0 + key <= qry, acc, -jnp.inf)
            sc_ref[pl.ds(r0, ta), :] = sc
            sc16_ref[pl.ds(r0, ta), :] = _top16(sc).astype(BF)
            n_finite = n_finite + jnp.sum(jnp.where(sc > -jnp.inf, 1.0, 0.0), 0, keepdims=True)
            n_nonneg = n_nonneg + jnp.sum(jnp.where(sc >= 0.0, 1.0, 0.0), 0, keepdims=True)
        return n_finite, n_nonneg

    zero = jnp.zeros((1, tq), F32)
    n_finite, n_nonneg = lax.fori_loop(0, nch, score_chunk, (zero, zero))
    t, need, tied = _kth_largest_cols(sc_ref, sc16_ref, nch, tk, ksel, n_finite, n_nonneg)

    tv = _attn_chunk(tq, tk)

    def qk(c, _):
        return _nn(b_ref[pl.ds(pl.multiple_of(c * tv, tv), tv), 0:DB], qbw_ref[...])

    def v_t(c, _):
        return vbt_ref[:, pl.ds(pl.multiple_of(c * tv, tv), tv)]

    def to_bias(exact_ties):
        def body(c, eq_before):
            for u in range(2):
                rows = pl.ds(pl.multiple_of(c * tk + u * ta, ta), ta)
                x = sc_ref[rows, :]
                if exact_ties:
                    eq = jnp.where(x == t, 1.0, 0.0)
                    rank = eq_before + _nn(_tri(ta, True), eq.astype(BF))
                    sc_ref[rows, :] = jnp.where(
                        x > t, 0.0, jnp.where(x == t, jnp.where(rank <= need, 0.0, NEG), NEG))
                    eq_before = eq_before + jnp.sum(eq, 0, keepdims=True)
                else:
                    sc_ref[rows, :] = jnp.where(x >= t, 0.0, NEG)
            return eq_before
        return lax.fori_loop(0, nch, body, jnp.zeros((1, tq), F32))

    lax.cond(jnp.max(tied) > 0.0, lambda: to_bias(True), lambda: to_bias(False))

    def add_bias(c, _, s, last):
        return s + heads(sc_ref[pl.ds(pl.multiple_of(c * tv, tv), tv), :])

    _flash_key_major(nch * tk // (2 * tv), 1, 1, H_B * tq, qk, add_bias, v_t, s_ref, p_ref, al_ref, acc_ref)
    o = acc_ref[0, 0:DB, :] * (1.0 / acc_ref[0, DB:DB + 1, :])
    for h in range(H_B):
        o_ref[:, h * DB:(h + 1) * DB] = o[:, h * tq:(h + 1) * tq].T.astype(o_ref.dtype)


def _sparse_attn_prompt(qt, wit, b_bf, vbt, batch, seq, ksel, tq=SPARSE_QUERY_TILE, tk=SPARSE_KEY_TILE):
    assert seq % tq == 0 and tq % tk == 0
    assert seq // 32 <= 256, "bfloat16 hit counters of the selection hold integers up to 256 only"
    nq = seq // tq
    return pl.pallas_call(
        functools.partial(_sparse_kernel, tq=tq, tk=tk, ksel=ksel),
        grid=(batch, nq),
        in_specs=[pl.BlockSpec((512, tq), lambda b, i: (1, b * nq + i)),
                  pl.BlockSpec((512, tq), lambda b, i: (2, b * nq + i)),
                  pl.BlockSpec((16, tq), lambda b, i: (0, b * nq + i)),
                  pl.BlockSpec((seq, 320), lambda b, i: (b, 0)),
                  pl.BlockSpec((V_ROWS, seq), lambda b, i: (0, b))],
        out_specs=pl.BlockSpec((tq, 512), lambda b, i: (b * nq + i, 0)),
        out_shape=jax.ShapeDtypeStruct((batch * seq, 512), BF),
        scratch_shapes=[pltpu.VMEM((seq, tq), F32), pltpu.VMEM((seq, tq), BF), pltpu.VMEM((DI, HI * tq), BF),
                        pltpu.VMEM((DB, H_B * tq), BF)] + _flash_scratch(1, V_ROWS, _attn_chunk(tq, tk), H_B * tq),
        compiler_params=_params(2, 32),
        name="sparse_attn_prompt",
    )(qt, qt, wit, b_bf, vbt)


def _sample_even_kernel(pt_ref, lam_ref, g_ref, q_ref, wi_ref, *rest, npg, page, ksel, lam_init):
    del pt_ref
    nblk = npg + 1
    a_pages = rest[:nblk]
    b_pages = rest[nblk:2 * nblk]
    o_ref = rest[2 * nblk]
    sc_ref = rest[2 * nblk + 1]
    r = q_ref.shape[0]
    lam = _lambda(lam_ref, lam_init)
    q = q_ref[...]

    def a_rows(p, j):
        n_pos = a_pages[p].shape[0] // (2 * H_A)
        x = a_pages[p][pl.ds(j, n_pos, stride=2 * H_A), :]
        if n_pos < page:
            x = jnp.concatenate([x, jnp.zeros((page - n_pos, 128), F32)], axis=0)
        return x.astype(BF)

    def new_ok(rows):
        tok = lax.broadcasted_iota(jnp.int32, (rows, page), 0) % r
        return lax.broadcasted_iota(jnp.int32, (rows, page), 1) <= tok

    ok2 = new_ok(2 * r)
    for h in range(H_A):
        qq = _split_q12(q[:, h * 128:(h + 1) * 128])
        ss = [_nt(qq, a_rows(p, h)) for p in range(nblk)]
        ss[npg] = jnp.where(ok2, ss[npg], NEG)
        m = functools.reduce(jnp.maximum, ss)
        m = jnp.max(m, -1, keepdims=True)
        ps = [jnp.exp(s - m) for s in ss]
        l = jnp.sum(functools.reduce(lambda x, y: x + y, ps), -1, keepdims=True)
        acc = functools.reduce(lambda x, y: x + y, [
            _nn(ps[p].astype(BF), a_rows(p, H_A + h))
            for p in range(nblk)])
        o_ref[:, h * 128:(h + 1) * 128] = _subln(acc / l, lam, g_ref[...], lam_init)

    qi = jnp.concatenate([q[:, 1024 + h * DI:1024 + (h + 1) * DI] for h in range(HI)], axis=0).astype(BF)
    w = wi_ref[...]
    wcol = jnp.concatenate([w[:, h:h + 1] for h in range(HI)], axis=0)
    ok1 = new_ok(r)
    for p in range(nblk):
        d = jnp.maximum(_nn(qi, b_pages[p][2 * DB:2 * DB + DI, :].astype(BF)), 0.0) * wcol
        sc = d[0:r]
        for h in range(1, HI):
            sc = sc + d[h * r:(h + 1) * r]
        if p == npg:
            sc = jnp.where(ok1, sc, -jnp.inf)
        sc_ref[:, p * page:(p + 1) * page] = sc
    t, need, _ = _kth_largest_rows(sc_ref, nblk, page, ksel)

    tri = _tri(page, False)
    q4 = jnp.concatenate([q[:, 512 + h * DB:512 + (h + 1) * DB] for h in range(H_B)], axis=0).astype(BF)
    eq_before = jnp.zeros((r, 1), F32)
    ss, sels = [], []
    for p in range(nblk):
        sel, n_eq = _selected(sc_ref[:, p * page:(p + 1) * page], t, need, eq_before, tri)
        eq_before = eq_before + n_eq
        if p == npg:
            sel = jnp.where(ok1, sel, 0.0)
        sels.append(jnp.concatenate([sel] * H_B, axis=0) > 0.5)
        ss.append(_nn(q4, b_pages[p][0:DB, :].astype(BF)))
    m = functools.reduce(jnp.maximum, [jnp.where(sl, s, NEG) for sl, s in zip(sels, ss)])
    m = jnp.max(m, -1, keepdims=True)
    ps = [jnp.where(sl, jnp.exp(s - m), 0.0) for sl, s in zip(sels, ss)]
    l = jnp.sum(functools.reduce(lambda x, y: x + y, ps), -1, keepdims=True)
    acc = functools.reduce(lambda x, y: x + y, [
        _nt(ps[p].astype(BF), b_pages[p][DB:2 * DB, :].astype(BF)) for p in range(nblk)])
    o = acc / l
    for h in range(H_B):
        o_ref[:, 512 + h * DB:512 + (h + 1) * DB] = o[h * r:(h + 1) * r]


def _sample_even(page_table, lam_e, g, qs, wis, anew_pg, bnew_pg, cache_a_pg, cache_b_pg, ksel, lam_init):
    nreq, npg = page_table.shape
    page = cache_b_pg.shape[2]
    r = qs.shape[1]
    req = lambda a: pl.BlockSpec((None,) + a.shape[1:], lambda i, pt: (i, 0, 0))

    def page_spec(a, p):
        return pl.BlockSpec((None,) + a.shape[1:], lambda i, pt, p=p: (pt[i, p], 0, 0))

    in_specs = [_const_spec(lam_e.shape), _const_spec(g.shape), req(qs), req(wis)]
    in_specs += [page_spec(cache_a_pg, p) for p in range(npg)] + [req(anew_pg)]
    in_specs += [page_spec(cache_b_pg, p) for p in range(npg)] + [req(bnew_pg)]
    grid_spec = pltpu.PrefetchScalarGridSpec(
        num_scalar_prefetch=1, grid=(nreq,), in_specs=in_specs,
        out_specs=pl.BlockSpec((None, r, 1024), lambda i, pt: (i, 0, 0)),
        scratch_shapes=[pltpu.VMEM((r, (npg + 1) * page), F32)])
    return pl.pallas_call(
        functools.partial(_sample_even_kernel, npg=npg, page=page, ksel=ksel, lam_init=lam_init),
        grid_spec=grid_spec,
        out_shape=jax.ShapeDtypeStruct((nreq, r, 1024), F32),
        compiler_params=_params(1, 40),
        name="sample_even",
    )(page_table, lam_e, g, qs, wis, *([cache_a_pg] * npg), anew_pg, *([cache_b_pg] * npg), bnew_pg)


def _layer_tail_kernel(*refs, n_lhs, ck):
    lhs = refs[:n_lhs]
    wo_ref, x_ref, g0_ref, b0_ref, w1_ref, w2_ref, g1_ref, b1_ref, o_ref, acc_ref = refs[n_lhs:]
    y = None
    k0 = 0
    for a_ref in lhs:
        kw = a_ref.shape[1]
        part = _nn(a_ref[...].astype(BF), wo_ref[k0:k0 + kw, :])
        y = part if y is None else y + part
        k0 += kw
    x1 = _ln(ALPHA * x_ref[...] + y, g0_ref[...], b0_ref[...])
    xb = x1.astype(BF)
    for c in range(D_FF // ck):
        h = jnp.maximum(_nn(xb, w1_ref[:, c * ck:(c + 1) * ck]), 0.0)
        part = _nn((h * h).astype(BF), w2_ref[c * ck:(c + 1) * ck, :])
        if c == 0:
            acc_ref[...] = part
        else:
            acc_ref[...] += part
    o_ref[...] = _ln(ALPHA * x1 + acc_ref[...], g1_ref[...], b1_ref[...])


def _layer_tail(lhs, w_out, x, g0, b0, w1, w2, g1, b1, tm=ROW_TILE, ck=FF_CHUNK):
    n = x.shape[0]
    tm = min(tm, n)
    row = lambda wd: pl.BlockSpec((tm, wd), lambda i: (i, 0))
    resident = lambda s: pl.BlockSpec(s, lambda i: (0, 0), pipeline_mode=pl.Buffered(1))
    vec = _const_spec(g0.shape)
    return pl.pallas_call(
        functools.partial(_layer_tail_kernel, n_lhs=len(lhs), ck=ck),
        grid=(n // tm,),
        in_specs=[row(a.shape[1]) for a in lhs] + [resident(w_out.shape), row(D_MODEL), vec, vec,
                                                   resident(w1.shape), resident(w2.shape), vec, vec],
        out_specs=row(D_MODEL),
        out_shape=jax.ShapeDtypeStruct((n, D_MODEL), F32),
        scratch_shapes=[pltpu.VMEM((tm, D_MODEL), F32)],
        compiler_params=_params(1, 48),
        name="layer_tail",
    )(*lhs, w_out, x, g0, b0, w1, w2, g1, b1)


def _gelu(x):
    return 0.5 * x * (1.0 + jnp.tanh(math.sqrt(2.0 / math.pi) * (x + 0.044715 * (x * x * x))))


def _proj_odd_kernel(x_ref, w_ref, g_ref, b_ref, xc_ref, u_ref, vn_ref):
    h = _nn(x_ref[...].astype(BF), w_ref[...])
    xc_ref[...] = h[:, :MIX_C]
    u_ref[...] = _gelu(h[:, MIX_C:MIX_C + MIX_D])
    vn_ref[...] = _ln(_gelu(h[:, MIX_C + MIX_D:]), g_ref[...], b_ref[...])


def _proj_odd(x, w, g, b, tm=ROW_TILE):
    n = x.shape[0]
    tm = min(tm, n)
    row = lambda wd: pl.BlockSpec((tm, wd), lambda i: (i, 0))
    return pl.pallas_call(
        _proj_odd_kernel,
        grid=(n // tm,),
        in_specs=[row(D_MODEL), _const_spec(w.shape), _const_spec(g.shape), _const_spec(b.shape)],
        out_specs=[row(512), row(512), row(512)],
        out_shape=[jax.ShapeDtypeStruct((n, 512), F32)] * 3,
        compiler_params=_params(1, 32),
        name="proj_odd",
    )(x, w, g, b)


def _odd_mixer_kernel(prev_ref, xh_ref, x_ref, w_ref, g_ref, b_ref, wp_ref, sc_ref, ws_ref, bs_ref,
                      o_ref, xc_ref, ext_ref, *, start):
    t = pl.program_id(1)
    hal = prev_ref.shape[0]
    rows = x_ref.shape[0]
    h = _nn(x_ref[...].astype(BF), w_ref[...])
    xc = h[:, :MIX_C]
    u = _gelu(h[:, MIX_C:MIX_C + MIX_D])
    vn = _ln(_gelu(h[:, MIX_C + MIX_D:]), g_ref[...], b_ref[...])
    xc_ref[...] = xc
    halo = _nn(xh_ref[...].astype(BF), w_ref[:, :MIX_C])
    ext_ref[0:hal, :] = jnp.where(t == 0, prev_ref[...], halo)
    ext_ref[hal:hal + rows, :] = xc
    pos = start + t * rows + lax.broadcasted_iota(jnp.int32, (rows, 1), 0)
    for g, w in enumerate(POOL_WINDOWS):
        gs = slice(g * C_GROUP, (g + 1) * C_GROUP)
        acc = ext_ref[hal:hal + rows, gs]
        for s in range(1, w):
            acc = acc + ext_ref[hal - s:hal - s + rows, gs]
        cnt = jnp.minimum(w, pos + 1).astype(F32)
        pooled = acc / cnt - ext_ref[hal:hal + rows, gs]
        c = _nn(pooled.astype(BF), wp_ref[g]) * sc_ref[:, gs]
        o_ref[:, gs] = c.astype(o_ref.dtype)
    r = lax.broadcasted_iota(jnp.int32, (CHUNK, CHUNK), 0)
    cc = lax.broadcasted_iota(jnp.int32, (CHUNK, CHUNK), 1)
    for g in range(D_GROUPS):
        gs = slice(g * 128, (g + 1) * 128)
        ws = jnp.where(r >= cc, ws_ref[g], 0.0).astype(BF)
        for k in range(rows // CHUNK):
            ks = slice(k * CHUNK, (k + 1) * CHUNK)
            s = _nn(ws, vn[ks, gs].astype(BF)) + bs_ref[:, g:g + 1]
            o_ref[ks, MIX_C + g * 128:MIX_C + (g + 1) * 128] = (u[ks, gs] * s).astype(o_ref.dtype)


def _odd_mixer_prompt(prev16, x, w_in, sg, sb, wp, scale, ws, bs_t, batch, seq, start, rows=ROW_TILE):
    rows = min(rows, seq)
    nt = seq // rows
    hal = prev16.shape[1]
    per = rows // hal
    row = lambda wd: pl.BlockSpec((rows, wd), lambda b, t: (b * nt + t, 0))
    return pl.pallas_call(
        functools.partial(_odd_mixer_kernel, start=start),
        grid=(batch, nt),
        in_specs=[pl.BlockSpec((None, hal, MIX_C), lambda b, t: (b, 0, 0)),
                  pl.BlockSpec((hal, D_MODEL), lambda b, t: (jnp.maximum((b * nt + t) * per - 1, 0), 0)),
                  row(D_MODEL), _const_spec(w_in.shape), _const_spec(sg.shape), _const_spec(sb.shape),
                  _const_spec(wp.shape), _const_spec(scale.shape), _const_spec(ws.shape),
                  _const_spec(bs_t.shape)],
        out_specs=[row(1024), row(MIX_C)],
        out_shape=[jax.ShapeDtypeStruct((batch * seq, 1024), BF),
                   jax.ShapeDtypeStruct((batch * seq, MIX_C), F32)],
        scratch_shapes=[pltpu.VMEM((hal + rows, MIX_C), F32)],
        compiler_params=_params(2, 32),
        name="odd_mixer_prompt",
    )(prev16, x, x, w_in, sg, sb, wp, scale, ws, bs_t)


def _pool_sgu_sample_kernel(ws_ref, bs_ref, prev_ref, xc_ref, u_ref, vn_ref, wp_ref, sc_ref, o_ref,
                            *, start):
    nprev = prev_ref.shape[0]
    ntok = xc_ref.shape[0]
    for t in range(ntok):
        for g, w in enumerate(POOL_WINDOWS):
            gs = slice(g * C_GROUP, (g + 1) * C_GROUP)
            acc = None
            for s in range(w):
                j = nprev + t - s
                slab = prev_ref[j, :, gs] if j < nprev else xc_ref[j - nprev, :, gs]
                acc = slab if acc is None else acc + slab
            cnt = float(min(w, start + t + 1))
            pooled = acc / cnt - xc_ref[t, :, gs]
            o_ref[t, :, gs] = _nn(pooled.astype(BF), wp_ref[g]) * sc_ref[:, gs]
        for g in range(D_GROUPS):
            gs = slice(g * 128, (g + 1) * 128)
            s = None
            for j in range(t + 1):
                term = ws_ref[(g * ntok + t) * ntok + j] * vn_ref[j, :, gs]
                s = term if s is None else s + term
            s = s + bs_ref[g * ntok + t]
            o_ref[t, :, MIX_C + g * 128:MIX_C + (g + 1) * 128] = u_ref[t, :, gs] * s


def _pool_sgu_sample(ws_small, bs_small, prev_t, xc_t, u_t, vn_t, wp, scale, start):
    ntok, nreq, _ = xc_t.shape
    smem = pl.BlockSpec(memory_space=pltpu.SMEM)
    return pl.pallas_call(
        functools.partial(_pool_sgu_sample_kernel, start=start),
        grid=(1,),
        in_specs=[smem, smem, _const_spec(prev_t.shape), _const_spec(xc_t.shape), _const_spec(u_t.shape),
                  _const_spec(vn_t.shape), _const_spec(wp.shape), _const_spec(scale.shape)],
        out_specs=_const_spec((ntok, nreq, 1024)),
        out_shape=jax.ShapeDtypeStruct((ntok, nreq, 1024), F32),
        compiler_params=_params(1, 32),
        name="pool_sgu_sample",
    )(ws_small, bs_small, prev_t, xc_t, u_t, vn_t, wp, scale)


def _even_weights(w_in):
    q_a, k_a, v_a, q_b, k_b, v_b, q_i, k_i, w_i = jnp.split(
        w_in, [512, 1024, 1536, 2048, 2176, 2304, 2816, 2880], axis=1)
    wa = jnp.concatenate([k_a.reshape(D_MODEL, H_A, 2 * DA), v_a.reshape(D_MODEL, H_A, DV_A)],
                         -1).reshape(D_MODEL, H_A * (2 * DA + DV_A))
    wb = jnp.concatenate([k_b, v_b, k_i], 1)
    wq = jnp.concatenate([q_a, q_b, q_i], 1)
    ww = jnp.concatenate([w_i, jnp.zeros((D_MODEL, 128 - HI), w_in.dtype)], 1)
    qscale = jnp.concatenate([jnp.full((512,), DA ** -0.5, F32), jnp.full((512,), DB ** -0.5, F32),
                              jnp.full((512,), DI ** -0.5, F32)]).reshape(1, 1536)
    wwt = jnp.concatenate([w_i.T, jnp.zeros((16 - HI, D_MODEL), w_in.dtype)], 0)
    natural = (wa.astype(BF), wb.astype(BF), wq.astype(BF), ww.astype(BF), qscale)
    feature_major = (wq.T.astype(BF), v_a.T.astype(BF), v_b.T.astype(BF), wwt.astype(BF))
    return natural, feature_major


def _a_pages(x):
    n, page, _ = x.shape
    return x.reshape(n, page, H_A, 2, 128).transpose(0, 1, 3, 2, 4).reshape(n, page * 2 * H_A, 128)


def _pad_rows(x, rows):
    return jnp.pad(x, ((0, 0), (0, rows - x.shape[1]), (0, 0)))


def kernel(x_prompt, x_sample, cache_a, cache_b, state_pool, page_table, w_in_e, lam_e, subln_g, w_out_e,
           w_in_o, w_pool, pool_scale, sgu_g, sgu_b, w_s, b_s, w_out_o, w_mlp1, w_mlp2, ln_g, ln_b):
    batch, seq, _ = x_prompt.shape
    nreq, ntok, _ = x_sample.shape
    npg = page_table.shape[1]
    page = cache_a.shape[2]
    past = npg * page
    xp = x_prompt.reshape(batch * seq, D_MODEL)
    xs = x_sample.reshape(nreq * ntok, D_MODEL)
    outs = {k: [] for k in ("a_p", "b_p", "pool_p", "a_s", "b_s", "pool_s", "v_s")}
    rpad = 8

    for l in range(DEPTH):
        i = l // 2
        row2 = lambda v: v.reshape(1, -1)
        if l % 2 == 0:
            lam_init = 0.8 - 0.6 * math.exp(-0.3 * l)
            (wa, wb, wq, ww, qscale), (wqt, wvat, wvbt, wwt) = _even_weights(w_in_e[i])
            g = row2(subln_g[i])
            w_out = w_out_e[i].astype(BF)
            na, nb, abf, bbf, qt, vat, vbt, wit = _proj_even_t(xp, wa, wb, wqt, wvat, wvbt, wwt)
            o_a = _diff_attn_prompt(lam_e[i], subln_g[i].reshape(-1, 1), qt, abf, vat, batch, seq, lam_init)
            o_b = _sparse_attn_prompt(qt, wit, bbf, vbt, batch, seq, min(TOPK_MAX, seq // 4))
            mix_p = [o_a, o_b]
            outs["a_p"].append(na.reshape(batch, seq, 2, H_A, 128).transpose(0, 1, 3, 2, 4)
                               .reshape(batch, seq, H_A, 2 * DA + DV_A))
            outs["b_p"].append(nb.reshape(batch, seq, 2 * DB + DI))
            nas, nbs, _, _, qs, wis = _proj_even(xs, wa, wb, wq, ww, qscale)
            o_s = _sample_even(
                page_table, lam_e[i], g,
                _pad_rows(qs.astype(F32).reshape(nreq, ntok, 1536), rpad),
                _pad_rows(wis.reshape(nreq, ntok, 128), rpad),
                _a_pages(_pad_rows(nas.reshape(nreq, ntok, 1024), rpad)),
                _pad_rows(nbs.reshape(nreq, ntok, 320), page).transpose(0, 2, 1),
                _a_pages(cache_a[i].reshape(-1, page, 1024)), cache_b[i].transpose(0, 2, 1),
                min(TOPK_MAX, (past + ntok) // 4), lam_init)
            mix_s = [o_s[:, :ntok].reshape(nreq * ntok, 1024)]
            outs["a_s"].append(nas.reshape(nreq, ntok, H_A, 2 * DA + DV_A))
            outs["b_s"].append(nbs.reshape(nreq, ntok, 2 * DB + DI))
        else:
            w_in = w_in_o[i].astype(BF)
            w_out = w_out_o[i].astype(BF)
            wp = w_pool[i].astype(BF)
            scale = row2(pool_scale[i])
            sg, sb = row2(sgu_g[i]), row2(sgu_b[i])
            prev16 = jnp.zeros((batch, POOL_BUF + 1, MIX_C), F32)
            m_p, xc = _odd_mixer_prompt(prev16, xp, w_in, sg, sb, wp, scale, w_s[i], b_s[i].T, batch, seq, 0)
            mix_p = [m_p]
            outs["pool_p"].append(xc.reshape(batch, seq, MIX_C)[:, seq - POOL_BUF:])
            xcs, us, vns = _proj_odd(xs, w_in, sg, sb)
            tmaj = lambda v: v.reshape(nreq, ntok, -1).transpose(1, 0, 2)
            m_t = _pool_sgu_sample(
                w_s[i][:, :ntok, :ntok].reshape(-1), b_s[i][:, :ntok].reshape(-1),
                state_pool[i].transpose(1, 0, 2), tmaj(xcs), tmaj(us), tmaj(vns), wp, scale, past)
            mix_s = [m_t.transpose(1, 0, 2).reshape(nreq * ntok, 1024)]
            ext = jnp.concatenate([state_pool[i], xcs.reshape(nreq, ntok, MIX_C)], 1)
            outs["pool_s"].append(ext[:, ext.shape[1] - POOL_BUF:])
            outs["v_s"].append(vns.reshape(nreq, ntok, MIX_D))
        g0, b0, g1, b1 = row2(ln_g[l, 0]), row2(ln_b[l, 0]), row2(ln_g[l, 1]), row2(ln_b[l, 1])
        w1, w2 = w_mlp1[l].astype(BF), w_mlp2[l].astype(BF)
        xp = _layer_tail(mix_p, w_out, xp, g0, b0, w1, w2, g1, b1)
        xs = _layer_tail(mix_s, w_out, xs, g0, b0, w1, w2, g1, b1)

    st = lambda k: jnp.stack(outs[k])
    return (xp.reshape(batch, seq, D_MODEL), xs.reshape(nreq, ntok, D_MODEL), st("a_p"), st("b_p"),
            st("pool_p"), st("a_s"), st("b_s"), st("pool_s"), st("v_s"))
```
